```python
import math
import jax
import jax.numpy as jnp
from jax import lax
import numpy as np

D_MODEL = 1024
BATCH = 8
SEQ = 2048
DEPTH = 1

GDN_HEADS = 8
GDN_HEAD_DIM = 128
GDN_WIDTH = GDN_HEADS * GDN_HEAD_DIM
GDN_CONV = 4
CHUNK = 64
CONV_WIDTH = 1024
SHORT_CONV = 3
MIX_WIDTH = GDN_WIDTH + CONV_WIDTH
EPS = 1e-6

PROJ_SPLITS = (
    3 * GDN_WIDTH,
    GDN_WIDTH,
    GDN_HEADS,
    GDN_HEADS,
    CONV_WIDTH,
    CONV_WIDTH,
    CONV_WIDTH,
    CONV_WIDTH,
)
PROJ_WIDTH = sum(PROJ_SPLITS)

kernel_name = "hybrid_gdn_shortconv_block"


def rmsnorm(x, w):
    xf = x.astype(jnp.float32)
    xf = xf * lax.rsqrt(jnp.mean(xf * xf, axis=-1, keepdims=True) + EPS)
    return (xf * w.astype(jnp.float32)).astype(x.dtype)


def l2norm(x):
    return x * lax.rsqrt(jnp.sum(x * x, axis=-1, keepdims=True) + EPS)


def causal_depthwise_conv(x, w):
    K = w.shape[0]
    L = x.shape[1]
    xp = jnp.pad(x, ((0, 0), (K - 1, 0), (0, 0)))
    return sum(xp[:, j:j + L] * w[j] for j in range(K))


def gated_delta_rule_chunked(q, k, v, g, beta):
    Bsz, L, H, DK = q.shape
    DV = v.shape[-1]
    n = L // CHUNK

    def chunks(t):
        return t.reshape(Bsz, n, CHUNK, H, -1).transpose(0, 3, 1, 2, 4)

    q, k, v = chunks(q), chunks(k), chunks(v)
    g = g.reshape(Bsz, n, CHUNK, H).transpose(0, 3, 1, 2)
    beta = beta.reshape(Bsz, n, CHUNK, H).transpose(0, 3, 1, 2)
    g = jnp.cumsum(g, axis=-1)

    causal = jnp.tril(jnp.ones((CHUNK, CHUNK), dtype=bool))
    strict = jnp.tril(jnp.ones((CHUNK, CHUNK), dtype=bool), k=-1)
    decay = jnp.exp(jnp.where(causal, g[..., :, None] - g[..., None, :], -jnp.inf))

    k_beta = k * beta[..., None]
    v_beta = v * beta[..., None]
    A = jnp.where(strict, jnp.einsum('bhnid,bhnjd->bhnij', k_beta, k) * decay, 0.0)
    eye = jnp.eye(CHUNK, dtype=q.dtype)
    rhs = jnp.concatenate([v_beta, k_beta * jnp.exp(g)[..., None]], axis=-1)
    sol = lax.linalg.triangular_solve(eye + A, rhs, left_side=True, lower=True,
                                      unit_diagonal=True)
    u = sol[..., :DV]
    w = sol[..., DV:]

    attn_intra = jnp.where(causal, jnp.einsum('bhnid,bhnjd->bhnij', q, k) * decay, 0.0)
    g_last = g[..., -1]
    k_state = k * jnp.exp(g_last[..., None] - g)[..., None]
    q_decay = q * jnp.exp(g)[..., None]

    def step(S, inp):
        qd, w_c, u_c, a_c, ks, gl = inp
        v_new = u_c - jnp.einsum('bhck,bhkv->bhcv', w_c, S)
        o = jnp.einsum('bhck,bhkv->bhcv', qd, S) + jnp.einsum('bhij,bhjv->bhiv', a_c, v_new)
        S = S * jnp.exp(gl)[..., None, None] + jnp.einsum('bhck,bhcv->bhkv', ks, v_new)
        return S, o

    xs = tuple(jnp.moveaxis(t, 2, 0) for t in (q_decay, w, u, attn_intra, k_state, g_last))
    S0 = jnp.zeros((Bsz, H, DK, DV), dtype=q.dtype)
    _, o = lax.scan(step, S0, xs)
    return o.transpose(1, 0, 3, 2, 4).reshape(Bsz, L, H, DV)


def hybrid_layer(x, norm_in_w, w_in, conv_qkv_w, A_log, dt_bias, gdn_norm_w,
                 conv_w, conv_b, w_out):
    Bsz, L, _ = x.shape
    h = rmsnorm(x, norm_in_w)
    proj = h @ w_in
    split_at = [int(i) for i in np.cumsum(PROJ_SPLITS)[:-1]]
    qkv, z_g, b_g, a_g, gate_b, gate_c, h_c, z_c = jnp.split(proj, split_at, axis=-1)

    qkv = jax.nn.silu(causal_depthwise_conv(qkv, conv_qkv_w))
    q, k, v = jnp.split(qkv, 3, axis=-1)
    shp = (Bsz, L, GDN_HEADS, GDN_HEAD_DIM)
    q = l2norm(q.reshape(shp).astype(jnp.float32)) * (GDN_HEAD_DIM ** -0.5)
    k = l2norm(k.reshape(shp).astype(jnp.float32))
    v = v.reshape(shp).astype(jnp.float32)
    beta = jax.nn.sigmoid(b_g.astype(jnp.float32))
    g = -jnp.exp(A_log.astype(jnp.float32)) * jax.nn.softplus(
        a_g.astype(jnp.float32) + dt_bias.astype(jnp.float32))
    o = gated_delta_rule_chunked(q, k, v, g, beta).astype(x.dtype)
    o = rmsnorm(o, gdn_norm_w) * jax.nn.silu(z_g.reshape(shp))
    o = o.reshape(Bsz, L, GDN_WIDTH)

    y_c = gate_b * (causal_depthwise_conv(gate_c * h_c, conv_w) + conv_b)
    y_c = y_c * jax.nn.silu(z_c)

    mix = jnp.concatenate([o, y_c], axis=-1)
    return x + mix @ w_out


def setup_inputs(seed: int = 0) -> dict:
    key = jax.random.key(seed)
    ks = jax.random.split(key, 12)
    f32 = jnp.float32
    x = jax.random.normal(ks[0], (BATCH, SEQ, D_MODEL), f32)
    norm_in_w = 1.0 + 0.02 * jax.random.normal(ks[1], (DEPTH, D_MODEL), f32)
    w_in = jax.random.normal(ks[2], (DEPTH, D_MODEL, PROJ_WIDTH), f32) * D_MODEL ** -0.5
    conv_qkv_w = jax.random.normal(ks[3], (DEPTH, GDN_CONV, 3 * GDN_WIDTH), f32) * GDN_CONV ** -0.5
    A_log = jnp.log(jax.random.uniform(ks[4], (DEPTH, GDN_HEADS), f32, minval=1.0, maxval=16.0))
    dt = jnp.exp(jax.random.uniform(ks[5], (DEPTH, GDN_HEADS), f32,
                                    minval=math.log(1e-3), maxval=math.log(1e-1)))
    dt_bias = dt + jnp.log(-jnp.expm1(-dt))
    gdn_norm_w = 1.0 + 0.02 * jax.random.normal(ks[6], (DEPTH, GDN_HEAD_DIM), f32)
    conv_w = jax.random.normal(ks[7], (DEPTH, SHORT_CONV, CONV_WIDTH), f32) * SHORT_CONV ** -0.5
    conv_b = 0.01 * jax.random.normal(ks[8], (DEPTH, CONV_WIDTH), f32)
    w_out = jax.random.normal(ks[9], (DEPTH, MIX_WIDTH, D_MODEL), f32) * MIX_WIDTH ** -0.5
    final_norm_w = 1.0 + 0.02 * jax.random.normal(ks[10], (D_MODEL,), f32)
    return {"x": x, "norm_in_w": norm_in_w, "w_in": w_in, "conv_qkv_w": conv_qkv_w,
            "A_log": A_log, "dt_bias": dt_bias, "gdn_norm_w": gdn_norm_w,
            "conv_w": conv_w, "conv_b": conv_b, "w_out": w_out,
            "final_norm_w": final_norm_w}


def reference(x, norm_in_w, w_in, conv_qkv_w, A_log, dt_bias, gdn_norm_w,
              conv_w, conv_b, w_out, final_norm_w):
    for layer in range(DEPTH):
        x = hybrid_layer(x, norm_in_w[layer], w_in[layer], conv_qkv_w[layer],
                         A_log[layer], dt_bias[layer], gdn_norm_w[layer],
                         conv_w[layer], conv_b[layer], w_out[layer])
    return rmsnorm(x, final_norm_w)
```

```python
import functools

import jax
import jax.numpy as jnp
from jax import lax
from jax.experimental import pallas as pl
from jax.experimental.pallas import tpu as pltpu

D_MODEL = 1024
HEADS = 8
HEAD_DIM = 128
GDN_WIDTH = HEADS * HEAD_DIM
CONV_WIDTH = 1024
GDN_CONV = 4
SHORT_CONV = 3
CHUNK = 64
EPS = 1e-6
LANES = 128
HALO = 8

COL_QKV = 0
COL_Z = 3 * GDN_WIDTH
COL_B = COL_Z + GDN_WIDTH
COL_C = COL_B + CONV_WIDTH
COL_H = COL_C + CONV_WIDTH
COL_ZC = COL_H + CONV_WIDTH
COL_BETA = COL_ZC + CONV_WIDTH
COL_A = COL_BETA + LANES
PROJ_PACKED = COL_A + LANES

IN_TM = 1024
IN_TN = PROJ_PACKED // 6
GDN_TL = 256
OUT_TL = 512
VMEM_LIMIT = 56 * 1024 * 1024


def _bdot(a, b):
    return jnp.dot(a.astype(jnp.bfloat16), b.astype(jnp.bfloat16),
                   preferred_element_type=jnp.float32)


def _bdot_nt(a, b):
    return lax.dot_general(a.astype(jnp.bfloat16), b.astype(jnp.bfloat16),
                           (((1,), (1,)), ((), ())), preferred_element_type=jnp.float32)


def _bdot_tn(a, b):
    return lax.dot_general(a.astype(jnp.bfloat16), b.astype(jnp.bfloat16),
                           (((0,), (0,)), ((), ())), preferred_element_type=jnp.float32)


def _silu(x):
    return x * jax.nn.sigmoid(x)


def _shift_rows(x, halo, s):
    r = pltpu.roll(x, s, axis=0)
    hr = pltpu.roll(halo, s, axis=0)
    rows = lax.broadcasted_iota(jnp.int32, hr.shape, 0)
    top = jnp.where(rows < s, hr, r[0:HALO])
    return jnp.concatenate([top, r[HALO:]], axis=0)


def _causal_conv(x, halo, w_ref, cols, width):
    acc = x * w_ref[width - 1:width, cols]
    for s in range(1, width):
        acc = acc + _shift_rows(x, halo, s) * w_ref[width - 1 - s:width - s, cols]
    return acc


def _in_proj_kernel(x_ref, nw_ref, w_ref, o_ref, h_s):
    @pl.when(pl.program_id(1) == 0)
    def _():
        x = x_ref[...]
        ms = jnp.mean(x * x, axis=-1, keepdims=True)
        h_s[...] = (x * lax.rsqrt(ms + EPS) * nw_ref[...]).astype(jnp.bfloat16)

    o_ref[...] = jnp.dot(h_s[...], w_ref[...], preferred_element_type=jnp.float32)


def _in_proj(x2d, norm_w, w_packed):
    tokens = x2d.shape[0]
    return pl.pallas_call(
        _in_proj_kernel,
        grid=(tokens // IN_TM, PROJ_PACKED // IN_TN),
        in_specs=[
            pl.BlockSpec((IN_TM, D_MODEL), lambda i, j: (i, 0)),
            pl.BlockSpec((1, D_MODEL), lambda i, j: (0, 0)),
            pl.BlockSpec((D_MODEL, IN_TN), lambda i, j: (0, j)),
        ],
        out_specs=pl.BlockSpec((IN_TM, IN_TN), lambda i, j: (i, j)),
        out_shape=jax.ShapeDtypeStruct((tokens, PROJ_PACKED), jnp.float32),
        scratch_shapes=[pltpu.VMEM((IN_TM, D_MODEL), jnp.bfloat16)],
        compiler_params=pltpu.CompilerParams(
            dimension_semantics=("arbitrary", "arbitrary"), vmem_limit_bytes=VMEM_LIMIT),
        name="in_proj",
    )(x2d, norm_w, w_packed)


def _gdn_kernel(qkv_ref, halo_ref, z_ref, bl_ref, al_ref, cw_ref, alog_ref, dtb_ref, gnw_ref,
                o_ref, s_ref, q_s, k_s, v_s, beta_s, gc_s, gct_s):
    t = pl.program_id(1)
    n_chunks = GDN_TL // CHUNK

    @pl.when(t == 0)
    def _():
        s_ref[...] = jnp.zeros_like(s_ref)

    halo = jnp.where(t > 0, halo_ref[...], 0.0)

    for grp, dst in enumerate((q_s, k_s, v_s)):
        cols = slice(grp * GDN_WIDTH, (grp + 1) * GDN_WIDTH)
        act = _silu(_causal_conv(qkv_ref[:, cols], halo[:, cols], cw_ref, cols, GDN_CONV))
        if grp == 2:
            dst[...] = act
        else:
            scale = HEAD_DIM ** -0.5 if grp == 0 else 1.0
            for h in range(HEADS):
                blk = act[:, h * HEAD_DIM:(h + 1) * HEAD_DIM]
                inv = lax.rsqrt(jnp.sum(blk * blk, axis=-1, keepdims=True) + EPS)
                dst[:, h * HEAD_DIM:(h + 1) * HEAD_DIM] = blk * inv * scale

    beta_s[...] = jax.nn.sigmoid(bl_ref[...])
    a_in = al_ref[...] + dtb_ref[...]
    softplus = jnp.maximum(a_in, 0.0) + jnp.log1p(jnp.exp(-jnp.abs(a_in)))
    gc = -jnp.exp(alog_ref[...]) * softplus
    pos = lax.broadcasted_iota(jnp.int32, gc.shape, 0) % CHUNK
    s = 1
    while s < CHUNK:
        gc = gc + jnp.where(pos >= s, pltpu.roll(gc, s, axis=0), 0.0)
        s *= 2
    gc_s[...] = gc
    gct = gc.T
    for c in range(n_chunks):
        gct_s[c] = gct[:, c * CHUNK:(c + 1) * CHUNK]

    ri = lax.broadcasted_iota(jnp.int32, (CHUNK, CHUNK), 0)
    ci = lax.broadcasted_iota(jnp.int32, (CHUNK, CHUNK), 1)
    causal = ri >= ci
    strict = ri > ci
    eye = (ri == ci).astype(jnp.float32)

    def chunk_body(c, carry):
        r0 = pl.multiple_of(c * CHUNK, CHUNK)
        rows = pl.ds(r0, CHUNK)
        beta_blk = beta_s[rows, :]
        gc_blk = gc_s[rows, :]
        g_last = gc_blk[CHUNK - 1:CHUNK, :]
        eg_blk = jnp.exp(gc_blk)
        ek_blk = jnp.exp(g_last - gc_blk)
        egl_blk = jnp.exp(g_last)
        gct_blk = gct_s[c]
        for h in range(HEADS):
            lanes = slice(h * HEAD_DIM, (h + 1) * HEAD_DIM)
            qh = q_s[rows, lanes]
            kh = k_s[rows, lanes]
            vh = v_s[rows, lanes]
            beta = beta_blk[:, h:h + 1]
            g_col = gc_blk[:, h:h + 1]
            g_row = gct_blk[h:h + 1, :]
            decay = jnp.where(causal, jnp.exp(g_col - g_row), 0.0)
            kb = kh * beta
            vb = vh * beta
            a_mat = jnp.where(strict, _bdot_nt(kb, kh) * decay, 0.0)
            t_mat = eye - a_mat
            p = a_mat
            n = 2
            while n < CHUNK:
                p = _bdot(p, p)
                t_mat = t_mat + _bdot(t_mat, p)
                n *= 2
            rhs = jnp.concatenate([vb, kb * eg_blk[:, h:h + 1]], axis=1)
            sol = _bdot(t_mat, rhs)
            u = sol[:, :HEAD_DIM]
            w = sol[:, HEAD_DIM:]
            attn = _bdot_nt(qh, kh) * decay
            qd = qh * eg_blk[:, h:h + 1]
            ks = kh * ek_blk[:, h:h + 1]
            state = s_ref[h]
            v_new = u - _bdot(w, state)
            o = _bdot(qd, state) + _bdot(attn, v_new)
            s_ref[h] = state * egl_blk[:, h:h + 1] + _bdot_tn(ks, v_new)
            ms = jnp.mean(o * o, axis=-1, keepdims=True)
            o = o * lax.rsqrt(ms + EPS) * gnw_ref[...]
            o_ref[rows, lanes] = o * _silu(z_ref[rows, lanes])
        return carry

    lax.fori_loop(0, n_chunks, chunk_body, 0)


def _gdn(proj, conv_qkv_w, a_log_pad, dt_bias_pad, gdn_norm_w, batch, seq):
    tokens = batch * seq
    tiles = seq // GDN_TL
    n_chunks = GDN_TL // CHUNK

    def row_map(col_block):
        return lambda b, t: (b * tiles + t, col_block)

    def halo_map(b, t):
        return (jnp.maximum((b * seq + t * GDN_TL) // HALO - 1, 0), 0)

    const = lambda b, t: (0, 0)
    return pl.pallas_call(
        _gdn_kernel,
        grid=(batch, tiles),
        in_specs=[
            pl.BlockSpec((GDN_TL, 3 * GDN_WIDTH), row_map(0)),
            pl.BlockSpec((HALO, 3 * GDN_WIDTH), halo_map),
            pl.BlockSpec((GDN_TL, GDN_WIDTH), row_map(COL_Z // GDN_WIDTH)),
            pl.BlockSpec((GDN_TL, LANES), row_map(COL_BETA // LANES)),
            pl.BlockSpec((GDN_TL, LANES), row_map(COL_A // LANES)),
            pl.BlockSpec((GDN_CONV, 3 * GDN_WIDTH), const),
            pl.BlockSpec((1, LANES), const),
            pl.BlockSpec((1, LANES), const),
            pl.BlockSpec((1, HEAD_DIM), const),
        ],
        out_specs=pl.BlockSpec((GDN_TL, GDN_WIDTH), lambda b, t: (b * tiles + t, 0)),
        out_shape=jax.ShapeDtypeStruct((tokens, GDN_WIDTH), jnp.float32),
        scratch_shapes=[
            pltpu.VMEM((HEADS, HEAD_DIM, HEAD_DIM), jnp.float32),
            pltpu.VMEM((GDN_TL, GDN_WIDTH), jnp.float32),
            pltpu.VMEM((GDN_TL, GDN_WIDTH), jnp.float32),
            pltpu.VMEM((GDN_TL, GDN_WIDTH), jnp.float32),
            pltpu.VMEM((GDN_TL, LANES), jnp.float32),
            pltpu.VMEM((GDN_TL, LANES), jnp.float32),
            pltpu.VMEM((n_chunks, LANES, CHUNK), jnp.float32),
        ],
        compiler_params=pltpu.CompilerParams(
            dimension_semantics=("arbitrary", "arbitrary"), vmem_limit_bytes=VMEM_LIMIT),
        name="gdn",
    )(proj, proj, proj, proj, proj, conv_qkv_w, a_log_pad, dt_bias_pad, gdn_norm_w)


def _out_proj_kernel(x_ref, o_ref, gb_ref, gc_ref, hc_ref, zc_ref, gc_halo_ref, hc_halo_ref,
                     cw_ref, cb_ref, wo_ref, fnw_ref, out_ref):
    t = pl.program_id(1)
    ch = gc_ref[...] * hc_ref[...]
    halo = jnp.where(t > 0, gc_halo_ref[...] * hc_halo_ref[...], 0.0)
    conv = _causal_conv(ch, halo, cw_ref, slice(None), SHORT_CONV)
    y = gb_ref[...] * (conv + cb_ref[...]) * _silu(zc_ref[...])
    res = (x_ref[...] + _bdot(o_ref[...], wo_ref[:GDN_WIDTH, :])
           + _bdot(y, wo_ref[GDN_WIDTH:, :]))
    ms = jnp.mean(res * res, axis=-1, keepdims=True)
    out_ref[...] = res * lax.rsqrt(ms + EPS) * fnw_ref[...]


def _out_proj(x2d, o_gdn, proj, conv_w, conv_b, w_out, final_norm_w, batch, seq):
    tokens = batch * seq
    tiles = seq // OUT_TL

    def row_map(col_block):
        return lambda b, t: (b * tiles + t, col_block)

    def halo_map(col_block):
        return lambda b, t: (jnp.maximum((b * seq + t * OUT_TL) // HALO - 1, 0), col_block)

    const = lambda b, t: (0, 0)
    wide = CONV_WIDTH
    return pl.pallas_call(
        _out_proj_kernel,
        grid=(batch, tiles),
        in_specs=[
            pl.BlockSpec((OUT_TL, D_MODEL), row_map(0)),
            pl.BlockSpec((OUT_TL, GDN_WIDTH), row_map(0)),
            pl.BlockSpec((OUT_TL, wide), row_map(COL_B // wide)),
            pl.BlockSpec((OUT_TL, wide), row_map(COL_C // wide)),
            pl.BlockSpec((OUT_TL, wide), row_map(COL_H // wide)),
            pl.BlockSpec((OUT_TL, wide), row_map(COL_ZC // wide)),
            pl.BlockSpec((HALO, wide), halo_map(COL_C // wide)),
            pl.BlockSpec((HALO, wide), halo_map(COL_H // wide)),
            pl.BlockSpec((SHORT_CONV, wide), const),
            pl.BlockSpec((1, wide), const),
            pl.BlockSpec((GDN_WIDTH + CONV_WIDTH, D_MODEL), const),
            pl.BlockSpec((1, D_MODEL), const),
        ],
        out_specs=pl.BlockSpec((OUT_TL, D_MODEL), row_map(0)),
        out_shape=jax.ShapeDtypeStruct((tokens, D_MODEL), jnp.float32),
        compiler_params=pltpu.CompilerParams(
            dimension_semantics=("arbitrary", "arbitrary"), vmem_limit_bytes=VMEM_LIMIT),
        name="out_proj",
    )(x2d, o_gdn, proj, proj, proj, proj, proj, proj, conv_w, conv_b, w_out, final_norm_w)


def _pad_lanes(v):
    return jnp.pad(v.reshape(1, -1), ((0, 0), (0, LANES - v.shape[-1])))


def _layer(x2d, batch, seq, norm_in_w, w_in, conv_qkv_w, a_log, dt_bias, gdn_norm_w,
           conv_w, conv_b, w_out, out_norm_w):
    n_main = COL_BETA
    w_main = jnp.concatenate([w_in[:, :COL_B], w_in[:, COL_B + 2 * HEADS:]], axis=1)
    w_beta = jnp.pad(w_in[:, COL_B:COL_B + HEADS], ((0, 0), (0, LANES - HEADS)))
    w_a = jnp.pad(w_in[:, COL_B + HEADS:COL_B + 2 * HEADS], ((0, 0), (0, LANES - HEADS)))
    w_packed = jnp.concatenate([w_main, w_beta, w_a], axis=1).astype(jnp.bfloat16)
    assert w_main.shape[1] == n_main and w_packed.shape[1] == PROJ_PACKED

    proj = _in_proj(x2d, norm_in_w.reshape(1, -1), w_packed)
    o_gdn = _gdn(proj, conv_qkv_w, _pad_lanes(a_log), _pad_lanes(dt_bias),
                 gdn_norm_w.reshape(1, -1), batch, seq)
    return _out_proj(x2d, o_gdn, proj, conv_w, conv_b.reshape(1, -1),
                     w_out.astype(jnp.bfloat16), out_norm_w.reshape(1, -1), batch, seq)


def kernel(x, norm_in_w, w_in, conv_qkv_w, A_log, dt_bias, gdn_norm_w, conv_w, conv_b, w_out,
           final_norm_w):
    batch, seq, d_model = x.shape
    depth = norm_in_w.shape[0]
    assert depth == 1 and d_model == D_MODEL
    assert seq % GDN_TL == 0 and seq % OUT_TL == 0 and (batch * seq) % IN_TM == 0
    x2d = x.reshape(batch * seq, d_model)
    out = _layer(x2d, batch, seq, norm_in_w[0], w_in[0], conv_qkv_w[0], A_log[0], dt_bias[0],
                 gdn_norm_w[0], conv_w[0], conv_b[0], w_out[0], final_norm_w)
    return out.reshape(batch, seq, d_model)
```

```python
import functools

import jax
import jax.numpy as jnp
from jax import lax
from jax.experimental import pallas as pl
from jax.experimental.pallas import tpu as pltpu

D_MODEL = 1024
HEADS = 8
HEAD_DIM = 128
GDN_WIDTH = HEADS * HEAD_DIM
CONV_WIDTH = 1024
GDN_CONV = 4
SHORT_CONV = 3
CHUNK = 64
EPS = 1e-6
LANES = 128
HALO = 8

COL_QKV = 0
COL_Z = 3 * GDN_WIDTH
COL_B = COL_Z + GDN_WIDTH
COL_C = COL_B + CONV_WIDTH
COL_H = COL_C + CONV_WIDTH
COL_ZC = COL_H + CONV_WIDTH
COL_BETA = COL_ZC + CONV_WIDTH
COL_A = COL_BETA + LANES
PROJ_PACKED = COL_A + LANES

IN_TM = 1024
IN_TN = PROJ_PACKED // 6
GDN_TL = 256
OUT_TL = 512
VMEM_LIMIT = 56 * 1024 * 1024


def _bdot(a, b):
    return jnp.dot(a.astype(jnp.bfloat16), b.astype(jnp.bfloat16),
                   preferred_element_type=jnp.float32)


def _bdot_nt(a, b):
    return lax.dot_general(a.astype(jnp.bfloat16), b.astype(jnp.bfloat16),
                           (((1,), (1,)), ((), ())), preferred_element_type=jnp.float32)


def _bdot_tn(a, b):
    return lax.dot_general(a.astype(jnp.bfloat16), b.astype(jnp.bfloat16),
                           (((0,), (0,)), ((), ())), preferred_element_type=jnp.float32)


def _silu(x):
    return x * jax.nn.sigmoid(x)


def _shift_rows(x, halo, s):
    r = pltpu.roll(x, s, axis=0)
    hr = pltpu.roll(halo, s, axis=0)
    rows = lax.broadcasted_iota(jnp.int32, hr.shape, 0)
    top = jnp.where(rows < s, hr, r[0:HALO])
    return jnp.concatenate([top, r[HALO:]], axis=0)


def _causal_conv(x, halo, w_ref, cols, width):
    acc = x * w_ref[width - 1:width, cols]
    for s in range(1, width):
        acc = acc + _shift_rows(x, halo, s) * w_ref[width - 1 - s:width - s, cols]
    return acc


def _in_proj_kernel(x_ref, nw_ref, w_ref, o_ref, h_s):
    @pl.when(pl.program_id(1) == 0)
    def _():
        x = x_ref[...]
        ms = jnp.mean(x * x, axis=-1, keepdims=True)
        h_s[...] = (x * lax.rsqrt(ms + EPS) * nw_ref[...]).astype(jnp.bfloat16)

    o_ref[...] = jnp.dot(h_s[...], w_ref[...], preferred_element_type=jnp.float32)


def _in_proj(x2d, norm_w, w_packed):
    tokens = x2d.shape[0]
    return pl.pallas_call(
        _in_proj_kernel,
        grid=(tokens // IN_TM, PROJ_PACKED // IN_TN),
        in_specs=[
            pl.BlockSpec((IN_TM, D_MODEL), lambda i, j: (i, 0)),
            pl.BlockSpec((1, D_MODEL), lambda i, j: (0, 0)),
            pl.BlockSpec((D_MODEL, IN_TN), lambda i, j: (0, j)),
        ],
        out_specs=pl.BlockSpec((IN_TM, IN_TN), lambda i, j: (i, j)),
        out_shape=jax.ShapeDtypeStruct((tokens, PROJ_PACKED), jnp.float32),
        scratch_shapes=[pltpu.VMEM((IN_TM, D_MODEL), jnp.bfloat16)],
        compiler_params=pltpu.CompilerParams(
            dimension_semantics=("arbitrary", "arbitrary"), vmem_limit_bytes=VMEM_LIMIT),
        name="in_proj",
    )(x2d, norm_w, w_packed)


def _gdn_kernel(qkv_ref, halo_ref, z_ref, bl_ref, al_ref, cw_ref, alog_ref, dtb_ref, gnw_ref,
                o_ref, s_ref, q_s, k_s, v_s, beta_s, gc_s, gct_s):
    t = pl.program_id(1)
    n_chunks = GDN_TL // CHUNK

    @pl.when(t == 0)
    def _():
        s_ref[...] = jnp.zeros_like(s_ref)

    halo = jnp.where(t > 0, halo_ref[...], 0.0)

    for grp, dst in enumerate((q_s, k_s, v_s)):
        cols = slice(grp * GDN_WIDTH, (grp + 1) * GDN_WIDTH)
        act = _silu(_causal_conv(qkv_ref[:, cols], halo[:, cols], cw_ref, cols, GDN_CONV))
        if grp == 2:
            dst[...] = act
        else:
            scale = HEAD_DIM ** -0.5 if grp == 0 else 1.0
            for h in range(HEADS):
                blk = act[:, h * HEAD_DIM:(h + 1) * HEAD_DIM]
                inv = lax.rsqrt(jnp.sum(blk * blk, axis=-1, keepdims=True) + EPS)
                dst[:, h * HEAD_DIM:(h + 1) * HEAD_DIM] = blk * inv * scale

    beta_s[...] = jax.nn.sigmoid(bl_ref[...])
    a_in = al_ref[...] + dtb_ref[...]
    softplus = jnp.maximum(a_in, 0.0) + jnp.log1p(jnp.exp(-jnp.abs(a_in)))
    gc = -jnp.exp(alog_ref[...]) * softplus
    pos = lax.broadcasted_iota(jnp.int32, gc.shape, 0) % CHUNK
    s = 1
    while s < CHUNK:
        gc = gc + jnp.where(pos >= s, pltpu.roll(gc, s, axis=0), 0.0)
        s *= 2
    gc_s[...] = gc
    gct = gc.T
    for c in range(n_chunks):
        gct_s[c] = gct[:, c * CHUNK:(c + 1) * CHUNK]

    ri = lax.broadcasted_iota(jnp.int32, (CHUNK, CHUNK), 0)
    ci = lax.broadcasted_iota(jnp.int32, (CHUNK, CHUNK), 1)
    causal = ri >= ci
    strict = ri > ci
    eye = (ri == ci).astype(jnp.float32)

    def chunk_body(c, carry):
        r0 = pl.multiple_of(c * CHUNK, CHUNK)
        rows = pl.ds(r0, CHUNK)
        beta_blk = beta_s[rows, :]
        gc_blk = gc_s[rows, :]
        g_last = gc_blk[CHUNK - 1:CHUNK, :]
        eg_blk = jnp.exp(gc_blk)
        ek_blk = jnp.exp(g_last - gc_blk)
        egl_blk = jnp.exp(g_last)
        gct_blk = gct_s[c]
        heads = range(HEADS)
        lanes = [slice(h * HEAD_DIM, (h + 1) * HEAD_DIM) for h in heads]
        col = lambda blk, h: blk[:, h:h + 1]
        q = [q_s[rows, lanes[h]] for h in heads]
        k = [k_s[rows, lanes[h]] for h in heads]
        kb = [k[h] * col(beta_blk, h) for h in heads]
        decay = [jnp.where(causal, jnp.exp(col(gc_blk, h) - gct_blk[h:h + 1, :]), 0.0)
                 for h in heads]
        kk = [_bdot_nt(kb[h], k[h]) for h in heads]
        qk = [_bdot_nt(q[h], k[h]) for h in heads]
        p = [jnp.where(strict, kk[h] * decay[h], 0.0) for h in heads]
        t_mat = [eye - p[h] for h in heads]
        n = 2
        while n < CHUNK:
            p = [_bdot(p[h], p[h]) for h in heads]
            t_mat = [t_mat[h] + _bdot(t_mat[h], p[h]) for h in heads]
            n *= 2
        rhs = [jnp.concatenate([v_s[rows, lanes[h]] * col(beta_blk, h),
                                kb[h] * col(eg_blk, h)], axis=1) for h in heads]
        sol = [_bdot(t_mat[h], rhs[h]) for h in heads]
        state = [s_ref[h] for h in heads]
        ws = [_bdot(sol[h][:, HEAD_DIM:], state[h]) for h in heads]
        qs = [_bdot(q[h] * col(eg_blk, h), state[h]) for h in heads]
        v_new = [sol[h][:, :HEAD_DIM] - ws[h] for h in heads]
        o = [qs[h] + _bdot(qk[h] * decay[h], v_new[h]) for h in heads]
        kv = [_bdot_tn(k[h] * col(ek_blk, h), v_new[h]) for h in heads]
        for h in heads:
            s_ref[h] = state[h] * col(egl_blk, h) + kv[h]
            ms = jnp.mean(o[h] * o[h], axis=-1, keepdims=True)
            o_ref[rows, lanes[h]] = (o[h] * lax.rsqrt(ms + EPS) * gnw_ref[...]
                                     * _silu(z_ref[rows, lanes[h]]))
        return carry

    lax.fori_loop(0, n_chunks, chunk_body, 0)


def _gdn(proj, conv_qkv_w, a_log_pad, dt_bias_pad, gdn_norm_w, batch, seq):
    tokens = batch * seq
    tiles = seq // GDN_TL
    n_chunks = GDN_TL // CHUNK

    def row_map(col_block):
        return lambda b, t: (b * tiles + t, col_block)

    def halo_map(b, t):
        return (jnp.maximum((b * seq + t * GDN_TL) // HALO - 1, 0), 0)

    const = lambda b, t: (0, 0)
    return pl.pallas_call(
        _gdn_kernel,
        grid=(batch, tiles),
        in_specs=[
            pl.BlockSpec((GDN_TL, 3 * GDN_WIDTH), row_map(0)),
            pl.BlockSpec((HALO, 3 * GDN_WIDTH), halo_map),
            pl.BlockSpec((GDN_TL, GDN_WIDTH), row_map(COL_Z // GDN_WIDTH)),
            pl.BlockSpec((GDN_TL, LANES), row_map(COL_BETA // LANES)),
            pl.BlockSpec((GDN_TL, LANES), row_map(COL_A // LANES)),
            pl.BlockSpec((GDN_CONV, 3 * GDN_WIDTH), const),
            pl.BlockSpec((1, LANES), const),
            pl.BlockSpec((1, LANES), const),
            pl.BlockSpec((1, HEAD_DIM), const),
        ],
        out_specs=pl.BlockSpec((GDN_TL, GDN_WIDTH), lambda b, t: (b * tiles + t, 0)),
        out_shape=jax.ShapeDtypeStruct((tokens, GDN_WIDTH), jnp.float32),
        scratch_shapes=[
            pltpu.VMEM((HEADS, HEAD_DIM, HEAD_DIM), jnp.float32),
            pltpu.VMEM((GDN_TL, GDN_WIDTH), jnp.float32),
            pltpu.VMEM((GDN_TL, GDN_WIDTH), jnp.float32),
            pltpu.VMEM((GDN_TL, GDN_WIDTH), jnp.float32),
            pltpu.VMEM((GDN_TL, LANES), jnp.float32),
            pltpu.VMEM((GDN_TL, LANES), jnp.float32),
            pltpu.VMEM((n_chunks, LANES, CHUNK), jnp.float32),
        ],
        compiler_params=pltpu.CompilerParams(
            dimension_semantics=("arbitrary", "arbitrary"), vmem_limit_bytes=VMEM_LIMIT),
        name="gdn",
    )(proj, proj, proj, proj, proj, conv_qkv_w, a_log_pad, dt_bias_pad, gdn_norm_w)


def _out_proj_kernel(x_ref, o_ref, gb_ref, gc_ref, hc_ref, zc_ref, gc_halo_ref, hc_halo_ref,
                     cw_ref, cb_ref, wo_ref, fnw_ref, out_ref):
    t = pl.program_id(1)
    ch = gc_ref[...] * hc_ref[...]
    halo = jnp.where(t > 0, gc_halo_ref[...] * hc_halo_ref[...], 0.0)
    conv = _causal_conv(ch, halo, cw_ref, slice(None), SHORT_CONV)
    y = gb_ref[...] * (conv + cb_ref[...]) * _silu(zc_ref[...])
    res = (x_ref[...] + _bdot(o_ref[...], wo_ref[:GDN_WIDTH, :])
           + _bdot(y, wo_ref[GDN_WIDTH:, :]))
    ms = jnp.mean(res * res, axis=-1, keepdims=True)
    out_ref[...] = res * lax.rsqrt(ms + EPS) * fnw_ref[...]


def _out_proj(x2d, o_gdn, proj, conv_w, conv_b, w_out, final_norm_w, batch, seq):
    tokens = batch * seq
    tiles = seq // OUT_TL

    def row_map(col_block):
        return lambda b, t: (b * tiles + t, col_block)

    def halo_map(col_block):
        return lambda b, t: (jnp.maximum((b * seq + t * OUT_TL) // HALO - 1, 0), col_block)

    const = lambda b, t: (0, 0)
    wide = CONV_WIDTH
    return pl.pallas_call(
        _out_proj_kernel,
        grid=(batch, tiles),
        in_specs=[
            pl.BlockSpec((OUT_TL, D_MODEL), row_map(0)),
            pl.BlockSpec((OUT_TL, GDN_WIDTH), row_map(0)),
            pl.BlockSpec((OUT_TL, wide), row_map(COL_B // wide)),
            pl.BlockSpec((OUT_TL, wide), row_map(COL_C // wide)),
            pl.BlockSpec((OUT_TL, wide), row_map(COL_H // wide)),
            pl.BlockSpec((OUT_TL, wide), row_map(COL_ZC // wide)),
            pl.BlockSpec((HALO, wide), halo_map(COL_C // wide)),
            pl.BlockSpec((HALO, wide), halo_map(COL_H // wide)),
            pl.BlockSpec((SHORT_CONV, wide), const),
            pl.BlockSpec((1, wide), const),
            pl.BlockSpec((GDN_WIDTH + CONV_WIDTH, D_MODEL), const),
            pl.BlockSpec((1, D_MODEL), const),
        ],
        out_specs=pl.BlockSpec((OUT_TL, D_MODEL), row_map(0)),
        out_shape=jax.ShapeDtypeStruct((tokens, D_MODEL), jnp.float32),
        compiler_params=pltpu.CompilerParams(
            dimension_semantics=("arbitrary", "arbitrary"), vmem_limit_bytes=VMEM_LIMIT),
        name="out_proj",
    )(x2d, o_gdn, proj, proj, proj, proj, proj, proj, conv_w, conv_b, w_out, final_norm_w)


def _pad_lanes(v):
    return jnp.pad(v.reshape(1, -1), ((0, 0), (0, LANES - v.shape[-1])))


def _layer(x2d, batch, seq, norm_in_w, w_in, conv_qkv_w, a_log, dt_bias, gdn_norm_w,
           conv_w, conv_b, w_out, out_norm_w):
    n_main = COL_BETA
    w_main = jnp.concatenate([w_in[:, :COL_B], w_in[:, COL_B + 2 * HEADS:]], axis=1)
    w_beta = jnp.pad(w_in[:, COL_B:COL_B + HEADS], ((0, 0), (0, LANES - HEADS)))
    w_a = jnp.pad(w_in[:, COL_B + HEADS:COL_B + 2 * HEADS], ((0, 0), (0, LANES - HEADS)))
    w_packed = jnp.concatenate([w_main, w_beta, w_a], axis=1).astype(jnp.bfloat16)
    assert w_main.shape[1] == n_main and w_packed.shape[1] == PROJ_PACKED

    proj = _in_proj(x2d, norm_in_w.reshape(1, -1), w_packed)
    o_gdn = _gdn(proj, conv_qkv_w, _pad_lanes(a_log), _pad_lanes(dt_bias),
                 gdn_norm_w.reshape(1, -1), batch, seq)
    return _out_proj(x2d, o_gdn, proj, conv_w, conv_b.reshape(1, -1),
                     w_out.astype(jnp.bfloat16), out_norm_w.reshape(1, -1), batch, seq)


def kernel(x, norm_in_w, w_in, conv_qkv_w, A_log, dt_bias, gdn_norm_w, conv_w, conv_b, w_out,
           final_norm_w):
    batch, seq, d_model = x.shape
    depth = norm_in_w.shape[0]
    assert depth == 1 and d_model == D_MODEL
    assert seq % GDN_TL == 0 and seq % OUT_TL == 0 and (batch * seq) % IN_TM == 0
    x2d = x.reshape(batch * seq, d_model)
    out = _layer(x2d, batch, seq, norm_in_w[0], w_in[0], conv_qkv_w[0], A_log[0], dt_bias[0],
                 gdn_norm_w[0], conv_w[0], conv_b[0], w_out[0], final_norm_w)
    return out.reshape(batch, seq, d_model)
```

```python
import functools

import jax
import jax.numpy as jnp
from jax import lax
from jax.experimental import pallas as pl
from jax.experimental.pallas import tpu as pltpu

D_MODEL = 1024
HEADS = 8
HEAD_DIM = 128
GDN_WIDTH = HEADS * HEAD_DIM
CONV_WIDTH = 1024
GDN_CONV = 4
SHORT_CONV = 3
CHUNK = 64
EPS = 1e-6
LANES = 128
HALO = 8

COL_QKV = 0
COL_Z = 3 * GDN_WIDTH
COL_B = COL_Z + GDN_WIDTH
COL_C = COL_B + CONV_WIDTH
COL_H = COL_C + CONV_WIDTH
COL_ZC = COL_H + CONV_WIDTH
COL_BETA = COL_ZC + CONV_WIDTH
COL_A = COL_BETA + LANES
PROJ_PACKED = COL_A + LANES

IN_TM = 1024
IN_TN = PROJ_PACKED // 6
GDN_TL = 256
SOLVE_GROUP = 4
OUT_TL = 512
VMEM_LIMIT = 56 * 1024 * 1024


def _bdot(a, b):
    return jnp.dot(a.astype(jnp.bfloat16), b.astype(jnp.bfloat16),
                   preferred_element_type=jnp.float32)


def _bdot_nt(a, b):
    return lax.dot_general(a.astype(jnp.bfloat16), b.astype(jnp.bfloat16),
                           (((1,), (1,)), ((), ())), preferred_element_type=jnp.float32)


def _bdot_tn(a, b):
    return lax.dot_general(a.astype(jnp.bfloat16), b.astype(jnp.bfloat16),
                           (((0,), (0,)), ((), ())), preferred_element_type=jnp.float32)


def _silu(x):
    return x * jax.nn.sigmoid(x)


def _shift_rows(x, halo, s):
    r = pltpu.roll(x, s, axis=0)
    hr = pltpu.roll(halo, s, axis=0)
    rows = lax.broadcasted_iota(jnp.int32, hr.shape, 0)
    top = jnp.where(rows < s, hr, r[0:HALO])
    return jnp.concatenate([top, r[HALO:]], axis=0)


def _causal_conv(x, halo, w_ref, cols, width):
    acc = x * w_ref[width - 1:width, cols]
    for s in range(1, width):
        acc = acc + _shift_rows(x, halo, s) * w_ref[width - 1 - s:width - s, cols]
    return acc


def _in_proj_kernel(x_ref, nw_ref, w_ref, o_ref, h_s):
    @pl.when(pl.program_id(1) == 0)
    def _():
        x = x_ref[...]
        ms = jnp.mean(x * x, axis=-1, keepdims=True)
        h_s[...] = (x * lax.rsqrt(ms + EPS) * nw_ref[...]).astype(jnp.bfloat16)

    o_ref[...] = jnp.dot(h_s[...], w_ref[...], preferred_element_type=jnp.float32)


def _in_proj(x2d, norm_w, w_packed):
    tokens = x2d.shape[0]
    return pl.pallas_call(
        _in_proj_kernel,
        grid=(tokens // IN_TM, PROJ_PACKED // IN_TN),
        in_specs=[
            pl.BlockSpec((IN_TM, D_MODEL), lambda i, j: (i, 0)),
            pl.BlockSpec((1, D_MODEL), lambda i, j: (0, 0)),
            pl.BlockSpec((D_MODEL, IN_TN), lambda i, j: (0, j)),
        ],
        out_specs=pl.BlockSpec((IN_TM, IN_TN), lambda i, j: (i, j)),
        out_shape=jax.ShapeDtypeStruct((tokens, PROJ_PACKED), jnp.float32),
        scratch_shapes=[pltpu.VMEM((IN_TM, D_MODEL), jnp.bfloat16)],
        compiler_params=pltpu.CompilerParams(
            dimension_semantics=("arbitrary", "arbitrary"), vmem_limit_bytes=VMEM_LIMIT),
        name="in_proj",
    )(x2d, norm_w, w_packed)


def _gdn_kernel(qkv_ref, halo_ref, z_ref, bl_ref, al_ref, cw_ref, alog_ref, dtb_ref, gnw_ref,
                o_ref, s_ref, q_s, k_s, v_s, beta_s, gc_s, gct_s, u_s, w_s, attn_s):
    t = pl.program_id(1)
    n_chunks = GDN_TL // CHUNK

    @pl.when(t == 0)
    def _():
        s_ref[...] = jnp.zeros_like(s_ref)

    halo = jnp.where(t > 0, halo_ref[...], 0.0)

    for grp, dst in enumerate((q_s, k_s, v_s)):
        cols = slice(grp * GDN_WIDTH, (grp + 1) * GDN_WIDTH)
        act = _silu(_causal_conv(qkv_ref[:, cols], halo[:, cols], cw_ref, cols, GDN_CONV))
        if grp == 2:
            dst[...] = act
        else:
            scale = HEAD_DIM ** -0.5 if grp == 0 else 1.0
            for h in range(HEADS):
                blk = act[:, h * HEAD_DIM:(h + 1) * HEAD_DIM]
                inv = lax.rsqrt(jnp.sum(blk * blk, axis=-1, keepdims=True) + EPS)
                dst[:, h * HEAD_DIM:(h + 1) * HEAD_DIM] = blk * inv * scale

    beta_s[...] = jax.nn.sigmoid(bl_ref[...])
    a_in = al_ref[...] + dtb_ref[...]
    softplus = jnp.maximum(a_in, 0.0) + jnp.log1p(jnp.exp(-jnp.abs(a_in)))
    gc = -jnp.exp(alog_ref[...]) * softplus
    pos = lax.broadcasted_iota(jnp.int32, gc.shape, 0) % CHUNK
    s = 1
    while s < CHUNK:
        gc = gc + jnp.where(pos >= s, pltpu.roll(gc, s, axis=0), 0.0)
        s *= 2
    gc_s[...] = gc
    gct = gc.T
    for c in range(n_chunks):
        blk = gct[:, c * CHUNK:(c + 1) * CHUNK]
        gct_s[c] = jnp.concatenate([blk, blk], axis=1)

    ri = lax.broadcasted_iota(jnp.int32, (CHUNK, 2 * CHUNK), 0)
    ci = lax.broadcasted_iota(jnp.int32, (CHUNK, 2 * CHUNK), 1)
    cj = ci % CHUNK
    causal2 = ri >= cj
    strict2 = ri > cj
    left = ci < CHUNK
    eye_right = jnp.where((ri == cj) & (ci >= CHUNK), 1.0, 0.0)

    heads = range(HEADS)
    lanes = [slice(h * HEAD_DIM, (h + 1) * HEAD_DIM) for h in heads]
    col = lambda blk, h: blk[:, h:h + 1]

    def solve_body(i, carry):
        items = []
        for cc in range(SOLVE_GROUP):
            c = i * SOLVE_GROUP + cc
            rows = pl.ds(pl.multiple_of(c * CHUNK, CHUNK), CHUNK)
            beta_blk = beta_s[rows, :]
            gc_blk = gc_s[rows, :]
            eg_blk = jnp.exp(gc_blk)
            gct_blk = gct_s[c]
            for h in heads:
                items.append((rows, h, col(beta_blk, h), col(gc_blk, h), col(eg_blk, h),
                              gct_blk[h:h + 1, :]))
        k = [k_s[rows, lanes[h]] for rows, h, *_ in items]
        kb = [k[n] * it[2] for n, it in enumerate(items)]
        decay2 = [jnp.where(causal2, jnp.exp(it[3] - it[5]), 0.0) for it in items]
        kq = [_bdot_nt(jnp.concatenate([kb[n], q_s[it[0], lanes[it[1]]]], axis=0),
                       jnp.concatenate([k[n], k[n]], axis=0)) for n, it in enumerate(items)]
        for n, it in enumerate(items):
            attn_s[it[1], it[0], :] = kq[n][CHUNK:, :CHUNK] * decay2[n][:, :CHUNK]
        a2 = [jnp.where(strict2, kq[n][:CHUNK] * decay2[n], 0.0) for n in range(len(items))]
        res = [_bdot(a[:, :CHUNK], a) for a in a2]
        pt = [jnp.where(left, res[n], eye_right - a2[n]) for n in range(len(items))]
        m = 4
        while m < CHUNK:
            res = [_bdot(x[:, :CHUNK], x) for x in pt]
            pt = [jnp.where(left, res[n], pt[n] + res[n]) for n in range(len(items))]
            m *= 2
        res = [_bdot(x[:, :CHUNK], x) for x in pt]
        t_mat = [(pt[n] + res[n])[:, CHUNK:] for n in range(len(items))]
        rhs = [jnp.concatenate([v_s[it[0], lanes[it[1]]] * it[2], kb[n] * it[4]], axis=1)
               for n, it in enumerate(items)]
        sol = [_bdot(t_mat[n], rhs[n]) for n in range(len(items))]
        for n, it in enumerate(items):
            u_s[it[0], lanes[it[1]]] = sol[n][:, :HEAD_DIM]
            w_s[it[0], lanes[it[1]]] = sol[n][:, HEAD_DIM:]
        return carry

    lax.fori_loop(0, n_chunks // SOLVE_GROUP, solve_body, 0)

    def chunk_body(c, carry):
        rows = pl.ds(pl.multiple_of(c * CHUNK, CHUNK), CHUNK)
        gc_blk = gc_s[rows, :]
        g_last = gc_blk[CHUNK - 1:CHUNK, :]
        eg_blk = jnp.exp(gc_blk)
        ek_blk = jnp.exp(g_last - gc_blk)
        egl_blk = jnp.exp(g_last)
        state = [s_ref[h] for h in heads]
        ws_qs = [_bdot(jnp.concatenate([w_s[rows, lanes[h]],
                                        q_s[rows, lanes[h]] * col(eg_blk, h)], axis=0), state[h])
                 for h in heads]
        v_new = [u_s[rows, lanes[h]] - ws_qs[h][:CHUNK] for h in heads]
        o = [ws_qs[h][CHUNK:] + _bdot(attn_s[h, rows, :], v_new[h]) for h in heads]
        kv = [_bdot_tn(k_s[rows, lanes[h]] * col(ek_blk, h), v_new[h]) for h in heads]
        for h in heads:
            s_ref[h] = state[h] * col(egl_blk, h) + kv[h]
            ms = jnp.mean(o[h] * o[h], axis=-1, keepdims=True)
            o_ref[rows, lanes[h]] = (o[h] * lax.rsqrt(ms + EPS) * gnw_ref[...]
                                     * _silu(z_ref[rows, lanes[h]]))
        return carry

    lax.fori_loop(0, n_chunks, chunk_body, 0, unroll=True)


def _gdn(proj, conv_qkv_w, a_log_pad, dt_bias_pad, gdn_norm_w, batch, seq):
    tokens = batch * seq
    tiles = seq // GDN_TL
    n_chunks = GDN_TL // CHUNK

    def row_map(col_block):
        return lambda b, t: (b * tiles + t, col_block)

    def halo_map(b, t):
        return (jnp.maximum((b * seq + t * GDN_TL) // HALO - 1, 0), 0)

    const = lambda b, t: (0, 0)
    return pl.pallas_call(
        _gdn_kernel,
        grid=(batch, tiles),
        in_specs=[
            pl.BlockSpec((GDN_TL, 3 * GDN_WIDTH), row_map(0)),
            pl.BlockSpec((HALO, 3 * GDN_WIDTH), halo_map),
            pl.BlockSpec((GDN_TL, GDN_WIDTH), row_map(COL_Z // GDN_WIDTH)),
            pl.BlockSpec((GDN_TL, LANES), row_map(COL_BETA // LANES)),
            pl.BlockSpec((GDN_TL, LANES), row_map(COL_A // LANES)),
            pl.BlockSpec((GDN_CONV, 3 * GDN_WIDTH), const),
            pl.BlockSpec((1, LANES), const),
            pl.BlockSpec((1, LANES), const),
            pl.BlockSpec((1, HEAD_DIM), const),
        ],
        out_specs=pl.BlockSpec((GDN_TL, GDN_WIDTH), lambda b, t: (b * tiles + t, 0)),
        out_shape=jax.ShapeDtypeStruct((tokens, GDN_WIDTH), jnp.float32),
        scratch_shapes=[
            pltpu.VMEM((HEADS, HEAD_DIM, HEAD_DIM), jnp.float32),
            pltpu.VMEM((GDN_TL, GDN_WIDTH), jnp.float32),
            pltpu.VMEM((GDN_TL, GDN_WIDTH), jnp.float32),
            pltpu.VMEM((GDN_TL, GDN_WIDTH), jnp.float32),
            pltpu.VMEM((GDN_TL, LANES), jnp.float32),
            pltpu.VMEM((GDN_TL, LANES), jnp.float32),
            pltpu.VMEM((n_chunks, LANES, 2 * CHUNK), jnp.float32),
            pltpu.VMEM((GDN_TL, GDN_WIDTH), jnp.float32),
            pltpu.VMEM((GDN_TL, GDN_WIDTH), jnp.float32),
            pltpu.VMEM((HEADS, GDN_TL, CHUNK), jnp.float32),
        ],
        compiler_params=pltpu.CompilerParams(
            dimension_semantics=("arbitrary", "arbitrary"), vmem_limit_bytes=VMEM_LIMIT),
        name="gdn",
    )(proj, proj, proj, proj, proj, conv_qkv_w, a_log_pad, dt_bias_pad, gdn_norm_w)


def _out_proj_kernel(x_ref, o_ref, gb_ref, gc_ref, hc_ref, zc_ref, gc_halo_ref, hc_halo_ref,
                     cw_ref, cb_ref, wo_ref, fnw_ref, out_ref):
    t = pl.program_id(1)
    ch = gc_ref[...] * hc_ref[...]
    halo = jnp.where(t > 0, gc_halo_ref[...] * hc_halo_ref[...], 0.0)
    conv = _causal_conv(ch, halo, cw_ref, slice(None), SHORT_CONV)
    y = gb_ref[...] * (conv + cb_ref[...]) * _silu(zc_ref[...])
    res = (x_ref[...] + _bdot(o_ref[...], wo_ref[:GDN_WIDTH, :])
           + _bdot(y, wo_ref[GDN_WIDTH:, :]))
    ms = jnp.mean(res * res, axis=-1, keepdims=True)
    out_ref[...] = res * lax.rsqrt(ms + EPS) * fnw_ref[...]


def _out_proj(x2d, o_gdn, proj, conv_w, conv_b, w_out, final_norm_w, batch, seq):
    tokens = batch * seq
    tiles = seq // OUT_TL

    def row_map(col_block):
        return lambda b, t: (b * tiles + t, col_block)

    def halo_map(col_block):
        return lambda b, t: (jnp.maximum((b * seq + t * OUT_TL) // HALO - 1, 0), col_block)

    const = lambda b, t: (0, 0)
    wide = CONV_WIDTH
    return pl.pallas_call(
        _out_proj_kernel,
        grid=(batch, tiles),
        in_specs=[
            pl.BlockSpec((OUT_TL, D_MODEL), row_map(0)),
            pl.BlockSpec((OUT_TL, GDN_WIDTH), row_map(0)),
            pl.BlockSpec((OUT_TL, wide), row_map(COL_B // wide)),
            pl.BlockSpec((OUT_TL, wide), row_map(COL_C // wide)),
            pl.BlockSpec((OUT_TL, wide), row_map(COL_H // wide)),
            pl.BlockSpec((OUT_TL, wide), row_map(COL_ZC // wide)),
            pl.BlockSpec((HALO, wide), halo_map(COL_C // wide)),
            pl.BlockSpec((HALO, wide), halo_map(COL_H // wide)),
            pl.BlockSpec((SHORT_CONV, wide), const),
            pl.BlockSpec((1, wide), const),
            pl.BlockSpec((GDN_WIDTH + CONV_WIDTH, D_MODEL), const),
            pl.BlockSpec((1, D_MODEL), const),
        ],
        out_specs=pl.BlockSpec((OUT_TL, D_MODEL), row_map(0)),
        out_shape=jax.ShapeDtypeStruct((tokens, D_MODEL), jnp.float32),
        compiler_params=pltpu.CompilerParams(
            dimension_semantics=("arbitrary", "arbitrary"), vmem_limit_bytes=VMEM_LIMIT),
        name="out_proj",
    )(x2d, o_gdn, proj, proj, proj, proj, proj, proj, conv_w, conv_b, w_out, final_norm_w)


def _pad_lanes(v):
    return jnp.pad(v.reshape(1, -1), ((0, 0), (0, LANES - v.shape[-1])))


def _layer(x2d, batch, seq, norm_in_w, w_in, conv_qkv_w, a_log, dt_bias, gdn_norm_w,
           conv_w, conv_b, w_out, out_norm_w):
    n_main = COL_BETA
    w_main = jnp.concatenate([w_in[:, :COL_B], w_in[:, COL_B + 2 * HEADS:]], axis=1)
    w_beta = jnp.pad(w_in[:, COL_B:COL_B + HEADS], ((0, 0), (0, LANES - HEADS)))
    w_a = jnp.pad(w_in[:, COL_B + HEADS:COL_B + 2 * HEADS], ((0, 0), (0, LANES - HEADS)))
    w_packed = jnp.concatenate([w_main, w_beta, w_a], axis=1).astype(jnp.bfloat16)
    assert w_main.shape[1] == n_main and w_packed.shape[1] == PROJ_PACKED

    proj = _in_proj(x2d, norm_in_w.reshape(1, -1), w_packed)
    o_gdn = _gdn(proj, conv_qkv_w, _pad_lanes(a_log), _pad_lanes(dt_bias),
                 gdn_norm_w.reshape(1, -1), batch, seq)
    return _out_proj(x2d, o_gdn, proj, conv_w, conv_b.reshape(1, -1),
                     w_out.astype(jnp.bfloat16), out_norm_w.reshape(1, -1), batch, seq)


def kernel(x, norm_in_w, w_in, conv_qkv_w, A_log, dt_bias, gdn_norm_w, conv_w, conv_b, w_out,
           final_norm_w):
    batch, seq, d_model = x.shape
    depth = norm_in_w.shape[0]
    assert depth == 1 and d_model == D_MODEL
    assert seq % GDN_TL == 0 and seq % OUT_TL == 0 and (batch * seq) % IN_TM == 0
    x2d = x.reshape(batch * seq, d_model)
    out = _layer(x2d, batch, seq, norm_in_w[0], w_in[0], conv_qkv_w[0], A_log[0], dt_bias[0],
                 gdn_norm_w[0], conv_w[0], conv_b[0], w_out[0], final_norm_w)
    return out.reshape(batch, seq, d_model)
```

```python
import jax
import jax.numpy as jnp
from jax import lax
from jax.experimental import pallas as pl
from jax.experimental.pallas import tpu as pltpu

D_MODEL = 1024
HEADS = 8
HEAD_DIM = 128
GDN_WIDTH = HEADS * HEAD_DIM
CONV_WIDTH = 1024
GDN_CONV = 4
SHORT_CONV = 3
CHUNK = 64
EPS = 1e-6
LANES = 128
HALO = 8

COL_Z = 3 * GDN_WIDTH
COL_BETA = COL_Z + GDN_WIDTH
COL_A = COL_BETA + LANES
GDN_PROJ = COL_A + LANES

GDN_TL = 256
OUT_TL = 512
VMEM_LIMIT = 56 * 1024 * 1024


def _bdot(a, b):
    return jnp.dot(a.astype(jnp.bfloat16), b.astype(jnp.bfloat16),
                   preferred_element_type=jnp.float32)


def _bdot_nt(a, b):
    return lax.dot_general(a.astype(jnp.bfloat16), b.astype(jnp.bfloat16),
                           (((1,), (1,)), ((), ())), preferred_element_type=jnp.float32)


def _bdot_tn(a, b):
    return lax.dot_general(a.astype(jnp.bfloat16), b.astype(jnp.bfloat16),
                           (((0,), (0,)), ((), ())), preferred_element_type=jnp.float32)


def _silu(x):
    return x * jax.nn.sigmoid(x)


def _rmsnorm(x, w):
    ms = jnp.mean(x * x, axis=-1, keepdims=True)
    return x * lax.rsqrt(ms + EPS) * w


def _shift_rows(x, halo, s):
    r = pltpu.roll(x, s, axis=0)
    hr = pltpu.roll(halo, s, axis=0)
    rows = lax.broadcasted_iota(jnp.int32, hr.shape, 0)
    top = jnp.where(rows < s, hr, r[0:HALO])
    return jnp.concatenate([top, r[HALO:]], axis=0)


def _causal_conv(x, halo, w_ref, cols, width):
    acc = x * w_ref[width - 1:width, cols]
    for s in range(1, width):
        acc = acc + _shift_rows(x, halo, s) * w_ref[width - 1 - s:width - s, cols]
    return acc


def _gdn_kernel(x_ref, nw_ref, wp_ref, cw_ref, alog_ref, dtb_ref, gnw_ref,
                o_ref, s_ref, halo_s, q_s, k_s, v_s, z_s, beta_s, gc_s, gct_s, u_s, w_s, attn_s):
    n_chunks = GDN_TL // CHUNK

    @pl.when(pl.program_id(1) == 0)
    def _():
        s_ref[...] = jnp.zeros_like(s_ref)
        halo_s[...] = jnp.zeros_like(halo_s)

    hn = _rmsnorm(x_ref[...], nw_ref[...]).astype(jnp.bfloat16)

    for grp, dst in enumerate((q_s, k_s, v_s)):
        cols = slice(grp * GDN_WIDTH, (grp + 1) * GDN_WIDTH)
        pre = jnp.dot(hn, wp_ref[:, cols], preferred_element_type=jnp.float32)
        act = _silu(_causal_conv(pre, halo_s[:, cols], cw_ref, cols, GDN_CONV))
        halo_s[:, cols] = pre[GDN_TL - HALO:, :]
        if grp == 2:
            dst[...] = act
        else:
            scale = HEAD_DIM ** -0.5 if grp == 0 else 1.0
            for h in range(HEADS):
                blk = act[:, h * HEAD_DIM:(h + 1) * HEAD_DIM]
                inv = lax.rsqrt(jnp.sum(blk * blk, axis=-1, keepdims=True) + EPS)
                dst[:, h * HEAD_DIM:(h + 1) * HEAD_DIM] = blk * inv * scale
    z_s[...] = jnp.dot(hn, wp_ref[:, COL_Z:COL_BETA], preferred_element_type=jnp.float32)

    ba = jnp.dot(hn, wp_ref[:, COL_BETA:], preferred_element_type=jnp.float32)
    beta_s[...] = jax.nn.sigmoid(ba[:, :LANES])
    a_in = ba[:, LANES:] + dtb_ref[...]
    softplus = jnp.maximum(a_in, 0.0) + jnp.log1p(jnp.exp(-jnp.abs(a_in)))
    gc = -jnp.exp(alog_ref[...]) * softplus
    pos = lax.broadcasted_iota(jnp.int32, gc.shape, 0) % CHUNK
    s = 1
    while s < CHUNK:
        gc = gc + jnp.where(pos >= s, pltpu.roll(gc, s, axis=0), 0.0)
        s *= 2
    gc_s[...] = gc
    gct = gc.T
    for c in range(n_chunks):
        blk = gct[:, c * CHUNK:(c + 1) * CHUNK]
        gct_s[c] = jnp.concatenate([blk, blk], axis=1)

    ri = lax.broadcasted_iota(jnp.int32, (CHUNK, 2 * CHUNK), 0)
    ci = lax.broadcasted_iota(jnp.int32, (CHUNK, 2 * CHUNK), 1)
    cj = ci % CHUNK
    causal2 = ri >= cj
    strict2 = ri > cj
    left = ci < CHUNK
    eye_right = jnp.where((ri == cj) & (ci >= CHUNK), 1.0, 0.0)

    heads = range(HEADS)
    lanes = [slice(h * HEAD_DIM, (h + 1) * HEAD_DIM) for h in heads]
    col = lambda blk, h: blk[:, h:h + 1]

    items = []
    for c in range(n_chunks):
        rows = slice(c * CHUNK, (c + 1) * CHUNK)
        beta_blk = beta_s[rows, :]
        gc_blk = gc_s[rows, :]
        eg_blk = jnp.exp(gc_blk)
        gct_blk = gct_s[c]
        for h in heads:
            items.append((rows, h, col(beta_blk, h), col(gc_blk, h), col(eg_blk, h),
                          gct_blk[h:h + 1, :]))
    n_items = range(len(items))
    k = [k_s[rows, lanes[h]] for rows, h, *_ in items]
    kb = [k[n] * items[n][2] for n in n_items]
    decay2 = [jnp.where(causal2, jnp.exp(it[3] - it[5]), 0.0) for it in items]
    kq = [_bdot_nt(jnp.concatenate([kb[n], q_s[it[0], lanes[it[1]]]], axis=0),
                   jnp.concatenate([k[n], k[n]], axis=0)) for n, it in enumerate(items)]
    for n, it in enumerate(items):
        attn_s[it[1], it[0], :] = kq[n][CHUNK:, :CHUNK] * decay2[n][:, :CHUNK]
    a2 = [jnp.where(strict2, kq[n][:CHUNK] * decay2[n], 0.0) for n in n_items]
    res = [_bdot(a[:, :CHUNK], a) for a in a2]
    pt = [jnp.where(left, res[n], eye_right - a2[n]) for n in n_items]
    m = 4
    while m < CHUNK:
        res = [_bdot(x[:, :CHUNK], x) for x in pt]
        pt = [jnp.where(left, res[n], pt[n] + res[n]) for n in n_items]
        m *= 2
    res = [_bdot(x[:, :CHUNK], x) for x in pt]
    t_mat = [(pt[n] + res[n])[:, CHUNK:] for n in n_items]
    rhs = [jnp.concatenate([v_s[it[0], lanes[it[1]]] * it[2], kb[n] * it[4]], axis=1)
           for n, it in enumerate(items)]
    sol = [_bdot(t_mat[n], rhs[n]) for n in n_items]
    for n, it in enumerate(items):
        u_s[it[0], lanes[it[1]]] = sol[n][:, :HEAD_DIM]
        w_s[it[0], lanes[it[1]]] = sol[n][:, HEAD_DIM:]

    for c in range(n_chunks):
        rows = slice(c * CHUNK, (c + 1) * CHUNK)
        gc_blk = gc_s[rows, :]
        g_last = gc_blk[CHUNK - 1:CHUNK, :]
        eg_blk = jnp.exp(gc_blk)
        ek_blk = jnp.exp(g_last - gc_blk)
        egl_blk = jnp.exp(g_last)
        state = [s_ref[h] for h in heads]
        ws_qs = [_bdot(jnp.concatenate([w_s[rows, lanes[h]],
                                        q_s[rows, lanes[h]] * col(eg_blk, h)], axis=0), state[h])
                 for h in heads]
        v_new = [u_s[rows, lanes[h]] - ws_qs[h][:CHUNK] for h in heads]
        o = [ws_qs[h][CHUNK:] + _bdot(attn_s[h, rows, :], v_new[h]) for h in heads]
        kv = [_bdot_tn(k_s[rows, lanes[h]] * col(ek_blk, h), v_new[h]) for h in heads]
        for h in heads:
            s_ref[h] = state[h] * col(egl_blk, h) + kv[h]
            o_ref[rows, lanes[h]] = _rmsnorm(o[h], gnw_ref[...]) * _silu(z_s[rows, lanes[h]])


def _gdn(x2d, norm_w, w_gdn, conv_qkv_w, a_log_pad, dt_bias_pad, gdn_norm_w, batch, seq):
    tokens = batch * seq
    tiles = seq // GDN_TL
    n_chunks = GDN_TL // CHUNK
    row = lambda b, t: (b * tiles + t, 0)
    const = lambda b, t: (0, 0)
    tile_f32 = pltpu.VMEM((GDN_TL, GDN_WIDTH), jnp.float32)
    return pl.pallas_call(
        _gdn_kernel,
        grid=(batch, tiles),
        in_specs=[
            pl.BlockSpec((GDN_TL, D_MODEL), row),
            pl.BlockSpec((1, D_MODEL), const),
            pl.BlockSpec((D_MODEL, GDN_PROJ), const),
            pl.BlockSpec((GDN_CONV, 3 * GDN_WIDTH), const),
            pl.BlockSpec((1, LANES), const),
            pl.BlockSpec((1, LANES), const),
            pl.BlockSpec((1, HEAD_DIM), const),
        ],
        out_specs=pl.BlockSpec((GDN_TL, GDN_WIDTH), row),
        out_shape=jax.ShapeDtypeStruct((tokens, GDN_WIDTH), jnp.float32),
        scratch_shapes=[
            pltpu.VMEM((HEADS, HEAD_DIM, HEAD_DIM), jnp.float32),
            pltpu.VMEM((HALO, 3 * GDN_WIDTH), jnp.float32),
            tile_f32, tile_f32, tile_f32,
            tile_f32,
            pltpu.VMEM((GDN_TL, LANES), jnp.float32),
            pltpu.VMEM((GDN_TL, LANES), jnp.float32),
            pltpu.VMEM((n_chunks, LANES, 2 * CHUNK), jnp.float32),
            tile_f32,
            tile_f32,
            pltpu.VMEM((HEADS, GDN_TL, CHUNK), jnp.float32),
        ],
        compiler_params=pltpu.CompilerParams(
            dimension_semantics=("arbitrary", "arbitrary"), vmem_limit_bytes=VMEM_LIMIT),
        name="gdn",
    )(x2d, norm_w, w_gdn, conv_qkv_w, a_log_pad, dt_bias_pad, gdn_norm_w)


def _out_proj_kernel(x_ref, o_ref, nw_ref, wp_ref, cw_ref, cb_ref, wo_ref, fnw_ref, out_ref,
                     halo_s):
    @pl.when(pl.program_id(1) == 0)
    def _():
        halo_s[...] = jnp.zeros_like(halo_s)

    x = x_ref[...]
    hn = _rmsnorm(x, nw_ref[...]).astype(jnp.bfloat16)
    proj = lambda j: jnp.dot(hn, wp_ref[:, j * CONV_WIDTH:(j + 1) * CONV_WIDTH],
                             preferred_element_type=jnp.float32)
    ch = proj(1) * proj(2)
    conv = _causal_conv(ch, halo_s[...], cw_ref, slice(None), SHORT_CONV)
    halo_s[...] = ch[OUT_TL - HALO:, :]
    y = proj(0) * (conv + cb_ref[...]) * _silu(proj(3))
    res = x + _bdot(o_ref[...], wo_ref[:GDN_WIDTH, :]) + _bdot(y, wo_ref[GDN_WIDTH:, :])
    out_ref[...] = _rmsnorm(res, fnw_ref[...])


def _out_proj(x2d, o_gdn, norm_w, w_conv, conv_w, conv_b, w_out, final_norm_w, batch, seq):
    tokens = batch * seq
    tiles = seq // OUT_TL
    row = lambda b, t: (b * tiles + t, 0)
    const = lambda b, t: (0, 0)
    return pl.pallas_call(
        _out_proj_kernel,
        grid=(batch, tiles),
        in_specs=[
            pl.BlockSpec((OUT_TL, D_MODEL), row),
            pl.BlockSpec((OUT_TL, GDN_WIDTH), row),
            pl.BlockSpec((1, D_MODEL), const),
            pl.BlockSpec((D_MODEL, 4 * CONV_WIDTH), const),
            pl.BlockSpec((SHORT_CONV, CONV_WIDTH), const),
            pl.BlockSpec((1, CONV_WIDTH), const),
            pl.BlockSpec((GDN_WIDTH + CONV_WIDTH, D_MODEL), const),
            pl.BlockSpec((1, D_MODEL), const),
        ],
        out_specs=pl.BlockSpec((OUT_TL, D_MODEL), row),
        out_shape=jax.ShapeDtypeStruct((tokens, D_MODEL), jnp.float32),
        scratch_shapes=[pltpu.VMEM((HALO, CONV_WIDTH), jnp.float32)],
        compiler_params=pltpu.CompilerParams(
            dimension_semantics=("arbitrary", "arbitrary"), vmem_limit_bytes=VMEM_LIMIT),
        name="out_proj",
    )(x2d, o_gdn, norm_w, w_conv, conv_w, conv_b, w_out, final_norm_w)


def _pad_lanes(v):
    return jnp.pad(v, ((0, 0), (0, LANES - v.shape[-1])))


def _layer(x2d, batch, seq, norm_in_w, w_in, conv_qkv_w, a_log, dt_bias, gdn_norm_w,
           conv_w, conv_b, w_out, out_norm_w):
    n_gdn = COL_BETA
    w_gdn = jnp.concatenate(
        [w_in[:, :n_gdn], _pad_lanes(w_in[:, n_gdn:n_gdn + HEADS]),
         _pad_lanes(w_in[:, n_gdn + HEADS:n_gdn + 2 * HEADS])], axis=1).astype(jnp.bfloat16)
    w_conv = w_in[:, n_gdn + 2 * HEADS:].astype(jnp.bfloat16)
    norm_w = norm_in_w.reshape(1, -1)
    o_gdn = _gdn(x2d, norm_w, w_gdn, conv_qkv_w, _pad_lanes(a_log.reshape(1, -1)),
                 _pad_lanes(dt_bias.reshape(1, -1)), gdn_norm_w.reshape(1, -1), batch, seq)
    return _out_proj(x2d, o_gdn, norm_w, w_conv, conv_w, conv_b.reshape(1, -1),
                     w_out.astype(jnp.bfloat16), out_norm_w.reshape(1, -1), batch, seq)


def kernel(x, norm_in_w, w_in, conv_qkv_w, A_log, dt_bias, gdn_norm_w, conv_w, conv_b, w_out,
           final_norm_w):
    batch, seq, d_model = x.shape
    assert norm_in_w.shape[0] == 1 and d_model == D_MODEL
    assert seq % GDN_TL == 0 and seq % OUT_TL == 0
    x2d = x.reshape(batch * seq, d_model)
    out = _layer(x2d, batch, seq, norm_in_w[0], w_in[0], conv_qkv_w[0], A_log[0], dt_bias[0],
                 gdn_norm_w[0], conv_w[0], conv_b[0], w_out[0], final_norm_w)
    return out.reshape(batch, seq, d_model)
```

```python
import jax
import jax.numpy as jnp
from jax import lax
from jax.experimental import pallas as pl
from jax.experimental.pallas import tpu as pltpu

D_MODEL = 1024
HEADS = 8
HEAD_DIM = 128
GDN_WIDTH = HEADS * HEAD_DIM
CONV_WIDTH = 1024
GDN_CONV = 4
SHORT_CONV = 3
CHUNK = 64
EPS = 1e-6
LANES = 128
HALO = 8

COL_Z = 3 * GDN_WIDTH
COL_BETA = COL_Z + GDN_WIDTH
COL_A = COL_BETA + LANES
COL_CONV = COL_A + LANES
PROJ_PACKED = COL_CONV + 4 * CONV_WIDTH

TL = 256
VMEM_LIMIT = 60 * 1024 * 1024


def _bdot(a, b):
    return jnp.dot(a.astype(jnp.bfloat16), b.astype(jnp.bfloat16),
                   preferred_element_type=jnp.float32)


def _bdot_nt(a, b):
    return lax.dot_general(a.astype(jnp.bfloat16), b.astype(jnp.bfloat16),
                           (((1,), (1,)), ((), ())), preferred_element_type=jnp.float32)


def _bdot_tn(a, b):
    return lax.dot_general(a.astype(jnp.bfloat16), b.astype(jnp.bfloat16),
                           (((0,), (0,)), ((), ())), preferred_element_type=jnp.float32)


def _silu(x):
    return x * jax.nn.sigmoid(x)


def _rmsnorm(x, w):
    ms = jnp.mean(x * x, axis=-1, keepdims=True)
    return x * lax.rsqrt(ms + EPS) * w


def _shift_rows(x, halo, s):
    r = pltpu.roll(x, s, axis=0)
    hr = pltpu.roll(halo, s, axis=0)
    rows = lax.broadcasted_iota(jnp.int32, hr.shape, 0)
    top = jnp.where(rows < s, hr, r[0:HALO])
    return jnp.concatenate([top, r[HALO:]], axis=0)


def _causal_conv(x, halo, w_ref, cols, width):
    acc = x * w_ref[width - 1:width, cols]
    for s in range(1, width):
        acc = acc + _shift_rows(x, halo, s) * w_ref[width - 1 - s:width - s, cols]
    return acc


def _block_kernel(x_ref, nw_ref, wp_ref, cw_ref, alog_ref, dtb_ref, gnw_ref, ccw_ref, cb_ref,
                  wo_ref, fnw_ref, out_ref,
                  s_ref, halo_s, chalo_s, q_s, k_s, v_s, z_s, beta_s, gc_s, gct_s, u_s, w_s,
                  attn_s, o_s):
    n_chunks = TL // CHUNK
    heads = range(HEADS)
    lanes = [slice(h * HEAD_DIM, (h + 1) * HEAD_DIM) for h in heads]
    col = lambda blk, h: blk[:, h:h + 1]

    @pl.when(pl.program_id(1) == 0)
    def _():
        s_ref[...] = jnp.zeros_like(s_ref)
        halo_s[...] = jnp.zeros_like(halo_s)
        chalo_s[...] = jnp.zeros_like(chalo_s)

    x = x_ref[...]
    hn = _rmsnorm(x, nw_ref[...]).astype(jnp.bfloat16)
    proj = lambda lo, hi: jnp.dot(hn, wp_ref[:, lo:hi], preferred_element_type=jnp.float32)
    conv_col = lambda j: proj(COL_CONV + j * CONV_WIDTH, COL_CONV + (j + 1) * CONV_WIDTH)

    ba = proj(COL_BETA, COL_CONV)
    beta_s[...] = jax.nn.sigmoid(ba[:, :LANES])
    a_in = ba[:, LANES:] + dtb_ref[...]
    softplus = jnp.maximum(a_in, 0.0) + jnp.log1p(jnp.exp(-jnp.abs(a_in)))
    gc = -jnp.exp(alog_ref[...]) * softplus
    pos = lax.broadcasted_iota(jnp.int32, gc.shape, 0) % CHUNK
    s = 1
    while s < CHUNK:
        gc = gc + jnp.where(pos >= s, pltpu.roll(gc, s, axis=0), 0.0)
        s *= 2
    gc_s[...] = gc
    gct = gc.T
    for c in range(n_chunks):
        blk = gct[:, c * CHUNK:(c + 1) * CHUNK]
        gct_s[c] = jnp.concatenate([blk, blk], axis=1)

    conv_in = []
    for grp, dst in enumerate((q_s, k_s, v_s)):
        cols = slice(grp * GDN_WIDTH, (grp + 1) * GDN_WIDTH)
        pre = proj(cols.start, cols.stop)
        conv_in.append(conv_col(grp))
        act = _silu(_causal_conv(pre, halo_s[:, cols], cw_ref, cols, GDN_CONV))
        halo_s[:, cols] = pre[TL - HALO:, :]
        if grp == 2:
            dst[...] = act
        else:
            scale = HEAD_DIM ** -0.5 if grp == 0 else 1.0
            for h in heads:
                blk = act[:, lanes[h]]
                inv = lax.rsqrt(jnp.sum(blk * blk, axis=-1, keepdims=True) + EPS)
                dst[:, lanes[h]] = blk * inv * scale
    z_s[...] = proj(COL_Z, COL_BETA)
    gate_c = conv_col(3)

    gate_b, gate_cc, h_c = conv_in
    ch = gate_cc * h_c
    y = (gate_b * (_causal_conv(ch, chalo_s[...], ccw_ref, slice(None), SHORT_CONV) + cb_ref[...])
         * _silu(gate_c))
    chalo_s[...] = ch[TL - HALO:, :]
    res = x + _bdot(y, wo_ref[GDN_WIDTH:, :])

    ri = lax.broadcasted_iota(jnp.int32, (CHUNK, 2 * CHUNK), 0)
    ci = lax.broadcasted_iota(jnp.int32, (CHUNK, 2 * CHUNK), 1)
    cj = ci % CHUNK
    causal2 = ri >= cj
    strict2 = ri > cj
    left = ci < CHUNK
    eye_right = jnp.where((ri == cj) & (ci >= CHUNK), 1.0, 0.0)

    items = []
    for c in range(n_chunks):
        rows = slice(c * CHUNK, (c + 1) * CHUNK)
        beta_blk = beta_s[rows, :]
        gc_blk = gc_s[rows, :]
        eg_blk = jnp.exp(gc_blk)
        gct_blk = gct_s[c]
        for h in heads:
            items.append((rows, h, col(beta_blk, h), col(gc_blk, h), col(eg_blk, h),
                          gct_blk[h:h + 1, :]))
    n_items = range(len(items))
    k = [k_s[rows, lanes[h]] for rows, h, *_ in items]
    kb = [k[n] * items[n][2] for n in n_items]
    decay2 = [jnp.where(causal2, jnp.exp(it[3] - it[5]), 0.0) for it in items]
    kq = [_bdot_nt(jnp.concatenate([kb[n], q_s[it[0], lanes[it[1]]]], axis=0),
                   jnp.concatenate([k[n], k[n]], axis=0)) for n, it in enumerate(items)]
    for n, it in enumerate(items):
        attn_s[it[1], it[0], :] = kq[n][CHUNK:, :CHUNK] * decay2[n][:, :CHUNK]
    a2 = [jnp.where(strict2, kq[n][:CHUNK] * decay2[n], 0.0) for n in n_items]
    prod = [_bdot(a[:, :CHUNK], a) for a in a2]
    pt = [jnp.where(left, prod[n], eye_right - a2[n]) for n in n_items]
    m = 4
    while m < CHUNK:
        prod = [_bdot(p[:, :CHUNK], p) for p in pt]
        pt = [jnp.where(left, prod[n], pt[n] + prod[n]) for n in n_items]
        m *= 2
    prod = [_bdot(p[:, :CHUNK], p) for p in pt]
    t_mat = [(pt[n] + prod[n])[:, CHUNK:] for n in n_items]
    rhs = [jnp.concatenate([v_s[it[0], lanes[it[1]]] * it[2], kb[n] * it[4]], axis=1)
           for n, it in enumerate(items)]
    sol = [_bdot(t_mat[n], rhs[n]) for n in n_items]
    for n, it in enumerate(items):
        u_s[it[0], lanes[it[1]]] = sol[n][:, :HEAD_DIM]
        w_s[it[0], lanes[it[1]]] = sol[n][:, HEAD_DIM:]

    state = [s_ref[h] for h in heads]
    for c in range(n_chunks):
        rows = slice(c * CHUNK, (c + 1) * CHUNK)
        gc_blk = gc_s[rows, :]
        g_last = gc_blk[CHUNK - 1:CHUNK, :]
        eg_blk = jnp.exp(gc_blk)
        ek_blk = jnp.exp(g_last - gc_blk)
        egl_blk = jnp.exp(g_last)
        ws_qs = [_bdot(jnp.concatenate([w_s[rows, lanes[h]],
                                        q_s[rows, lanes[h]] * col(eg_blk, h)], axis=0), state[h])
                 for h in heads]
        v_new = [u_s[rows, lanes[h]] - ws_qs[h][:CHUNK] for h in heads]
        o = [ws_qs[h][CHUNK:] + _bdot(attn_s[h, rows, :], v_new[h]) for h in heads]
        kv = [_bdot_tn(k_s[rows, lanes[h]] * col(ek_blk, h), v_new[h]) for h in heads]
        state = [state[h] * col(egl_blk, h) + kv[h] for h in heads]
        for h in heads:
            o_s[rows, lanes[h]] = _rmsnorm(o[h], gnw_ref[...]) * _silu(z_s[rows, lanes[h]])
    for h in heads:
        s_ref[h] = state[h]

    res = res + _bdot(o_s[...], wo_ref[:GDN_WIDTH, :])
    out_ref[...] = _rmsnorm(res, fnw_ref[...])


def _block(x2d, norm_w, w_packed, conv_qkv_w, a_log_pad, dt_bias_pad, gdn_norm_w, conv_w, conv_b,
           w_out, out_norm_w, batch, seq):
    tokens = batch * seq
    tiles = seq // TL
    n_chunks = TL // CHUNK
    row = lambda b, t: (b * tiles + t, 0)
    const = lambda b, t: (0, 0)
    resident = lambda shape: pl.BlockSpec(shape, const, pipeline_mode=pl.Buffered(1))
    tile_f32 = pltpu.VMEM((TL, GDN_WIDTH), jnp.float32)
    return pl.pallas_call(
        _block_kernel,
        grid=(batch, tiles),
        in_specs=[
            pl.BlockSpec((TL, D_MODEL), row),
            resident((1, D_MODEL)),
            resident((D_MODEL, PROJ_PACKED)),
            resident((GDN_CONV, 3 * GDN_WIDTH)),
            resident((1, LANES)),
            resident((1, LANES)),
            resident((1, HEAD_DIM)),
            resident((SHORT_CONV, CONV_WIDTH)),
            resident((1, CONV_WIDTH)),
            resident((GDN_WIDTH + CONV_WIDTH, D_MODEL)),
            resident((1, D_MODEL)),
        ],
        out_specs=pl.BlockSpec((TL, D_MODEL), row),
        out_shape=jax.ShapeDtypeStruct((tokens, D_MODEL), jnp.float32),
        scratch_shapes=[
            pltpu.VMEM((HEADS, HEAD_DIM, HEAD_DIM), jnp.float32),
            pltpu.VMEM((HALO, 3 * GDN_WIDTH), jnp.float32),
            pltpu.VMEM((HALO, CONV_WIDTH), jnp.float32),
            tile_f32, tile_f32, tile_f32,
            tile_f32,
            pltpu.VMEM((TL, LANES), jnp.float32),
            pltpu.VMEM((TL, LANES), jnp.float32),
            pltpu.VMEM((n_chunks, LANES, 2 * CHUNK), jnp.float32),
            tile_f32, tile_f32,
            pltpu.VMEM((HEADS, TL, CHUNK), jnp.float32),
            tile_f32,
        ],
        compiler_params=pltpu.CompilerParams(
            dimension_semantics=("arbitrary", "arbitrary"), vmem_limit_bytes=VMEM_LIMIT),
        name="hybrid_block",
    )(x2d, norm_w, w_packed, conv_qkv_w, a_log_pad, dt_bias_pad, gdn_norm_w, conv_w, conv_b,
      w_out, out_norm_w)


def _pad_lanes(v):
    return jnp.pad(v, ((0, 0), (0, LANES - v.shape[-1])))


def _layer(x2d, batch, seq, norm_in_w, w_in, conv_qkv_w, a_log, dt_bias, gdn_norm_w,
           conv_w, conv_b, w_out, out_norm_w):
    n_gdn = COL_BETA
    w_packed = jnp.concatenate(
        [w_in[:, :n_gdn], _pad_lanes(w_in[:, n_gdn:n_gdn + HEADS]),
         _pad_lanes(w_in[:, n_gdn + HEADS:n_gdn + 2 * HEADS]), w_in[:, n_gdn + 2 * HEADS:]],
        axis=1).astype(jnp.bfloat16)
    return _block(x2d, norm_in_w.reshape(1, -1), w_packed, conv_qkv_w,
                  _pad_lanes(a_log.reshape(1, -1)), _pad_lanes(dt_bias.reshape(1, -1)),
                  gdn_norm_w.reshape(1, -1), conv_w, conv_b.reshape(1, -1),
                  w_out.astype(jnp.bfloat16), out_norm_w.reshape(1, -1), batch, seq)


def kernel(x, norm_in_w, w_in, conv_qkv_w, A_log, dt_bias, gdn_norm_w, conv_w, conv_b, w_out,
           final_norm_w):
    batch, seq, d_model = x.shape
    assert norm_in_w.shape[0] == 1 and d_model == D_MODEL and seq % TL == 0
    x2d = x.reshape(batch * seq, d_model)
    out = _layer(x2d, batch, seq, norm_in_w[0], w_in[0], conv_qkv_w[0], A_log[0], dt_bias[0],
                 gdn_norm_w[0], conv_w[0], conv_b[0], w_out[0], final_norm_w)
    return out.reshape(batch, seq, d_model)
```

```python
import jax
import jax.numpy as jnp
from jax import lax
from jax.experimental import pallas as pl
from jax.experimental.pallas import tpu as pltpu

D_MODEL = 1024
HEADS = 8
HEAD_DIM = 128
GDN_WIDTH = HEADS * HEAD_DIM
CONV_WIDTH = 1024
GDN_CONV = 4
SHORT_CONV = 3
CHUNK = 64
EPS = 1e-6
LANES = 128
HALO = 8
ROW_STRIDE = 4

COL_Z = 3 * GDN_WIDTH
COL_BETA = COL_Z + GDN_WIDTH
COL_A = COL_BETA + LANES
COL_CONV = COL_A + LANES
PROJ_PACKED = COL_CONV + 4 * CONV_WIDTH

TL = 256
VMEM_LIMIT = 60 * 1024 * 1024


def _bdot(a, b):
    return jnp.dot(a.astype(jnp.bfloat16), b.astype(jnp.bfloat16),
                   preferred_element_type=jnp.float32)


def _bdot_nt(a, b):
    return lax.dot_general(a.astype(jnp.bfloat16), b.astype(jnp.bfloat16),
                           (((1,), (1,)), ((), ())), preferred_element_type=jnp.float32)


def _bdot_tn(a, b):
    return lax.dot_general(a.astype(jnp.bfloat16), b.astype(jnp.bfloat16),
                           (((0,), (0,)), ((), ())), preferred_element_type=jnp.float32)


def _silu(x):
    return x * jax.nn.sigmoid(x)


def _rmsnorm(x, w):
    ms = jnp.mean(x * x, axis=-1, keepdims=True)
    return x * lax.rsqrt(ms + EPS) * w


def _fill_slabs(slab_ref, first, x):
    for j in range(x.shape[1] // LANES):
        slab_ref[first + j, HALO:, :] = x[:, j * LANES:(j + 1) * LANES]


def _conv_row_classes(slab_ref, j, w_ref, lane0, width):
    taps = [w_ref[width - 1 - s:width - s, lane0:lane0 + LANES] for s in range(width)]
    loaded = {}

    def rows_from(start):
        if start not in loaded:
            loaded[start] = slab_ref[j, pl.ds(start, TL // ROW_STRIDE, stride=ROW_STRIDE), :]
        return loaded[start]

    out = []
    for r in range(ROW_STRIDE):
        acc = rows_from(HALO + r) * taps[0]
        for s in range(1, width):
            acc = acc + rows_from(HALO + r - s) * taps[s]
        out.append(acc)
    slab_ref[j, :HALO, :] = slab_ref[j, TL:, :]
    return out


def _store_row_class(dst_ref, j, r, val):
    dst_ref[j, pl.ds(r, TL // ROW_STRIDE, stride=ROW_STRIDE), :] = val


def _block_kernel(x_ref, nw_ref, wp_ref, cw_ref, alog_ref, dtb_ref, gnw_ref, ccw_ref, cb_ref,
                  wo_ref, fnw_ref, out_ref,
                  s_ref, pre_slab, ch_slab, cv_s, q_s, k_s, v_s, z_s, beta_s, gc_s, gct_s, u_s,
                  w_s, attn_s, o_s):
    n_chunks = TL // CHUNK
    heads = range(HEADS)
    lanes = [slice(h * HEAD_DIM, (h + 1) * HEAD_DIM) for h in heads]
    col = lambda blk, h: blk[:, h:h + 1]

    @pl.when(pl.program_id(1) == 0)
    def _():
        s_ref[...] = jnp.zeros_like(s_ref)
        pre_slab[:, :HALO, :] = jnp.zeros((pre_slab.shape[0], HALO, LANES), jnp.float32)
        ch_slab[:, :HALO, :] = jnp.zeros((ch_slab.shape[0], HALO, LANES), jnp.float32)

    x = x_ref[...]
    hn = _rmsnorm(x, nw_ref[...]).astype(jnp.bfloat16)
    proj = lambda lo, hi: jnp.dot(hn, wp_ref[:, lo:hi], preferred_element_type=jnp.float32)
    conv_col = lambda j: proj(COL_CONV + j * CONV_WIDTH, COL_CONV + (j + 1) * CONV_WIDTH)

    ba = proj(COL_BETA, COL_CONV)
    beta_s[...] = jax.nn.sigmoid(ba[:, :LANES])
    a_in = ba[:, LANES:] + dtb_ref[...]
    softplus = jnp.maximum(a_in, 0.0) + jnp.log1p(jnp.exp(-jnp.abs(a_in)))
    gc = -jnp.exp(alog_ref[...]) * softplus
    pos = lax.broadcasted_iota(jnp.int32, gc.shape, 0) % CHUNK
    s = 1
    while s < CHUNK:
        gc = gc + jnp.where(pos >= s, pltpu.roll(gc, s, axis=0), 0.0)
        s *= 2
    gc_s[...] = gc
    gct = gc.T
    for c in range(n_chunks):
        blk = gct[:, c * CHUNK:(c + 1) * CHUNK]
        gct_s[c] = jnp.concatenate([blk, blk], axis=1)

    conv_in = []
    for grp, dst in enumerate((q_s, k_s, v_s)):
        col0 = grp * GDN_WIDTH
        _fill_slabs(pre_slab, grp * HEADS, proj(col0, col0 + GDN_WIDTH))
        conv_in.append(conv_col(grp))
        scale = HEAD_DIM ** -0.5 if grp == 0 else 1.0
        for h in heads:
            classes = _conv_row_classes(pre_slab, grp * HEADS + h, cw_ref, col0 + h * HEAD_DIM,
                                        GDN_CONV)
            for r, acc in enumerate(classes):
                act = _silu(acc)
                if grp < 2:
                    inv = lax.rsqrt(jnp.sum(act * act, axis=-1, keepdims=True) + EPS)
                    act = act * inv * scale
                _store_row_class(dst, h, r, act)
    z_s[...] = proj(COL_Z, COL_BETA)
    gate_c = conv_col(3)

    gate_b, gate_cc, h_c = conv_in
    _fill_slabs(ch_slab, 0, gate_cc * h_c)
    for j in range(CONV_WIDTH // LANES):
        for r, acc in enumerate(_conv_row_classes(ch_slab, j, ccw_ref, j * LANES, SHORT_CONV)):
            _store_row_class(cv_s, j, r, acc)
    conv = jnp.concatenate([cv_s[j] for j in range(CONV_WIDTH // LANES)], axis=1)
    y = gate_b * (conv + cb_ref[...]) * _silu(gate_c)
    res = x + _bdot(y, wo_ref[GDN_WIDTH:, :])

    ri = lax.broadcasted_iota(jnp.int32, (CHUNK, 2 * CHUNK), 0)
    ci = lax.broadcasted_iota(jnp.int32, (CHUNK, 2 * CHUNK), 1)
    cj = ci % CHUNK
    causal2 = ri >= cj
    strict2 = ri > cj
    left = ci < CHUNK
    eye_right = jnp.where((ri == cj) & (ci >= CHUNK), 1.0, 0.0)

    items = []
    for c in range(n_chunks):
        rows = slice(c * CHUNK, (c + 1) * CHUNK)
        beta_blk = beta_s[rows, :]
        gc_blk = gc_s[rows, :]
        eg_blk = jnp.exp(gc_blk)
        gct_blk = gct_s[c]
        for h in heads:
            items.append((rows, h, col(beta_blk, h), col(gc_blk, h), col(eg_blk, h),
                          gct_blk[h:h + 1, :]))
    n_items = range(len(items))
    k = [k_s[h, rows, :] for rows, h, *_ in items]
    kb = [k[n] * items[n][2] for n in n_items]
    decay2 = [jnp.where(causal2, jnp.exp(it[3] - it[5]), 0.0) for it in items]
    kq = [_bdot_nt(jnp.concatenate([kb[n], q_s[it[1], it[0], :]], axis=0),
                   jnp.concatenate([k[n], k[n]], axis=0)) for n, it in enumerate(items)]
    for n, it in enumerate(items):
        attn_s[it[1], it[0], :] = kq[n][CHUNK:, :CHUNK] * decay2[n][:, :CHUNK]
    a2 = [jnp.where(strict2, kq[n][:CHUNK] * decay2[n], 0.0) for n in n_items]
    prod = [_bdot(a[:, :CHUNK], a) for a in a2]
    pt = [jnp.where(left, prod[n], eye_right - a2[n]) for n in n_items]
    m = 4
    while m < CHUNK:
        prod = [_bdot(p[:, :CHUNK], p) for p in pt]
        pt = [jnp.where(left, prod[n], pt[n] + prod[n]) for n in n_items]
        m *= 2
    prod = [_bdot(p[:, :CHUNK], p) for p in pt]
    t_mat = [(pt[n] + prod[n])[:, CHUNK:] for n in n_items]
    rhs = [jnp.concatenate([v_s[it[1], it[0], :] * it[2], kb[n] * it[4]], axis=1)
           for n, it in enumerate(items)]
    sol = [_bdot(t_mat[n], rhs[n]) for n in n_items]
    for n, it in enumerate(items):
        u_s[it[0], lanes[it[1]]] = sol[n][:, :HEAD_DIM]
        w_s[it[0], lanes[it[1]]] = sol[n][:, HEAD_DIM:]

    state = [s_ref[h] for h in heads]
    for c in range(n_chunks):
        rows = slice(c * CHUNK, (c + 1) * CHUNK)
        gc_blk = gc_s[rows, :]
        g_last = gc_blk[CHUNK - 1:CHUNK, :]
        eg_blk = jnp.exp(gc_blk)
        ek_blk = jnp.exp(g_last - gc_blk)
        egl_blk = jnp.exp(g_last)
        ws_qs = [_bdot(jnp.concatenate([w_s[rows, lanes[h]],
                                        q_s[h, rows, :] * col(eg_blk, h)], axis=0), state[h])
                 for h in heads]
        v_new = [u_s[rows, lanes[h]] - ws_qs[h][:CHUNK] for h in heads]
        o = [ws_qs[h][CHUNK:] + _bdot(attn_s[h, rows, :], v_new[h]) for h in heads]
        kv = [_bdot_tn(k_s[h, rows, :] * col(ek_blk, h), v_new[h]) for h in heads]
        state = [state[h] * col(egl_blk, h) + kv[h] for h in heads]
        for h in heads:
            o_s[rows, lanes[h]] = _rmsnorm(o[h], gnw_ref[...]) * _silu(z_s[rows, lanes[h]])
    for h in heads:
        s_ref[h] = state[h]

    res = res + _bdot(o_s[...], wo_ref[:GDN_WIDTH, :])
    out_ref[...] = _rmsnorm(res, fnw_ref[...])


def _block(x2d, norm_w, w_packed, conv_qkv_w, a_log_pad, dt_bias_pad, gdn_norm_w, conv_w, conv_b,
           w_out, out_norm_w, batch, seq):
    tokens = batch * seq
    tiles = seq // TL
    n_chunks = TL // CHUNK
    row = lambda b, t: (b * tiles + t, 0)
    const = lambda b, t: (0, 0)
    resident = lambda shape: pl.BlockSpec(shape, const, pipeline_mode=pl.Buffered(1))
    tile_f32 = pltpu.VMEM((TL, GDN_WIDTH), jnp.float32)
    head_tile_f32 = pltpu.VMEM((HEADS, TL, HEAD_DIM), jnp.float32)
    return pl.pallas_call(
        _block_kernel,
        grid=(batch, tiles),
        in_specs=[
            pl.BlockSpec((TL, D_MODEL), row),
            resident((1, D_MODEL)),
            resident((D_MODEL, PROJ_PACKED)),
            resident((GDN_CONV, 3 * GDN_WIDTH)),
            resident((1, LANES)),
            resident((1, LANES)),
            resident((1, HEAD_DIM)),
            resident((SHORT_CONV, CONV_WIDTH)),
            resident((1, CONV_WIDTH)),
            resident((GDN_WIDTH + CONV_WIDTH, D_MODEL)),
            resident((1, D_MODEL)),
        ],
        out_specs=pl.BlockSpec((TL, D_MODEL), row),
        out_shape=jax.ShapeDtypeStruct((tokens, D_MODEL), jnp.float32),
        scratch_shapes=[
            pltpu.VMEM((HEADS, HEAD_DIM, HEAD_DIM), jnp.float32),
            pltpu.VMEM((3 * HEADS, HALO + TL, LANES), jnp.float32),
            pltpu.VMEM((CONV_WIDTH // LANES, HALO + TL, LANES), jnp.float32),
            pltpu.VMEM((CONV_WIDTH // LANES, TL, LANES), jnp.float32),
            head_tile_f32, head_tile_f32, head_tile_f32,
            tile_f32,
            pltpu.VMEM((TL, LANES), jnp.float32),
            pltpu.VMEM((TL, LANES), jnp.float32),
            pltpu.VMEM((n_chunks, LANES, 2 * CHUNK), jnp.float32),
            tile_f32, tile_f32,
            pltpu.VMEM((HEADS, TL, CHUNK), jnp.float32),
            tile_f32,
        ],
        compiler_params=pltpu.CompilerParams(
            dimension_semantics=("arbitrary", "arbitrary"), vmem_limit_bytes=VMEM_LIMIT),
        name="hybrid_block",
    )(x2d, norm_w, w_packed, conv_qkv_w, a_log_pad, dt_bias_pad, gdn_norm_w, conv_w, conv_b,
      w_out, out_norm_w)


def _pad_lanes(v):
    return jnp.pad(v, ((0, 0), (0, LANES - v.shape[-1])))


def _layer(x2d, batch, seq, norm_in_w, w_in, conv_qkv_w, a_log, dt_bias, gdn_norm_w,
           conv_w, conv_b, w_out, out_norm_w):
    n_gdn = COL_BETA
    w_bf = w_in.astype(jnp.bfloat16)
    w_packed = jnp.concatenate(
        [w_bf[:, :n_gdn], _pad_lanes(w_bf[:, n_gdn:n_gdn + HEADS]),
         _pad_lanes(w_bf[:, n_gdn + HEADS:n_gdn + 2 * HEADS]), w_bf[:, n_gdn + 2 * HEADS:]],
        axis=1)
    return _block(x2d, norm_in_w.reshape(1, -1), w_packed, conv_qkv_w,
                  _pad_lanes(a_log.reshape(1, -1)), _pad_lanes(dt_bias.reshape(1, -1)),
                  gdn_norm_w.reshape(1, -1), conv_w, conv_b.reshape(1, -1),
                  w_out.astype(jnp.bfloat16), out_norm_w.reshape(1, -1), batch, seq)


def kernel(x, norm_in_w, w_in, conv_qkv_w, A_log, dt_bias, gdn_norm_w, conv_w, conv_b, w_out,
           final_norm_w):
    batch, seq, d_model = x.shape
    assert norm_in_w.shape[0] == 1 and d_model == D_MODEL and seq % TL == 0
    x2d = x.reshape(batch * seq, d_model)
    out = _layer(x2d, batch, seq, norm_in_w[0], w_in[0], conv_qkv_w[0], A_log[0], dt_bias[0],
                 gdn_norm_w[0], conv_w[0], conv_b[0], w_out[0], final_norm_w)
    return out.reshape(batch, seq, d_model)
```

```python
import jax
import jax.numpy as jnp
from jax import lax
from jax.experimental import pallas as pl
from jax.experimental.pallas import tpu as pltpu

D_MODEL = 1024
HEADS = 8
HEAD_DIM = 128
GDN_WIDTH = HEADS * HEAD_DIM
CONV_WIDTH = 1024
GDN_CONV = 4
SHORT_CONV = 3
CHUNK = 64
EPS = 1e-6
LANES = 128
HALO = 8
ROW_STRIDE = 4

COL_Z = 3 * GDN_WIDTH
COL_CONV = COL_Z + GDN_WIDTH
PROJ_MAIN = COL_CONV + 4 * CONV_WIDTH
N_BA = 2 * HEADS
PACK_TN = 1024

TL = 256
VMEM_LIMIT = 60 * 1024 * 1024


def _bdot(a, b):
    return jnp.dot(a.astype(jnp.bfloat16), b.astype(jnp.bfloat16),
                   preferred_element_type=jnp.float32)


def _bdot_nt(a, b):
    return lax.dot_general(a.astype(jnp.bfloat16), b.astype(jnp.bfloat16),
                           (((1,), (1,)), ((), ())), preferred_element_type=jnp.float32)


def _bdot_tn(a, b):
    return lax.dot_general(a.astype(jnp.bfloat16), b.astype(jnp.bfloat16),
                           (((0,), (0,)), ((), ())), preferred_element_type=jnp.float32)


def _silu(x):
    return x * jax.nn.sigmoid(x)


def _rmsnorm(x, w):
    ms = jnp.mean(x * x, axis=-1, keepdims=True)
    return x * lax.rsqrt(ms + EPS) * w


def _fill_slabs(slab_ref, first, x):
    for j in range(x.shape[1] // LANES):
        slab_ref[first + j, HALO:, :] = x[:, j * LANES:(j + 1) * LANES]


def _conv_row_classes(slab_ref, j, w_ref, lane0, width):
    taps = [w_ref[width - 1 - s:width - s, lane0:lane0 + LANES] for s in range(width)]
    loaded = {}

    def rows_from(start):
        if start not in loaded:
            loaded[start] = slab_ref[j, pl.ds(start, TL // ROW_STRIDE, stride=ROW_STRIDE), :]
        return loaded[start]

    out = []
    for r in range(ROW_STRIDE):
        acc = rows_from(HALO + r) * taps[0]
        for s in range(1, width):
            acc = acc + rows_from(HALO + r - s) * taps[s]
        out.append(acc)
    slab_ref[j, :HALO, :] = slab_ref[j, TL:, :]
    return out


def _store_row_class(dst_ref, j, r, val):
    dst_ref[j, pl.ds(r, TL // ROW_STRIDE, stride=ROW_STRIDE), :] = val


def _pack_kernel(a_ref, b_ref, tail_ref, o_ref):
    j = pl.program_id(0)
    last = pl.num_programs(0) - 1

    @pl.when(j < COL_CONV // PACK_TN)
    def _():
        o_ref[...] = a_ref[0].astype(jnp.bfloat16)

    @pl.when(j >= COL_CONV // PACK_TN)
    def _():
        nxt = jnp.where(j == last, tail_ref[...], b_ref[0])
        o_ref[...] = jnp.concatenate([a_ref[0][:, N_BA:], nxt[:, :N_BA]],
                                     axis=1).astype(jnp.bfloat16)


def _pack_weights(w_in3, w_tail):
    n_blocks = PROJ_MAIN // PACK_TN
    per = PACK_TN // LANES
    return pl.pallas_call(
        _pack_kernel,
        grid=(n_blocks,),
        in_specs=[
            pl.BlockSpec((1, D_MODEL, PACK_TN), lambda j: (0, 0, j)),
            pl.BlockSpec((1, D_MODEL, LANES),
                         lambda j: (0, 0, jnp.minimum((j + 1) * per, n_blocks * per - 1))),
            pl.BlockSpec((D_MODEL, LANES), lambda j: (0, 0)),
        ],
        out_specs=pl.BlockSpec((D_MODEL, PACK_TN), lambda j: (0, j)),
        out_shape=jax.ShapeDtypeStruct((D_MODEL, PROJ_MAIN), jnp.bfloat16),
        compiler_params=pltpu.CompilerParams(
            dimension_semantics=("arbitrary",), vmem_limit_bytes=VMEM_LIMIT),
        name="pack_weights",
    )(w_in3, w_in3, w_tail)


def _block_kernel(x_ref, nw_ref, wp_ref, wba_ref, cw_ref, alog_ref, dtb_ref, gnw_ref, ccw_ref,
                  cb_ref, wo_ref, fnw_ref, out_ref,
                  s_ref, pre_slab, ch_slab, cv_s, q_s, k_s, v_s, z_s, beta_s, gc_s, gct_s, u_s,
                  w_s, attn_s, o_s):
    n_chunks = TL // CHUNK
    heads = range(HEADS)
    lanes = [slice(h * HEAD_DIM, (h + 1) * HEAD_DIM) for h in heads]
    col = lambda blk, h: blk[:, h:h + 1]

    @pl.when(pl.program_id(1) == 0)
    def _():
        s_ref[...] = jnp.zeros_like(s_ref)
        pre_slab[:, :HALO, :] = jnp.zeros((pre_slab.shape[0], HALO, LANES), jnp.float32)
        ch_slab[:, :HALO, :] = jnp.zeros((ch_slab.shape[0], HALO, LANES), jnp.float32)

    x = x_ref[...]
    hn = _rmsnorm(x, nw_ref[...]).astype(jnp.bfloat16)
    proj = lambda lo, hi: jnp.dot(hn, wp_ref[:, lo:hi], preferred_element_type=jnp.float32)
    conv_col = lambda j: proj(COL_CONV + j * CONV_WIDTH, COL_CONV + (j + 1) * CONV_WIDTH)

    ba = jnp.dot(hn, wba_ref[...], preferred_element_type=jnp.float32)
    beta_s[...] = jax.nn.sigmoid(ba[:, :LANES])
    a_in = ba[:, LANES:] + dtb_ref[...]
    softplus = jnp.maximum(a_in, 0.0) + jnp.log1p(jnp.exp(-jnp.abs(a_in)))
    gc = -jnp.exp(alog_ref[...]) * softplus
    pos = lax.broadcasted_iota(jnp.int32, gc.shape, 0) % CHUNK
    s = 1
    while s < CHUNK:
        gc = gc + jnp.where(pos >= s, pltpu.roll(gc, s, axis=0), 0.0)
        s *= 2
    gc_s[...] = gc
    gct = gc.T
    for c in range(n_chunks):
        blk = gct[:, c * CHUNK:(c + 1) * CHUNK]
        gct_s[c] = jnp.concatenate([blk, blk], axis=1)

    conv_in = []
    for grp, dst in enumerate((q_s, k_s, v_s)):
        col0 = grp * GDN_WIDTH
        _fill_slabs(pre_slab, grp * HEADS, proj(col0, col0 + GDN_WIDTH))
        conv_in.append(conv_col(grp))
        scale = HEAD_DIM ** -0.5 if grp == 0 else 1.0
        for h in heads:
            classes = _conv_row_classes(pre_slab, grp * HEADS + h, cw_ref, col0 + h * HEAD_DIM,
                                        GDN_CONV)
            for r, acc in enumerate(classes):
                act = _silu(acc)
                if grp < 2:
                    inv = lax.rsqrt(jnp.sum(act * act, axis=-1, keepdims=True) + EPS)
                    act = act * inv * scale
                _store_row_class(dst, h, r, act)
    z_s[...] = proj(COL_Z, COL_CONV)
    gate_c = conv_col(3)

    gate_b, gate_cc, h_c = conv_in
    _fill_slabs(ch_slab, 0, gate_cc * h_c)
    for j in range(CONV_WIDTH // LANES):
        for r, acc in enumerate(_conv_row_classes(ch_slab, j, ccw_ref, j * LANES, SHORT_CONV)):
            _store_row_class(cv_s, j, r, acc)
    conv = jnp.concatenate([cv_s[j] for j in range(CONV_WIDTH // LANES)], axis=1)
    y = gate_b * (conv + cb_ref[...]) * _silu(gate_c)
    res = x + _bdot(y, wo_ref[GDN_WIDTH:, :])

    ri = lax.broadcasted_iota(jnp.int32, (CHUNK, 2 * CHUNK), 0)
    ci = lax.broadcasted_iota(jnp.int32, (CHUNK, 2 * CHUNK), 1)
    cj = ci % CHUNK
    causal2 = ri >= cj
    strict2 = ri > cj
    left = ci < CHUNK
    eye_right = jnp.where((ri == cj) & (ci >= CHUNK), 1.0, 0.0)

    items = []
    for c in range(n_chunks):
        rows = slice(c * CHUNK, (c + 1) * CHUNK)
        beta_blk = beta_s[rows, :]
        gc_blk = gc_s[rows, :]
        eg_blk = jnp.exp(gc_blk)
        gct_blk = gct_s[c]
        for h in heads:
            items.append((rows, h, col(beta_blk, h), col(gc_blk, h), col(eg_blk, h),
                          gct_blk[h:h + 1, :]))
    n_items = range(len(items))
    k = [k_s[h, rows, :] for rows, h, *_ in items]
    kb = [k[n] * items[n][2] for n in n_items]
    decay2 = [jnp.where(causal2, jnp.exp(it[3] - it[5]), 0.0) for it in items]
    kq = [_bdot_nt(jnp.concatenate([kb[n], q_s[it[1], it[0], :]], axis=0),
                   jnp.concatenate([k[n], k[n]], axis=0)) for n, it in enumerate(items)]
    for n, it in enumerate(items):
        attn_s[it[1], it[0], :] = kq[n][CHUNK:, :CHUNK] * decay2[n][:, :CHUNK]
    a2 = [jnp.where(strict2, kq[n][:CHUNK] * decay2[n], 0.0) for n in n_items]
    prod = [_bdot(a[:, :CHUNK], a) for a in a2]
    pt = [jnp.where(left, prod[n], eye_right - a2[n]) for n in n_items]
    m = 4
    while m < CHUNK:
        prod = [_bdot(p[:, :CHUNK], p) for p in pt]
        pt = [jnp.where(left, prod[n], pt[n] + prod[n]) for n in n_items]
        m *= 2
    prod = [_bdot(p[:, :CHUNK], p) for p in pt]
    t_mat = [(pt[n] + prod[n])[:, CHUNK:] for n in n_items]
    rhs = [jnp.concatenate([v_s[it[1], it[0], :] * it[2], kb[n] * it[4]], axis=1)
           for n, it in enumerate(items)]
    sol = [_bdot(t_mat[n], rhs[n]) for n in n_items]
    for n, it in enumerate(items):
        u_s[it[0], lanes[it[1]]] = sol[n][:, :HEAD_DIM]
        w_s[it[0], lanes[it[1]]] = sol[n][:, HEAD_DIM:]

    state = [s_ref[h] for h in heads]
    for c in range(n_chunks):
        rows = slice(c * CHUNK, (c + 1) * CHUNK)
        gc_blk = gc_s[rows, :]
        g_last = gc_blk[CHUNK - 1:CHUNK, :]
        eg_blk = jnp.exp(gc_blk)
        ek_blk = jnp.exp(g_last - gc_blk)
        egl_blk = jnp.exp(g_last)
        ws_qs = [_bdot(jnp.concatenate([w_s[rows, lanes[h]],
                                        q_s[h, rows, :] * col(eg_blk, h)], axis=0), state[h])
                 for h in heads]
        v_new = [u_s[rows, lanes[h]] - ws_qs[h][:CHUNK] for h in heads]
        o = [ws_qs[h][CHUNK:] + _bdot(attn_s[h, rows, :], v_new[h]) for h in heads]
        kv = [_bdot_tn(k_s[h, rows, :] * col(ek_blk, h), v_new[h]) for h in heads]
        state = [state[h] * col(egl_blk, h) + kv[h] for h in heads]
        for h in heads:
            o_s[rows, lanes[h]] = _rmsnorm(o[h], gnw_ref[...]) * _silu(z_s[rows, lanes[h]])
    for h in heads:
        s_ref[h] = state[h]

    res = res + _bdot(o_s[...], wo_ref[:GDN_WIDTH, :])
    out_ref[...] = _rmsnorm(res, fnw_ref[...])


def _block(x2d, norm_w, w_packed, w_ba, conv_qkv_w, a_log_pad, dt_bias_pad, gdn_norm_w, conv_w,
           conv_b, w_out, out_norm_w, batch, seq):
    tokens = batch * seq
    tiles = seq // TL
    n_chunks = TL // CHUNK
    row = lambda b, t: (b * tiles + t, 0)
    const = lambda b, t: (0, 0)
    resident = lambda shape: pl.BlockSpec(shape, const, pipeline_mode=pl.Buffered(1))
    tile_f32 = pltpu.VMEM((TL, GDN_WIDTH), jnp.float32)
    head_tile_f32 = pltpu.VMEM((HEADS, TL, HEAD_DIM), jnp.float32)
    return pl.pallas_call(
        _block_kernel,
        grid=(batch, tiles),
        in_specs=[
            pl.BlockSpec((TL, D_MODEL), row),
            resident((1, D_MODEL)),
            resident((D_MODEL, PROJ_MAIN)),
            resident((D_MODEL, 2 * LANES)),
            resident((GDN_CONV, 3 * GDN_WIDTH)),
            resident((1, LANES)),
            resident((1, LANES)),
            resident((1, HEAD_DIM)),
            resident((SHORT_CONV, CONV_WIDTH)),
            resident((1, CONV_WIDTH)),
            resident((GDN_WIDTH + CONV_WIDTH, D_MODEL)),
            resident((1, D_MODEL)),
        ],
        out_specs=pl.BlockSpec((TL, D_MODEL), row),
        out_shape=jax.ShapeDtypeStruct((tokens, D_MODEL), jnp.float32),
        scratch_shapes=[
            pltpu.VMEM((HEADS, HEAD_DIM, HEAD_DIM), jnp.float32),
            pltpu.VMEM((3 * HEADS, HALO + TL, LANES), jnp.float32),
            pltpu.VMEM((CONV_WIDTH // LANES, HALO + TL, LANES), jnp.float32),
            pltpu.VMEM((CONV_WIDTH // LANES, TL, LANES), jnp.float32),
            head_tile_f32, head_tile_f32, head_tile_f32,
            tile_f32,
            pltpu.VMEM((TL, LANES), jnp.float32),
            pltpu.VMEM((TL, LANES), jnp.float32),
            pltpu.VMEM((n_chunks, LANES, 2 * CHUNK), jnp.float32),
            tile_f32, tile_f32,
            pltpu.VMEM((HEADS, TL, CHUNK), jnp.float32),
            tile_f32,
        ],
        compiler_params=pltpu.CompilerParams(
            dimension_semantics=("arbitrary", "arbitrary"), vmem_limit_bytes=VMEM_LIMIT),
        name="hybrid_block",
    )(x2d, norm_w, w_packed, w_ba, conv_qkv_w, a_log_pad, dt_bias_pad, gdn_norm_w, conv_w, conv_b,
      w_out, out_norm_w)


def _pad_lanes(v):
    return jnp.pad(v, ((0, 0), (0, LANES - v.shape[-1])))


def _layer(x2d, batch, seq, norm_in_w, w_in3, conv_qkv_w, a_log, dt_bias, gdn_norm_w,
           conv_w, conv_b, w_out, out_norm_w):
    w_in = w_in3[0]
    assert w_in.shape[1] == PROJ_MAIN + N_BA
    w_packed = _pack_weights(w_in3, _pad_lanes(w_in[:, PROJ_MAIN:]))
    w_ba = jnp.concatenate([_pad_lanes(w_in[:, COL_CONV:COL_CONV + HEADS]),
                            _pad_lanes(w_in[:, COL_CONV + HEADS:COL_CONV + N_BA])],
                           axis=1).astype(jnp.bfloat16)
    return _block(x2d, norm_in_w.reshape(1, -1), w_packed, w_ba, conv_qkv_w,
                  _pad_lanes(a_log.reshape(1, -1)), _pad_lanes(dt_bias.reshape(1, -1)),
                  gdn_norm_w.reshape(1, -1), conv_w, conv_b.reshape(1, -1),
                  w_out.astype(jnp.bfloat16), out_norm_w.reshape(1, -1), batch, seq)


def kernel(x, norm_in_w, w_in, conv_qkv_w, A_log, dt_bias, gdn_norm_w, conv_w, conv_b, w_out,
           final_norm_w):
    batch, seq, d_model = x.shape
    assert norm_in_w.shape[0] == 1 and d_model == D_MODEL and seq % TL == 0
    x2d = x.reshape(batch * seq, d_model)
    out = _layer(x2d, batch, seq, norm_in_w[0], w_in, conv_qkv_w[0], A_log[0], dt_bias[0],
                 gdn_norm_w[0], conv_w[0], conv_b[0], w_out[0], final_norm_w)
    return out.reshape(batch, seq, d_model)
```

```python
import jax
import jax.numpy as jnp
from jax import lax
from jax.experimental import pallas as pl
from jax.experimental.pallas import tpu as pltpu

D_MODEL = 1024
HEADS = 8
HEAD_DIM = 128
GDN_WIDTH = HEADS * HEAD_DIM
CONV_WIDTH = 1024
GDN_CONV = 4
SHORT_CONV = 3
CHUNK = 64
EPS = 1e-6
LANES = 128
HALO = 8
ROW_STRIDE = 4

COL_Z = 3 * GDN_WIDTH
COL_BETA = COL_Z + GDN_WIDTH
COL_A = COL_BETA + LANES
COL_CONV = COL_A + LANES
PROJ_PACKED = COL_CONV + 4 * CONV_WIDTH

TL = 256
PIECE = 256
SOLVE_STAGES = 8
VMEM_LIMIT = 60 * 1024 * 1024


def _bdot(a, b):
    return jnp.dot(a.astype(jnp.bfloat16), b.astype(jnp.bfloat16),
                   preferred_element_type=jnp.float32)


def _bdot_nt(a, b):
    return lax.dot_general(a.astype(jnp.bfloat16), b.astype(jnp.bfloat16),
                           (((1,), (1,)), ((), ())), preferred_element_type=jnp.float32)


def _bdot_tn(a, b):
    return lax.dot_general(a.astype(jnp.bfloat16), b.astype(jnp.bfloat16),
                           (((0,), (0,)), ((), ())), preferred_element_type=jnp.float32)


def _silu(x):
    return x * jax.nn.sigmoid(x)


def _rmsnorm(x, w):
    ms = jnp.mean(x * x, axis=-1, keepdims=True)
    return x * lax.rsqrt(ms + EPS) * w


def _fill_slabs(slab_ref, first, x):
    for j in range(x.shape[1] // LANES):
        slab_ref[first + j, HALO:, :] = x[:, j * LANES:(j + 1) * LANES]


def _conv_row_classes(slab_ref, j, w_ref, lane0, width):
    taps = [w_ref[width - 1 - s:width - s, lane0:lane0 + LANES] for s in range(width)]
    loaded = {}

    def rows_from(start):
        if start not in loaded:
            loaded[start] = slab_ref[j, pl.ds(start, TL // ROW_STRIDE, stride=ROW_STRIDE), :]
        return loaded[start]

    out = []
    for r in range(ROW_STRIDE):
        acc = rows_from(HALO + r) * taps[0]
        for s in range(1, width):
            acc = acc + rows_from(HALO + r - s) * taps[s]
        out.append(acc)
    slab_ref[j, :HALO, :] = slab_ref[j, TL:, :]
    return out


def _store_row_class(dst_ref, j, r, val):
    dst_ref[j, pl.ds(r, TL // ROW_STRIDE, stride=ROW_STRIDE), :] = val


def _block_kernel(x_ref, nw_ref, wp_ref, cw_ref, alog_ref, dtb_ref, gnw_ref, ccw_ref, cb_ref,
                  wo_ref, fnw_ref, out_ref,
                  s_ref, pre_slab, ch_slab, cv_s, q_s, k_s, v_s, z_s, cb_s, beta_s, gc_s, gct_s,
                  u_s, w_s, attn_s, o_s, res_s):
    n_chunks = TL // CHUNK
    heads = range(HEADS)
    lanes = [slice(h * HEAD_DIM, (h + 1) * HEAD_DIM) for h in heads]
    col = lambda blk, h: blk[:, h:h + 1]

    @pl.when(pl.program_id(1) == 0)
    def _():
        s_ref[...] = jnp.zeros_like(s_ref)
        pre_slab[:, :HALO, :] = jnp.zeros((pre_slab.shape[0], HALO, LANES), jnp.float32)
        ch_slab[:, :HALO, :] = jnp.zeros((ch_slab.shape[0], HALO, LANES), jnp.float32)

    x = x_ref[...]
    hn = _rmsnorm(x, nw_ref[...]).astype(jnp.bfloat16)
    proj = lambda lo, hi: jnp.dot(hn, wp_ref[:, lo:hi], preferred_element_type=jnp.float32)

    ba = proj(COL_BETA, COL_CONV)
    beta_s[...] = jax.nn.sigmoid(ba[:, :LANES])
    a_in = ba[:, LANES:] + dtb_ref[...]
    softplus = jnp.maximum(a_in, 0.0) + jnp.log1p(jnp.exp(-jnp.abs(a_in)))
    gc = -jnp.exp(alog_ref[...]) * softplus
    pos = lax.broadcasted_iota(jnp.int32, gc.shape, 0) % CHUNK
    s = 1
    while s < CHUNK:
        gc = gc + jnp.where(pos >= s, pltpu.roll(gc, s, axis=0), 0.0)
        s *= 2
    gc_s[...] = gc
    gct = gc.T
    for c in range(n_chunks):
        blk = gct[:, c * CHUNK:(c + 1) * CHUNK]
        gct_s[c] = jnp.concatenate([blk, blk], axis=1)

    for grp, dst in enumerate((q_s, k_s, v_s)):
        col0 = grp * GDN_WIDTH
        _fill_slabs(pre_slab, grp * HEADS, proj(col0, col0 + GDN_WIDTH))
        if grp < 2:
            cb_s[grp] = proj(COL_CONV + grp * CONV_WIDTH, COL_CONV + (grp + 1) * CONV_WIDTH)
        scale = HEAD_DIM ** -0.5 if grp == 0 else 1.0
        for h in heads:
            classes = _conv_row_classes(pre_slab, grp * HEADS + h, cw_ref, col0 + h * HEAD_DIM,
                                        GDN_CONV)
            for r, acc in enumerate(classes):
                act = _silu(acc)
                if grp < 2:
                    inv = lax.rsqrt(jnp.sum(act * act, axis=-1, keepdims=True) + EPS)
                    act = act * inv * scale
                _store_row_class(dst, h, r, act)

    def proj_piece(dst_ref, lead, col0, p):
        def run():
            lo = p * PIECE
            val = proj(col0 + lo, col0 + lo + PIECE)
            if lead is None:
                dst_ref[:, lo:lo + PIECE] = val
            else:
                dst_ref[lead, :, lo:lo + PIECE] = val
        return run

    pieces = range(CONV_WIDTH // PIECE)
    fill = [proj_piece(z_s, None, COL_Z, p) for p in pieces]
    fill += [proj_piece(cb_s, j, COL_CONV + j * CONV_WIDTH, p) for j in (2, 3) for p in pieces]

    ri = lax.broadcasted_iota(jnp.int32, (CHUNK, 2 * CHUNK), 0)
    ci = lax.broadcasted_iota(jnp.int32, (CHUNK, 2 * CHUNK), 1)
    cj = ci % CHUNK
    causal2 = ri >= cj
    strict2 = ri > cj
    left = ci < CHUNK
    eye_right = jnp.where((ri == cj) & (ci >= CHUNK), 1.0, 0.0)

    def solve_stages():
        items = []
        for c in range(n_chunks):
            rows = slice(c * CHUNK, (c + 1) * CHUNK)
            beta_blk = beta_s[rows, :]
            gc_blk = gc_s[rows, :]
            eg_blk = jnp.exp(gc_blk)
            gct_blk = gct_s[c]
            for h in heads:
                items.append((rows, h, col(beta_blk, h), col(gc_blk, h), col(eg_blk, h),
                              gct_blk[h:h + 1, :]))
        n_items = range(len(items))
        k = [k_s[h, rows, :] for rows, h, *_ in items]
        kb = [k[n] * items[n][2] for n in n_items]
        decay2 = [jnp.where(causal2, jnp.exp(it[3] - it[5]), 0.0) for it in items]
        kq = [_bdot_nt(jnp.concatenate([kb[n], q_s[it[1], it[0], :]], axis=0),
                       jnp.concatenate([k[n], k[n]], axis=0)) for n, it in enumerate(items)]
        yield
        for n, it in enumerate(items):
            attn_s[it[1], it[0], :] = kq[n][CHUNK:, :CHUNK] * decay2[n][:, :CHUNK]
        a2 = [jnp.where(strict2, kq[n][:CHUNK] * decay2[n], 0.0) for n in n_items]
        prod = [_bdot(a[:, :CHUNK], a) for a in a2]
        yield
        pt = [jnp.where(left, prod[n], eye_right - a2[n]) for n in n_items]
        m = 4
        while m < CHUNK:
            prod = [_bdot(p[:, :CHUNK], p) for p in pt]
            yield
            pt = [jnp.where(left, prod[n], pt[n] + prod[n]) for n in n_items]
            m *= 2
        prod = [_bdot(p[:, :CHUNK], p) for p in pt]
        yield
        t_mat = [(pt[n] + prod[n])[:, CHUNK:] for n in n_items]
        rhs = [jnp.concatenate([v_s[it[1], it[0], :] * it[2], kb[n] * it[4]], axis=1)
               for n, it in enumerate(items)]
        sol = [_bdot(t_mat[n], rhs[n]) for n in n_items]
        yield
        for n, it in enumerate(items):
            u_s[it[0], lanes[it[1]]] = sol[n][:, :HEAD_DIM]
            w_s[it[0], lanes[it[1]]] = sol[n][:, HEAD_DIM:]

    def recurrence_stages():
        state = [s_ref[h] for h in heads]
        for c in range(n_chunks):
            rows = slice(c * CHUNK, (c + 1) * CHUNK)
            gc_blk = gc_s[rows, :]
            g_last = gc_blk[CHUNK - 1:CHUNK, :]
            eg_blk = jnp.exp(gc_blk)
            ek_blk = jnp.exp(g_last - gc_blk)
            egl_blk = jnp.exp(g_last)
            ws_qs = [_bdot(jnp.concatenate([w_s[rows, lanes[h]],
                                            q_s[h, rows, :] * col(eg_blk, h)], axis=0), state[h])
                     for h in heads]
            yield
            v_new = [u_s[rows, lanes[h]] - ws_qs[h][:CHUNK] for h in heads]
            o = [ws_qs[h][CHUNK:] + _bdot(attn_s[h, rows, :], v_new[h]) for h in heads]
            kv = [_bdot_tn(k_s[h, rows, :] * col(ek_blk, h), v_new[h]) for h in heads]
            state = [state[h] * col(egl_blk, h) + kv[h] for h in heads]
            for h in heads:
                o_s[rows, lanes[h]] = _rmsnorm(o[h], gnw_ref[...]) * _silu(z_s[rows, lanes[h]])
            yield
        for h in heads:
            s_ref[h] = state[h]

    solve = solve_stages()
    for i in range(SOLVE_STAGES):
        next(solve)
        for run in fill[i * len(fill) // SOLVE_STAGES:(i + 1) * len(fill) // SOLVE_STAGES]:
            run()
    for _ in solve:
        pass

    _fill_slabs(ch_slab, 0, cb_s[1] * cb_s[2])
    for j in range(CONV_WIDTH // LANES):
        for r, acc in enumerate(_conv_row_classes(ch_slab, j, ccw_ref, j * LANES, SHORT_CONV)):
            _store_row_class(cv_s, j, r, acc)
    conv = jnp.concatenate([cv_s[j] for j in range(CONV_WIDTH // LANES)], axis=1)
    y = (cb_s[0] * (conv + cb_ref[...]) * _silu(cb_s[3])).astype(jnp.bfloat16)

    def out_piece(p):
        lo = p * PIECE
        res_s[:, lo:lo + PIECE] = x[:, lo:lo + PIECE] + jnp.dot(
            y, wo_ref[GDN_WIDTH:, lo:lo + PIECE], preferred_element_type=jnp.float32)

    for i, _ in enumerate(recurrence_stages()):
        if i % 2 == 0:
            out_piece(i // 2)

    res = res_s[...] + _bdot(o_s[...], wo_ref[:GDN_WIDTH, :])
    out_ref[...] = _rmsnorm(res, fnw_ref[...])


def _block(x2d, norm_w, w_packed, conv_qkv_w, a_log_pad, dt_bias_pad, gdn_norm_w, conv_w, conv_b,
           w_out, out_norm_w, batch, seq):
    tokens = batch * seq
    tiles = seq // TL
    n_chunks = TL // CHUNK
    row = lambda b, t: (b * tiles + t, 0)
    const = lambda b, t: (0, 0)
    resident = lambda shape: pl.BlockSpec(shape, const, pipeline_mode=pl.Buffered(1))
    tile_f32 = pltpu.VMEM((TL, GDN_WIDTH), jnp.float32)
    head_tile_f32 = pltpu.VMEM((HEADS, TL, HEAD_DIM), jnp.float32)
    return pl.pallas_call(
        _block_kernel,
        grid=(batch, tiles),
        in_specs=[
            pl.BlockSpec((TL, D_MODEL), row),
            resident((1, D_MODEL)),
            resident((D_MODEL, PROJ_PACKED)),
            resident((GDN_CONV, 3 * GDN_WIDTH)),
            resident((1, LANES)),
            resident((1, LANES)),
            resident((1, HEAD_DIM)),
            resident((SHORT_CONV, CONV_WIDTH)),
            resident((1, CONV_WIDTH)),
            resident((GDN_WIDTH + CONV_WIDTH, D_MODEL)),
            resident((1, D_MODEL)),
        ],
        out_specs=pl.BlockSpec((TL, D_MODEL), row),
        out_shape=jax.ShapeDtypeStruct((tokens, D_MODEL), jnp.float32),
        scratch_shapes=[
            pltpu.VMEM((HEADS, HEAD_DIM, HEAD_DIM), jnp.float32),
            pltpu.VMEM((3 * HEADS, HALO + TL, LANES), jnp.float32),
            pltpu.VMEM((CONV_WIDTH // LANES, HALO + TL, LANES), jnp.float32),
            pltpu.VMEM((CONV_WIDTH // LANES, TL, LANES), jnp.float32),
            head_tile_f32, head_tile_f32, head_tile_f32,
            tile_f32,
            pltpu.VMEM((4, TL, CONV_WIDTH), jnp.float32),
            pltpu.VMEM((TL, LANES), jnp.float32),
            pltpu.VMEM((TL, LANES), jnp.float32),
            pltpu.VMEM((n_chunks, LANES, 2 * CHUNK), jnp.float32),
            tile_f32, tile_f32,
            pltpu.VMEM((HEADS, TL, CHUNK), jnp.float32),
            tile_f32,
            tile_f32,
        ],
        compiler_params=pltpu.CompilerParams(
            dimension_semantics=("arbitrary", "arbitrary"), vmem_limit_bytes=VMEM_LIMIT),
        name="hybrid_block",
    )(x2d, norm_w, w_packed, conv_qkv_w, a_log_pad, dt_bias_pad, gdn_norm_w, conv_w, conv_b,
      w_out, out_norm_w)


def _pad_lanes(v):
    return jnp.pad(v, ((0, 0), (0, LANES - v.shape[-1])))


def _layer(x2d, batch, seq, norm_in_w, w_in, conv_qkv_w, a_log, dt_bias, gdn_norm_w,
           conv_w, conv_b, w_out, out_norm_w):
    n_gdn = COL_BETA
    w_bf = w_in.astype(jnp.bfloat16)
    w_packed = jnp.concatenate(
        [w_bf[:, :n_gdn], _pad_lanes(w_bf[:, n_gdn:n_gdn + HEADS]),
         _pad_lanes(w_bf[:, n_gdn + HEADS:n_gdn + 2 * HEADS]), w_bf[:, n_gdn + 2 * HEADS:]],
        axis=1)
    return _block(x2d, norm_in_w.reshape(1, -1), w_packed, conv_qkv_w,
                  _pad_lanes(a_log.reshape(1, -1)), _pad_lanes(dt_bias.reshape(1, -1)),
                  gdn_norm_w.reshape(1, -1), conv_w, conv_b.reshape(1, -1),
                  w_out.astype(jnp.bfloat16), out_norm_w.reshape(1, -1), batch, seq)


def kernel(x, norm_in_w, w_in, conv_qkv_w, A_log, dt_bias, gdn_norm_w, conv_w, conv_b, w_out,
           final_norm_w):
    batch, seq, d_model = x.shape
    assert norm_in_w.shape[0] == 1 and d_model == D_MODEL and seq % TL == 0
    x2d = x.reshape(batch * seq, d_model)
    out = _layer(x2d, batch, seq, norm_in_w[0], w_in[0], conv_qkv_w[0], A_log[0], dt_bias[0],
                 gdn_norm_w[0], conv_w[0], conv_b[0], w_out[0], final_norm_w)
    return out.reshape(batch, seq, d_model)
```

```python
import jax
import jax.numpy as jnp
from jax import lax
from jax.experimental import pallas as pl
from jax.experimental.pallas import tpu as pltpu

D_MODEL = 1024
HEADS = 8
HEAD_DIM = 128
GDN_WIDTH = HEADS * HEAD_DIM
CONV_WIDTH = 1024
GDN_CONV = 4
SHORT_CONV = 3
CHUNK = 64
EPS = 1e-6
LANES = 128
HALO = 8
ROW_STRIDE = 4

COL_Z = 3 * GDN_WIDTH
COL_BETA = COL_Z + GDN_WIDTH
COL_A = COL_BETA + LANES
COL_CONV = COL_A + LANES
PROJ_PACKED = COL_CONV + 4 * CONV_WIDTH

ROWS = 4
TLR = 64
TL = ROWS * TLR
VMEM_LIMIT = 60 * 1024 * 1024


def _bdot(a, b):
    return jnp.dot(a.astype(jnp.bfloat16), b.astype(jnp.bfloat16),
                   preferred_element_type=jnp.float32)


def _bdot_nt(a, b):
    return lax.dot_general(a.astype(jnp.bfloat16), b.astype(jnp.bfloat16),
                           (((1,), (1,)), ((), ())), preferred_element_type=jnp.float32)


def _bdot_tn(a, b):
    return lax.dot_general(a.astype(jnp.bfloat16), b.astype(jnp.bfloat16),
                           (((0,), (0,)), ((), ())), preferred_element_type=jnp.float32)


def _silu(x):
    return x * jax.nn.sigmoid(x)


def _rmsnorm(x, w):
    ms = jnp.mean(x * x, axis=-1, keepdims=True)
    return x * lax.rsqrt(ms + EPS) * w


def _fill_slabs(slab_ref, x):
    n = x.shape[1] // LANES
    for b in range(ROWS):
        for j in range(n):
            slab_ref[b * n + j, HALO:, :] = x[b * TLR:(b + 1) * TLR, j * LANES:(j + 1) * LANES]


def _conv_row_classes(slab_ref, j, w_ref, lane0, width):
    taps = [w_ref[width - 1 - s:width - s, lane0:lane0 + LANES] for s in range(width)]
    loaded = {}

    def rows_from(start):
        if start not in loaded:
            loaded[start] = slab_ref[j, pl.ds(start, TLR // ROW_STRIDE, stride=ROW_STRIDE), :]
        return loaded[start]

    out = []
    for r in range(ROW_STRIDE):
        acc = rows_from(HALO + r) * taps[0]
        for s in range(1, width):
            acc = acc + rows_from(HALO + r - s) * taps[s]
        out.append(acc)
    slab_ref[j, :HALO, :] = slab_ref[j, TLR:, :]
    return out


def _store_row_class(dst_ref, j, b, r, val):
    dst_ref[j, pl.ds(b * TLR + r, TLR // ROW_STRIDE, stride=ROW_STRIDE), :] = val


def _block_kernel(x_ref, nw_ref, wp_ref, cw_ref, alog_ref, dtb_ref, gnw_ref, ccw_ref, cb_ref,
                  wo_ref, fnw_ref, out_ref,
                  s_ref, pre_slab, ch_slab, cv_s, q_s, k_s, v_s, z_s, beta_s, gc_s, gct_s, u_s,
                  w_s, attn_s, o_s):
    n_chunks = TL // CHUNK
    row_chunks = TLR // CHUNK
    heads = range(HEADS)
    lanes = [slice(h * HEAD_DIM, (h + 1) * HEAD_DIM) for h in heads]
    col = lambda blk, h: blk[:, h:h + 1]
    conv_tiles = CONV_WIDTH // LANES

    @pl.when(pl.program_id(1) == 0)
    def _():
        s_ref[...] = jnp.zeros_like(s_ref)
        pre_slab[:, :, :HALO, :] = jnp.zeros(pre_slab.shape[:2] + (HALO, LANES), jnp.float32)
        ch_slab[:, :HALO, :] = jnp.zeros((ch_slab.shape[0], HALO, LANES), jnp.float32)

    x = x_ref[...].reshape(TL, D_MODEL)
    hn = _rmsnorm(x, nw_ref[...]).astype(jnp.bfloat16)
    proj = lambda lo, hi: jnp.dot(hn, wp_ref[:, lo:hi], preferred_element_type=jnp.float32)
    conv_col = lambda j: proj(COL_CONV + j * CONV_WIDTH, COL_CONV + (j + 1) * CONV_WIDTH)

    ba = proj(COL_BETA, COL_CONV)
    beta_s[...] = jax.nn.sigmoid(ba[:, :LANES])
    a_in = ba[:, LANES:] + dtb_ref[...]
    softplus = jnp.maximum(a_in, 0.0) + jnp.log1p(jnp.exp(-jnp.abs(a_in)))
    gc = -jnp.exp(alog_ref[...]) * softplus
    pos = lax.broadcasted_iota(jnp.int32, gc.shape, 0) % CHUNK
    s = 1
    while s < CHUNK:
        gc = gc + jnp.where(pos >= s, pltpu.roll(gc, s, axis=0), 0.0)
        s *= 2
    gc_s[...] = gc
    gct = gc.T
    for c in range(n_chunks):
        blk = gct[:, c * CHUNK:(c + 1) * CHUNK]
        gct_s[c] = jnp.concatenate([blk, blk], axis=1)

    conv_in = []
    for grp, dst in enumerate((q_s, k_s, v_s)):
        col0 = grp * GDN_WIDTH
        slabs = pre_slab.at[grp]
        _fill_slabs(slabs, proj(col0, col0 + GDN_WIDTH))
        conv_in.append(conv_col(grp))
        scale = HEAD_DIM ** -0.5 if grp == 0 else 1.0
        for b in range(ROWS):
            for h in heads:
                classes = _conv_row_classes(slabs, b * HEADS + h, cw_ref, col0 + h * HEAD_DIM,
                                            GDN_CONV)
                for r, acc in enumerate(classes):
                    act = _silu(acc)
                    if grp < 2:
                        inv = lax.rsqrt(jnp.sum(act * act, axis=-1, keepdims=True) + EPS)
                        act = act * inv * scale
                    _store_row_class(dst, h, b, r, act)
    z_s[...] = proj(COL_Z, COL_BETA)
    gate_c = conv_col(3)

    gate_b, gate_cc, h_c = conv_in
    _fill_slabs(ch_slab, gate_cc * h_c)
    for b in range(ROWS):
        for j in range(conv_tiles):
            classes = _conv_row_classes(ch_slab, b * conv_tiles + j, ccw_ref, j * LANES,
                                        SHORT_CONV)
            for r, acc in enumerate(classes):
                _store_row_class(cv_s, j, b, r, acc)
    conv = jnp.concatenate([cv_s[j] for j in range(conv_tiles)], axis=1)
    y = gate_b * (conv + cb_ref[...]) * _silu(gate_c)
    res = x + _bdot(y, wo_ref[GDN_WIDTH:, :])

    ri = lax.broadcasted_iota(jnp.int32, (CHUNK, 2 * CHUNK), 0)
    ci = lax.broadcasted_iota(jnp.int32, (CHUNK, 2 * CHUNK), 1)
    cj = ci % CHUNK
    causal2 = ri >= cj
    strict2 = ri > cj
    left = ci < CHUNK
    eye_right = jnp.where((ri == cj) & (ci >= CHUNK), 1.0, 0.0)

    items = []
    for c in range(n_chunks):
        rows = slice(c * CHUNK, (c + 1) * CHUNK)
        beta_blk = beta_s[rows, :]
        gc_blk = gc_s[rows, :]
        eg_blk = jnp.exp(gc_blk)
        gct_blk = gct_s[c]
        for h in heads:
            items.append((rows, h, col(beta_blk, h), col(gc_blk, h), col(eg_blk, h),
                          gct_blk[h:h + 1, :]))
    n_items = range(len(items))
    k = [k_s[h, rows, :] for rows, h, *_ in items]
    kb = [k[n] * items[n][2] for n in n_items]
    decay2 = [jnp.where(causal2, jnp.exp(it[3] - it[5]), 0.0) for it in items]
    kq = [_bdot_nt(jnp.concatenate([kb[n], q_s[it[1], it[0], :]], axis=0),
                   jnp.concatenate([k[n], k[n]], axis=0)) for n, it in enumerate(items)]
    for n, it in enumerate(items):
        attn_s[it[1], it[0], :] = kq[n][CHUNK:, :CHUNK] * decay2[n][:, :CHUNK]
    a2 = [jnp.where(strict2, kq[n][:CHUNK] * decay2[n], 0.0) for n in n_items]
    prod = [_bdot(a[:, :CHUNK], a) for a in a2]
    pt = [jnp.where(left, prod[n], eye_right - a2[n]) for n in n_items]
    m = 4
    while m < CHUNK:
        prod = [_bdot(p[:, :CHUNK], p) for p in pt]
        pt = [jnp.where(left, prod[n], pt[n] + prod[n]) for n in n_items]
        m *= 2
    prod = [_bdot(p[:, :CHUNK], p) for p in pt]
    t_mat = [(pt[n] + prod[n])[:, CHUNK:] for n in n_items]
    rhs = [jnp.concatenate([v_s[it[1], it[0], :] * it[2], kb[n] * it[4]], axis=1)
           for n, it in enumerate(items)]
    sol = [_bdot(t_mat[n], rhs[n]) for n in n_items]
    for n, it in enumerate(items):
        u_s[it[0], lanes[it[1]]] = sol[n][:, :HEAD_DIM]
        w_s[it[0], lanes[it[1]]] = sol[n][:, HEAD_DIM:]

    chains = [(b, h) for b in range(ROWS) for h in heads]
    state = [s_ref[b * HEADS + h] for b, h in chains]
    for step in range(row_chunks):
        rows = [slice((b * row_chunks + step) * CHUNK, (b * row_chunks + step + 1) * CHUNK)
                for b in range(ROWS)]
        gc_blk = [gc_s[r, :] for r in rows]
        g_last = [g[CHUNK - 1:CHUNK, :] for g in gc_blk]
        eg_blk = [jnp.exp(g) for g in gc_blk]
        ek_blk = [jnp.exp(g_last[b] - gc_blk[b]) for b in range(ROWS)]
        egl_blk = [jnp.exp(g) for g in g_last]
        ws_qs = [_bdot(jnp.concatenate([w_s[rows[b], lanes[h]],
                                        q_s[h, rows[b], :] * col(eg_blk[b], h)], axis=0),
                       state[n]) for n, (b, h) in enumerate(chains)]
        v_new = [u_s[rows[b], lanes[h]] - ws_qs[n][:CHUNK] for n, (b, h) in enumerate(chains)]
        o = [ws_qs[n][CHUNK:] + _bdot(attn_s[h, rows[b], :], v_new[n])
             for n, (b, h) in enumerate(chains)]
        kv = [_bdot_tn(k_s[h, rows[b], :] * col(ek_blk[b], h), v_new[n])
              for n, (b, h) in enumerate(chains)]
        state = [state[n] * col(egl_blk[b], h) + kv[n] for n, (b, h) in enumerate(chains)]
        for n, (b, h) in enumerate(chains):
            o_s[rows[b], lanes[h]] = (_rmsnorm(o[n], gnw_ref[...])
                                      * _silu(z_s[rows[b], lanes[h]]))
    for n, (b, h) in enumerate(chains):
        s_ref[b * HEADS + h] = state[n]

    res = res + _bdot(o_s[...], wo_ref[:GDN_WIDTH, :])
    out_ref[...] = _rmsnorm(res, fnw_ref[...]).reshape(ROWS, TLR, D_MODEL)


def _block(x, norm_w, w_packed, conv_qkv_w, a_log_pad, dt_bias_pad, gdn_norm_w, conv_w, conv_b,
           w_out, out_norm_w):
    batch, seq, _ = x.shape
    n_chunks = TL // CHUNK
    conv_tiles = CONV_WIDTH // LANES
    tile = lambda g, t: (g, t, 0)
    const = lambda g, t: (0, 0)
    resident = lambda shape: pl.BlockSpec(shape, const, pipeline_mode=pl.Buffered(1))
    tile_f32 = pltpu.VMEM((TL, GDN_WIDTH), jnp.float32)
    head_tile_f32 = pltpu.VMEM((HEADS, TL, HEAD_DIM), jnp.float32)
    return pl.pallas_call(
        _block_kernel,
        grid=(batch // ROWS, seq // TLR),
        in_specs=[
            pl.BlockSpec((ROWS, TLR, D_MODEL), tile),
            resident((1, D_MODEL)),
            resident((D_MODEL, PROJ_PACKED)),
            resident((GDN_CONV, 3 * GDN_WIDTH)),
            resident((1, LANES)),
            resident((1, LANES)),
            resident((1, HEAD_DIM)),
            resident((SHORT_CONV, CONV_WIDTH)),
            resident((1, CONV_WIDTH)),
            resident((GDN_WIDTH + CONV_WIDTH, D_MODEL)),
            resident((1, D_MODEL)),
        ],
        out_specs=pl.BlockSpec((ROWS, TLR, D_MODEL), tile),
        out_shape=jax.ShapeDtypeStruct(x.shape, jnp.float32),
        scratch_shapes=[
            pltpu.VMEM((ROWS * HEADS, HEAD_DIM, HEAD_DIM), jnp.float32),
            pltpu.VMEM((3, ROWS * HEADS, HALO + TLR, LANES), jnp.float32),
            pltpu.VMEM((ROWS * conv_tiles, HALO + TLR, LANES), jnp.float32),
            pltpu.VMEM((conv_tiles, TL, LANES), jnp.float32),
            head_tile_f32, head_tile_f32, head_tile_f32,
            tile_f32,
            pltpu.VMEM((TL, LANES), jnp.float32),
            pltpu.VMEM((TL, LANES), jnp.float32),
            pltpu.VMEM((n_chunks, LANES, 2 * CHUNK), jnp.float32),
            tile_f32, tile_f32,
            pltpu.VMEM((HEADS, TL, CHUNK), jnp.float32),
            tile_f32,
        ],
        compiler_params=pltpu.CompilerParams(
            dimension_semantics=("arbitrary", "arbitrary"), vmem_limit_bytes=VMEM_LIMIT),
        name="hybrid_block",
    )(x, norm_w, w_packed, conv_qkv_w, a_log_pad, dt_bias_pad, gdn_norm_w, conv_w, conv_b,
      w_out, out_norm_w)


def _pad_lanes(v):
    return jnp.pad(v, ((0, 0), (0, LANES - v.shape[-1])))


def _layer(x, norm_in_w, w_in, conv_qkv_w, a_log, dt_bias, gdn_norm_w, conv_w, conv_b, w_out,
           out_norm_w):
    n_gdn = COL_BETA
    w_bf = w_in.astype(jnp.bfloat16)
    w_packed = jnp.concatenate(
        [w_bf[:, :n_gdn], _pad_lanes(w_bf[:, n_gdn:n_gdn + HEADS]),
         _pad_lanes(w_bf[:, n_gdn + HEADS:n_gdn + 2 * HEADS]), w_bf[:, n_gdn + 2 * HEADS:]],
        axis=1)
    return _block(x, norm_in_w.reshape(1, -1), w_packed, conv_qkv_w,
                  _pad_lanes(a_log.reshape(1, -1)), _pad_lanes(dt_bias.reshape(1, -1)),
                  gdn_norm_w.reshape(1, -1), conv_w, conv_b.reshape(1, -1),
                  w_out.astype(jnp.bfloat16), out_norm_w.reshape(1, -1))


def kernel(x, norm_in_w, w_in, conv_qkv_w, A_log, dt_bias, gdn_norm_w, conv_w, conv_b, w_out,
           final_norm_w):
    batch, seq, d_model = x.shape
    assert norm_in_w.shape[0] == 1 and d_model == D_MODEL
    assert batch % ROWS == 0 and seq % TLR == 0 and TLR % CHUNK == 0
    return _layer(x, norm_in_w[0], w_in[0], conv_qkv_w[0], A_log[0], dt_bias[0], gdn_norm_w[0],
                  conv_w[0], conv_b[0], w_out[0], final_norm_w)
```

```python
import jax
import jax.numpy as jnp
from jax import lax
from jax.experimental import pallas as pl
from jax.experimental.pallas import tpu as pltpu

D_MODEL = 1024
HEADS = 8
HEAD_DIM = 128
GDN_WIDTH = HEADS * HEAD_DIM
CONV_WIDTH = 1024
GDN_CONV = 4
SHORT_CONV = 3
CHUNK = 64
EPS = 1e-6
LANES = 128
HALO = 8
ROW_STRIDE = 4

COL_Z = 3 * GDN_WIDTH
COL_BETA = COL_Z + GDN_WIDTH
COL_A = COL_BETA + LANES
COL_CONV = COL_A + LANES
PROJ_PACKED = COL_CONV + 4 * CONV_WIDTH
N_BA = 2 * HEADS
PACK_TN = 256

ROWS = 4
TLR = 64
TL = ROWS * TLR
VMEM_LIMIT = 60 * 1024 * 1024


def _bdot(a, b):
    return jnp.dot(a.astype(jnp.bfloat16), b.astype(jnp.bfloat16),
                   preferred_element_type=jnp.float32)


def _bdot_nt(a, b):
    return lax.dot_general(a.astype(jnp.bfloat16), b.astype(jnp.bfloat16),
                           (((1,), (1,)), ((), ())), preferred_element_type=jnp.float32)


def _bdot_tn(a, b):
    return lax.dot_general(a.astype(jnp.bfloat16), b.astype(jnp.bfloat16),
                           (((0,), (0,)), ((), ())), preferred_element_type=jnp.float32)


def _silu(x):
    return x * jax.nn.sigmoid(x)


def _rmsnorm(x, w):
    ms = jnp.mean(x * x, axis=-1, keepdims=True)
    return x * lax.rsqrt(ms + EPS) * w


def _fill_slabs(slab_ref, x):
    n = x.shape[1] // LANES
    for b in range(ROWS):
        for j in range(n):
            slab_ref[b * n + j, HALO:, :] = x[b * TLR:(b + 1) * TLR, j * LANES:(j + 1) * LANES]


def _conv_row_classes(slab_ref, j, w_ref, lane0, width):
    taps = [w_ref[width - 1 - s:width - s, lane0:lane0 + LANES] for s in range(width)]
    loaded = {}

    def rows_from(start):
        if start not in loaded:
            loaded[start] = slab_ref[j, pl.ds(start, TLR // ROW_STRIDE, stride=ROW_STRIDE), :]
        return loaded[start]

    out = []
    for r in range(ROW_STRIDE):
        acc = rows_from(HALO + r) * taps[0]
        for s in range(1, width):
            acc = acc + rows_from(HALO + r - s) * taps[s]
        out.append(acc)
    slab_ref[j, :HALO, :] = slab_ref[j, TLR:, :]
    return out


def _store_row_class(dst_ref, j, b, r, val):
    dst_ref[j, pl.ds(b * TLR + r, TLR // ROW_STRIDE, stride=ROW_STRIDE), :] = val


def _block_kernel(x_ref, nw_ref, wp_ref, cw_ref, alog_ref, dtb_ref, gnw_ref, ccw_ref, cb_ref,
                  wo_ref, fnw_ref, out_ref,
                  s_ref, pre_slab, ch_slab, cv_s, q_s, k_s, v_s, z_s, beta_s, gc_s, gct_s, u_s,
                  w_s, attn_s, o_s):
    n_chunks = TL // CHUNK
    row_chunks = TLR // CHUNK
    heads = range(HEADS)
    lanes = [slice(h * HEAD_DIM, (h + 1) * HEAD_DIM) for h in heads]
    col = lambda blk, h: blk[:, h:h + 1]
    conv_tiles = CONV_WIDTH // LANES

    @pl.when(pl.program_id(1) == 0)
    def _():
        s_ref[...] = jnp.zeros_like(s_ref)
        pre_slab[:, :, :HALO, :] = jnp.zeros(pre_slab.shape[:2] + (HALO, LANES), jnp.float32)
        ch_slab[:, :HALO, :] = jnp.zeros((ch_slab.shape[0], HALO, LANES), jnp.float32)

    x = x_ref[...].reshape(TL, D_MODEL)
    hn = _rmsnorm(x, nw_ref[...]).astype(jnp.bfloat16)
    proj = lambda lo, hi: jnp.dot(hn, wp_ref[:, lo:hi], preferred_element_type=jnp.float32)
    conv_col = lambda j: proj(COL_CONV + j * CONV_WIDTH, COL_CONV + (j + 1) * CONV_WIDTH)

    ba = proj(COL_BETA, COL_CONV)
    beta_s[...] = jax.nn.sigmoid(ba[:, :LANES])
    a_in = ba[:, LANES:] + dtb_ref[...]
    softplus = jnp.maximum(a_in, 0.0) + jnp.log1p(jnp.exp(-jnp.abs(a_in)))
    gc = -jnp.exp(alog_ref[...]) * softplus
    pos = lax.broadcasted_iota(jnp.int32, gc.shape, 0) % CHUNK
    s = 1
    while s < CHUNK:
        gc = gc + jnp.where(pos >= s, pltpu.roll(gc, s, axis=0), 0.0)
        s *= 2
    gc_s[...] = gc
    gct = gc.T
    for c in range(n_chunks):
        blk = gct[:, c * CHUNK:(c + 1) * CHUNK]
        gct_s[c] = jnp.concatenate([blk, blk], axis=1)

    conv_in = []
    for grp, dst in enumerate((q_s, k_s, v_s)):
        col0 = grp * GDN_WIDTH
        slabs = pre_slab.at[grp]
        _fill_slabs(slabs, proj(col0, col0 + GDN_WIDTH))
        conv_in.append(conv_col(grp))
        scale = HEAD_DIM ** -0.5 if grp == 0 else 1.0
        for b in range(ROWS):
            for h in heads:
                classes = _conv_row_classes(slabs, b * HEADS + h, cw_ref, col0 + h * HEAD_DIM,
                                            GDN_CONV)
                for r, acc in enumerate(classes):
                    act = _silu(acc)
                    if grp < 2:
                        inv = lax.rsqrt(jnp.sum(act * act, axis=-1, keepdims=True) + EPS)
                        act = act * inv * scale
                    _store_row_class(dst, h, b, r, act)
    z_s[...] = proj(COL_Z, COL_BETA)
    gate_c = conv_col(3)

    gate_b, gate_cc, h_c = conv_in
    _fill_slabs(ch_slab, gate_cc * h_c)
    for b in range(ROWS):
        for j in range(conv_tiles):
            classes = _conv_row_classes(ch_slab, b * conv_tiles + j, ccw_ref, j * LANES,
                                        SHORT_CONV)
            for r, acc in enumerate(classes):
                _store_row_class(cv_s, j, b, r, acc)
    conv = jnp.concatenate([cv_s[j] for j in range(conv_tiles)], axis=1)
    y = gate_b * (conv + cb_ref[...]) * _silu(gate_c)
    res = x + _bdot(y, wo_ref[GDN_WIDTH:, :])

    ri = lax.broadcasted_iota(jnp.int32, (CHUNK, 2 * CHUNK), 0)
    ci = lax.broadcasted_iota(jnp.int32, (CHUNK, 2 * CHUNK), 1)
    cj = ci % CHUNK
    causal2 = ri >= cj
    strict2 = ri > cj
    left = ci < CHUNK
    eye_right = jnp.where((ri == cj) & (ci >= CHUNK), 1.0, 0.0)

    items = []
    for c in range(n_chunks):
        rows = slice(c * CHUNK, (c + 1) * CHUNK)
        beta_blk = beta_s[rows, :]
        gc_blk = gc_s[rows, :]
        eg_blk = jnp.exp(gc_blk)
        gct_blk = gct_s[c]
        for h in heads:
            items.append((rows, h, col(beta_blk, h), col(gc_blk, h), col(eg_blk, h),
                          gct_blk[h:h + 1, :]))
    n_items = range(len(items))
    k = [k_s[h, rows, :] for rows, h, *_ in items]
    kb = [k[n] * items[n][2] for n in n_items]
    decay2 = [jnp.where(causal2, jnp.exp(it[3] - it[5]), 0.0) for it in items]
    kq = [_bdot_nt(jnp.concatenate([kb[n], q_s[it[1], it[0], :]], axis=0),
                   jnp.concatenate([k[n], k[n]], axis=0)) for n, it in enumerate(items)]
    for n, it in enumerate(items):
        attn_s[it[1], it[0], :] = kq[n][CHUNK:, :CHUNK] * decay2[n][:, :CHUNK]
    a2 = [jnp.where(strict2, kq[n][:CHUNK] * decay2[n], 0.0) for n in n_items]
    prod = [_bdot(a[:, :CHUNK], a) for a in a2]
    pt = [jnp.where(left, prod[n], eye_right - a2[n]) for n in n_items]
    m = 4
    while m < CHUNK:
        prod = [_bdot(p[:, :CHUNK], p) for p in pt]
        pt = [jnp.where(left, prod[n], pt[n] + prod[n]) for n in n_items]
        m *= 2
    prod = [_bdot(p[:, :CHUNK], p) for p in pt]
    t_mat = [(pt[n] + prod[n])[:, CHUNK:] for n in n_items]
    rhs = [jnp.concatenate([v_s[it[1], it[0], :] * it[2], kb[n] * it[4]], axis=1)
           for n, it in enumerate(items)]
    sol = [_bdot(t_mat[n], rhs[n]) for n in n_items]
    for n, it in enumerate(items):
        u_s[it[0], lanes[it[1]]] = sol[n][:, :HEAD_DIM]
        w_s[it[0], lanes[it[1]]] = sol[n][:, HEAD_DIM:]

    chains = [(b, h) for b in range(ROWS) for h in heads]
    state = [s_ref[b * HEADS + h] for b, h in chains]
    for step in range(row_chunks):
        rows = [slice((b * row_chunks + step) * CHUNK, (b * row_chunks + step + 1) * CHUNK)
                for b in range(ROWS)]
        gc_blk = [gc_s[r, :] for r in rows]
        g_last = [g[CHUNK - 1:CHUNK, :] for g in gc_blk]
        eg_blk = [jnp.exp(g) for g in gc_blk]
        ek_blk = [jnp.exp(g_last[b] - gc_blk[b]) for b in range(ROWS)]
        egl_blk = [jnp.exp(g) for g in g_last]
        ws_qs = [_bdot(jnp.concatenate([w_s[rows[b], lanes[h]],
                                        q_s[h, rows[b], :] * col(eg_blk[b], h)], axis=0),
                       state[n]) for n, (b, h) in enumerate(chains)]
        v_new = [u_s[rows[b], lanes[h]] - ws_qs[n][:CHUNK] for n, (b, h) in enumerate(chains)]
        o = [ws_qs[n][CHUNK:] + _bdot(attn_s[h, rows[b], :], v_new[n])
             for n, (b, h) in enumerate(chains)]
        kv = [_bdot_tn(k_s[h, rows[b], :] * col(ek_blk[b], h), v_new[n])
              for n, (b, h) in enumerate(chains)]
        state = [state[n] * col(egl_blk[b], h) + kv[n] for n, (b, h) in enumerate(chains)]
        for n, (b, h) in enumerate(chains):
            o_s[rows[b], lanes[h]] = (_rmsnorm(o[n], gnw_ref[...])
                                      * _silu(z_s[rows[b], lanes[h]]))
    for n, (b, h) in enumerate(chains):
        s_ref[b * HEADS + h] = state[n]

    res = res + _bdot(o_s[...], wo_ref[:GDN_WIDTH, :])
    out_ref[...] = _rmsnorm(res, fnw_ref[...]).reshape(ROWS, TLR, D_MODEL)


def _block(x, norm_w, w_packed, conv_qkv_w, a_log_pad, dt_bias_pad, gdn_norm_w, conv_w, conv_b,
           w_out, out_norm_w):
    batch, seq, _ = x.shape
    n_chunks = TL // CHUNK
    conv_tiles = CONV_WIDTH // LANES
    tile = lambda g, t: (g, t, 0)
    const = lambda g, t: (0, 0)
    resident = lambda shape: pl.BlockSpec(shape, const, pipeline_mode=pl.Buffered(1))
    tile_f32 = pltpu.VMEM((TL, GDN_WIDTH), jnp.float32)
    head_tile_f32 = pltpu.VMEM((HEADS, TL, HEAD_DIM), jnp.float32)
    return pl.pallas_call(
        _block_kernel,
        grid=(batch // ROWS, seq // TLR),
        in_specs=[
            pl.BlockSpec((ROWS, TLR, D_MODEL), tile),
            resident((1, D_MODEL)),
            resident((D_MODEL, PROJ_PACKED)),
            resident((GDN_CONV, 3 * GDN_WIDTH)),
            resident((1, LANES)),
            resident((1, LANES)),
            resident((1, HEAD_DIM)),
            resident((SHORT_CONV, CONV_WIDTH)),
            resident((1, CONV_WIDTH)),
            resident((GDN_WIDTH + CONV_WIDTH, D_MODEL)),
            resident((1, D_MODEL)),
        ],
        out_specs=pl.BlockSpec((ROWS, TLR, D_MODEL), tile),
        out_shape=jax.ShapeDtypeStruct(x.shape, jnp.float32),
        scratch_shapes=[
            pltpu.VMEM((ROWS * HEADS, HEAD_DIM, HEAD_DIM), jnp.float32),
            pltpu.VMEM((3, ROWS * HEADS, HALO + TLR, LANES), jnp.float32),
            pltpu.VMEM((ROWS * conv_tiles, HALO + TLR, LANES), jnp.float32),
            pltpu.VMEM((conv_tiles, TL, LANES), jnp.float32),
            head_tile_f32, head_tile_f32, head_tile_f32,
            tile_f32,
            pltpu.VMEM((TL, LANES), jnp.float32),
            pltpu.VMEM((TL, LANES), jnp.float32),
            pltpu.VMEM((n_chunks, LANES, 2 * CHUNK), jnp.float32),
            tile_f32, tile_f32,
            pltpu.VMEM((HEADS, TL, CHUNK), jnp.float32),
            tile_f32,
        ],
        compiler_params=pltpu.CompilerParams(
            dimension_semantics=("arbitrary", "arbitrary"), vmem_limit_bytes=VMEM_LIMIT),
        name="hybrid_block",
    )(x, norm_w, w_packed, conv_qkv_w, a_log_pad, dt_bias_pad, gdn_norm_w, conv_w, conv_b,
      w_out, out_norm_w)


def _pad_lanes(v):
    return jnp.pad(v, ((0, 0), (0, LANES - v.shape[-1])))


def _pack_kernel(a_ref, m_ref, o_ref):
    j = pl.program_id(0)
    ba_block = COL_BETA // PACK_TN

    @pl.when(j < ba_block)
    def _():
        o_ref[...] = a_ref[...].T.astype(jnp.bfloat16)

    @pl.when(j == ba_block)
    def _():
        pad = jnp.zeros((LANES - HEADS, D_MODEL), jnp.float32)
        rows = jnp.concatenate([a_ref[:HEADS, :], pad, a_ref[HEADS:N_BA, :], pad], axis=0)
        o_ref[...] = rows.T.astype(jnp.bfloat16)

    @pl.when(j > ba_block)
    def _():
        rows = jnp.concatenate([a_ref[N_BA:, :], m_ref[...]], axis=0)
        o_ref[...] = rows.T.astype(jnp.bfloat16)


def _pack_weights(wt):
    ba_block = COL_BETA // PACK_TN
    a_index = lambda j: jnp.where(j <= ba_block, j, j - 1)
    return pl.pallas_call(
        _pack_kernel,
        grid=(PROJ_PACKED // PACK_TN,),
        in_specs=[
            pl.BlockSpec((PACK_TN, D_MODEL), lambda j: (a_index(j), 0)),
            pl.BlockSpec((N_BA, D_MODEL), lambda j: ((a_index(j) + 1) * (PACK_TN // N_BA), 0)),
        ],
        out_specs=pl.BlockSpec((D_MODEL, PACK_TN), lambda j: (0, j)),
        out_shape=jax.ShapeDtypeStruct((D_MODEL, PROJ_PACKED), jnp.bfloat16),
        compiler_params=pltpu.CompilerParams(dimension_semantics=("arbitrary",)),
        name="pack_weights",
    )(wt, wt)


def _layer(x, norm_in_w, w_in, conv_qkv_w, a_log, dt_bias, gdn_norm_w, conv_w, conv_b, w_out,
           out_norm_w):
    assert w_in.shape[1] + 2 * LANES - N_BA == PROJ_PACKED
    w_packed = _pack_weights(w_in.T)
    return _block(x, norm_in_w.reshape(1, -1), w_packed, conv_qkv_w,
                  _pad_lanes(a_log.reshape(1, -1)), _pad_lanes(dt_bias.reshape(1, -1)),
                  gdn_norm_w.reshape(1, -1), conv_w, conv_b.reshape(1, -1),
                  w_out.astype(jnp.bfloat16), out_norm_w.reshape(1, -1))


def kernel(x, norm_in_w, w_in, conv_qkv_w, A_log, dt_bias, gdn_norm_w, conv_w, conv_b, w_out,
           final_norm_w):
    batch, seq, d_model = x.shape
    assert norm_in_w.shape[0] == 1 and d_model == D_MODEL
    assert batch % ROWS == 0 and seq % TLR == 0 and TLR % CHUNK == 0
    return _layer(x, norm_in_w[0], w_in[0], conv_qkv_w[0], A_log[0], dt_bias[0], gdn_norm_w[0],
                  conv_w[0], conv_b[0], w_out[0], final_norm_w)
```

```python
import jax
import jax.numpy as jnp
from jax import lax
from jax.experimental import pallas as pl
from jax.experimental.pallas import tpu as pltpu

D_MODEL = 1024
HEADS = 8
HEAD_DIM = 128
GDN_WIDTH = HEADS * HEAD_DIM
CONV_WIDTH = 1024
GDN_CONV = 4
SHORT_CONV = 3
CHUNK = 64
EPS = 1e-6
LANES = 128
HALO = 8
ROW_STRIDE = 4

COL_Z = 3 * GDN_WIDTH
COL_CONV = COL_Z + GDN_WIDTH
COL_BETA = COL_CONV + 4 * CONV_WIDTH
COL_A = COL_BETA + LANES
PROJ_PACKED = COL_A + LANES
N_BA = 2 * HEADS
PACK_TN = 1024

ROWS = 4
TLR = 64
TL = ROWS * TLR
VMEM_LIMIT = 60 * 1024 * 1024


def _bdot(a, b):
    return jnp.dot(a.astype(jnp.bfloat16), b.astype(jnp.bfloat16),
                   preferred_element_type=jnp.float32)


def _bdot_nt(a, b):
    return lax.dot_general(a.astype(jnp.bfloat16), b.astype(jnp.bfloat16),
                           (((1,), (1,)), ((), ())), preferred_element_type=jnp.float32)


def _bdot_tn(a, b):
    return lax.dot_general(a.astype(jnp.bfloat16), b.astype(jnp.bfloat16),
                           (((0,), (0,)), ((), ())), preferred_element_type=jnp.float32)


def _silu(x):
    return x * jax.nn.sigmoid(x)


def _rmsnorm(x, w):
    ms = jnp.mean(x * x, axis=-1, keepdims=True)
    return x * lax.rsqrt(ms + EPS) * w


def _fill_slabs(slab_ref, x):
    n = x.shape[1] // LANES
    for b in range(ROWS):
        for j in range(n):
            slab_ref[b * n + j, HALO:, :] = x[b * TLR:(b + 1) * TLR, j * LANES:(j + 1) * LANES]


def _conv_row_classes(slab_ref, j, w_ref, lane0, width):
    taps = [w_ref[width - 1 - s:width - s, lane0:lane0 + LANES] for s in range(width)]
    loaded = {}

    def rows_from(start):
        if start not in loaded:
            loaded[start] = slab_ref[j, pl.ds(start, TLR // ROW_STRIDE, stride=ROW_STRIDE), :]
        return loaded[start]

    out = []
    for r in range(ROW_STRIDE):
        acc = rows_from(HALO + r) * taps[0]
        for s in range(1, width):
            acc = acc + rows_from(HALO + r - s) * taps[s]
        out.append(acc)
    slab_ref[j, :HALO, :] = slab_ref[j, TLR:, :]
    return out


def _store_row_class(dst_ref, j, b, r, val):
    dst_ref[j, pl.ds(b * TLR + r, TLR // ROW_STRIDE, stride=ROW_STRIDE), :] = val


def _block_kernel(x_ref, nw_ref, wp_ref, cw_ref, alog_ref, dtb_ref, gnw_ref, ccw_ref, cb_ref,
                  wo_ref, fnw_ref, out_ref,
                  s_ref, pre_slab, ch_slab, cv_s, q_s, k_s, v_s, z_s, beta_s, gc_s, gct_s, u_s,
                  w_s, attn_s, o_s):
    n_chunks = TL // CHUNK
    row_chunks = TLR // CHUNK
    heads = range(HEADS)
    lanes = [slice(h * HEAD_DIM, (h + 1) * HEAD_DIM) for h in heads]
    col = lambda blk, h: blk[:, h:h + 1]
    conv_tiles = CONV_WIDTH // LANES

    @pl.when(pl.program_id(1) == 0)
    def _():
        s_ref[...] = jnp.zeros_like(s_ref)
        pre_slab[:, :, :HALO, :] = jnp.zeros(pre_slab.shape[:2] + (HALO, LANES), jnp.float32)
        ch_slab[:, :HALO, :] = jnp.zeros((ch_slab.shape[0], HALO, LANES), jnp.float32)

    x = x_ref[...].reshape(TL, D_MODEL)
    hn = _rmsnorm(x, nw_ref[...]).astype(jnp.bfloat16)
    proj = lambda lo, hi: jnp.dot(hn, wp_ref[:, lo:hi], preferred_element_type=jnp.float32)
    conv_col = lambda j: proj(COL_CONV + j * CONV_WIDTH, COL_CONV + (j + 1) * CONV_WIDTH)

    ba = proj(COL_BETA, PROJ_PACKED)
    beta_s[...] = jax.nn.sigmoid(ba[:, :LANES])
    a_in = ba[:, LANES:] + dtb_ref[...]
    softplus = jnp.maximum(a_in, 0.0) + jnp.log1p(jnp.exp(-jnp.abs(a_in)))
    gc = -jnp.exp(alog_ref[...]) * softplus
    pos = lax.broadcasted_iota(jnp.int32, gc.shape, 0) % CHUNK
    s = 1
    while s < CHUNK:
        gc = gc + jnp.where(pos >= s, pltpu.roll(gc, s, axis=0), 0.0)
        s *= 2
    gc_s[...] = gc
    gct = gc.T
    for c in range(n_chunks):
        blk = gct[:, c * CHUNK:(c + 1) * CHUNK]
        gct_s[c] = jnp.concatenate([blk, blk], axis=1)

    conv_in = []
    for grp, dst in enumerate((q_s, k_s, v_s)):
        col0 = grp * GDN_WIDTH
        slabs = pre_slab.at[grp]
        _fill_slabs(slabs, proj(col0, col0 + GDN_WIDTH))
        conv_in.append(conv_col(grp))
        scale = HEAD_DIM ** -0.5 if grp == 0 else 1.0
        for b in range(ROWS):
            for h in heads:
                classes = _conv_row_classes(slabs, b * HEADS + h, cw_ref, col0 + h * HEAD_DIM,
                                            GDN_CONV)
                for r, acc in enumerate(classes):
                    act = _silu(acc)
                    if grp < 2:
                        inv = lax.rsqrt(jnp.sum(act * act, axis=-1, keepdims=True) + EPS)
                        act = act * inv * scale
                    _store_row_class(dst, h, b, r, act)
    z_s[...] = proj(COL_Z, COL_CONV)
    gate_c = conv_col(3)

    gate_b, gate_cc, h_c = conv_in
    _fill_slabs(ch_slab, gate_cc * h_c)
    for b in range(ROWS):
        for j in range(conv_tiles):
            classes = _conv_row_classes(ch_slab, b * conv_tiles + j, ccw_ref, j * LANES,
                                        SHORT_CONV)
            for r, acc in enumerate(classes):
                _store_row_class(cv_s, j, b, r, acc)
    conv = jnp.concatenate([cv_s[j] for j in range(conv_tiles)], axis=1)
    y = gate_b * (conv + cb_ref[...]) * _silu(gate_c)
    res = x + _bdot(y, wo_ref[GDN_WIDTH:, :])

    ri = lax.broadcasted_iota(jnp.int32, (CHUNK, 2 * CHUNK), 0)
    ci = lax.broadcasted_iota(jnp.int32, (CHUNK, 2 * CHUNK), 1)
    cj = ci % CHUNK
    causal2 = ri >= cj
    strict2 = ri > cj
    left = ci < CHUNK
    eye_right = jnp.where((ri == cj) & (ci >= CHUNK), 1.0, 0.0)

    items = []
    for c in range(n_chunks):
        rows = slice(c * CHUNK, (c + 1) * CHUNK)
        beta_blk = beta_s[rows, :]
        gc_blk = gc_s[rows, :]
        eg_blk = jnp.exp(gc_blk)
        gct_blk = gct_s[c]
        for h in heads:
            items.append((rows, h, col(beta_blk, h), col(gc_blk, h), col(eg_blk, h),
                          gct_blk[h:h + 1, :]))
    n_items = range(len(items))
    k = [k_s[h, rows, :] for rows, h, *_ in items]
    kb = [k[n] * items[n][2] for n in n_items]
    decay2 = [jnp.where(causal2, jnp.exp(it[3] - it[5]), 0.0) for it in items]
    kq = [_bdot_nt(jnp.concatenate([kb[n], q_s[it[1], it[0], :]], axis=0),
                   jnp.concatenate([k[n], k[n]], axis=0)) for n, it in enumerate(items)]
    for n, it in enumerate(items):
        attn_s[it[1], it[0], :] = kq[n][CHUNK:, :CHUNK] * decay2[n][:, :CHUNK]
    a2 = [jnp.where(strict2, kq[n][:CHUNK] * decay2[n], 0.0) for n in n_items]
    prod = [_bdot(a[:, :CHUNK], a) for a in a2]
    pt = [jnp.where(left, prod[n], eye_right - a2[n]) for n in n_items]
    m = 4
    while m < CHUNK:
        prod = [_bdot(p[:, :CHUNK], p) for p in pt]
        pt = [jnp.where(left, prod[n], pt[n] + prod[n]) for n in n_items]
        m *= 2
    prod = [_bdot(p[:, :CHUNK], p) for p in pt]
    t_mat = [(pt[n] + prod[n])[:, CHUNK:] for n in n_items]
    rhs = [jnp.concatenate([v_s[it[1], it[0], :] * it[2], kb[n] * it[4]], axis=1)
           for n, it in enumerate(items)]
    sol = [_bdot(t_mat[n], rhs[n]) for n in n_items]
    for n, it in enumerate(items):
        u_s[it[0], lanes[it[1]]] = sol[n][:, :HEAD_DIM]
        w_s[it[0], lanes[it[1]]] = sol[n][:, HEAD_DIM:]

    chains = [(b, h) for b in range(ROWS) for h in heads]
    state = [s_ref[b * HEADS + h] for b, h in chains]
    for step in range(row_chunks):
        rows = [slice((b * row_chunks + step) * CHUNK, (b * row_chunks + step + 1) * CHUNK)
                for b in range(ROWS)]
        gc_blk = [gc_s[r, :] for r in rows]
        g_last = [g[CHUNK - 1:CHUNK, :] for g in gc_blk]
        eg_blk = [jnp.exp(g) for g in gc_blk]
        ek_blk = [jnp.exp(g_last[b] - gc_blk[b]) for b in range(ROWS)]
        egl_blk = [jnp.exp(g) for g in g_last]
        ws_qs = [_bdot(jnp.concatenate([w_s[rows[b], lanes[h]],
                                        q_s[h, rows[b], :] * col(eg_blk[b], h)], axis=0),
                       state[n]) for n, (b, h) in enumerate(chains)]
        v_new = [u_s[rows[b], lanes[h]] - ws_qs[n][:CHUNK] for n, (b, h) in enumerate(chains)]
        o = [ws_qs[n][CHUNK:] + _bdot(attn_s[h, rows[b], :], v_new[n])
             for n, (b, h) in enumerate(chains)]
        kv = [_bdot_tn(k_s[h, rows[b], :] * col(ek_blk[b], h), v_new[n])
              for n, (b, h) in enumerate(chains)]
        state = [state[n] * col(egl_blk[b], h) + kv[n] for n, (b, h) in enumerate(chains)]
        for n, (b, h) in enumerate(chains):
            o_s[rows[b], lanes[h]] = (_rmsnorm(o[n], gnw_ref[...])
                                      * _silu(z_s[rows[b], lanes[h]]))
    for n, (b, h) in enumerate(chains):
        s_ref[b * HEADS + h] = state[n]

    res = res + _bdot(o_s[...], wo_ref[:GDN_WIDTH, :])
    out_ref[...] = _rmsnorm(res, fnw_ref[...]).reshape(ROWS, TLR, D_MODEL)


def _block(x, norm_w, w_packed, conv_qkv_w, a_log_pad, dt_bias_pad, gdn_norm_w, conv_w, conv_b,
           w_out, out_norm_w):
    batch, seq, _ = x.shape
    n_chunks = TL // CHUNK
    conv_tiles = CONV_WIDTH // LANES
    tile = lambda g, t: (g, t, 0)
    const = lambda g, t: (0, 0)
    resident = lambda shape: pl.BlockSpec(shape, const, pipeline_mode=pl.Buffered(1))
    tile_f32 = pltpu.VMEM((TL, GDN_WIDTH), jnp.float32)
    head_tile_f32 = pltpu.VMEM((HEADS, TL, HEAD_DIM), jnp.float32)
    return pl.pallas_call(
        _block_kernel,
        grid=(batch // ROWS, seq // TLR),
        in_specs=[
            pl.BlockSpec((ROWS, TLR, D_MODEL), tile),
            resident((1, D_MODEL)),
            resident((D_MODEL, PROJ_PACKED)),
            resident((GDN_CONV, 3 * GDN_WIDTH)),
            resident((1, LANES)),
            resident((1, LANES)),
            resident((1, HEAD_DIM)),
            resident((SHORT_CONV, CONV_WIDTH)),
            resident((1, CONV_WIDTH)),
            resident((GDN_WIDTH + CONV_WIDTH, D_MODEL)),
            resident((1, D_MODEL)),
        ],
        out_specs=pl.BlockSpec((ROWS, TLR, D_MODEL), tile),
        out_shape=jax.ShapeDtypeStruct(x.shape, jnp.float32),
        scratch_shapes=[
            pltpu.VMEM((ROWS * HEADS, HEAD_DIM, HEAD_DIM), jnp.float32),
            pltpu.VMEM((3, ROWS * HEADS, HALO + TLR, LANES), jnp.float32),
            pltpu.VMEM((ROWS * conv_tiles, HALO + TLR, LANES), jnp.float32),
            pltpu.VMEM((conv_tiles, TL, LANES), jnp.float32),
            head_tile_f32, head_tile_f32, head_tile_f32,
            tile_f32,
            pltpu.VMEM((TL, LANES), jnp.float32),
            pltpu.VMEM((TL, LANES), jnp.float32),
            pltpu.VMEM((n_chunks, LANES, 2 * CHUNK), jnp.float32),
            tile_f32, tile_f32,
            pltpu.VMEM((HEADS, TL, CHUNK), jnp.float32),
            tile_f32,
        ],
        compiler_params=pltpu.CompilerParams(
            dimension_semantics=("arbitrary", "arbitrary"), vmem_limit_bytes=VMEM_LIMIT),
        name="hybrid_block",
    )(x, norm_w, w_packed, conv_qkv_w, a_log_pad, dt_bias_pad, gdn_norm_w, conv_w, conv_b,
      w_out, out_norm_w)


def _pad_lanes(v):
    return jnp.pad(v, ((0, 0), (0, LANES - v.shape[-1])))


def _pack_kernel(a_ref, m_ref, o_ref):
    @pl.when(pl.program_id(0) < COL_CONV // PACK_TN)
    def _():
        o_ref[...] = a_ref[...].T.astype(jnp.bfloat16)

    @pl.when(pl.program_id(0) >= COL_CONV // PACK_TN)
    def _():
        rows = jnp.concatenate([a_ref[N_BA:, :], m_ref[...]], axis=0)
        o_ref[...] = rows.T.astype(jnp.bfloat16)


def _pack_ba_kernel(ba_ref, packed_hbm_ref, o_ref):
    del packed_hbm_ref
    pad = jnp.zeros((LANES - HEADS, D_MODEL), jnp.float32)
    rows = jnp.concatenate([ba_ref[:HEADS, :], pad, ba_ref[HEADS:, :], pad], axis=0)
    o_ref[...] = rows.T.astype(jnp.bfloat16)


def _pack_weights(wt):
    mini_per_block = PACK_TN // N_BA
    packed = pl.pallas_call(
        _pack_kernel,
        grid=(COL_BETA // PACK_TN,),
        in_specs=[
            pl.BlockSpec((PACK_TN, D_MODEL), lambda j: (j, 0)),
            pl.BlockSpec((N_BA, D_MODEL), lambda j: ((j + 1) * mini_per_block, 0)),
        ],
        out_specs=pl.BlockSpec((D_MODEL, PACK_TN), lambda j: (0, j)),
        out_shape=jax.ShapeDtypeStruct((D_MODEL, PROJ_PACKED), jnp.bfloat16),
        compiler_params=pltpu.CompilerParams(
            dimension_semantics=("arbitrary",), vmem_limit_bytes=VMEM_LIMIT),
        name="pack_weights",
    )(wt, wt)
    return pl.pallas_call(
        _pack_ba_kernel,
        grid=(1,),
        in_specs=[
            pl.BlockSpec((N_BA, D_MODEL), lambda i: (COL_CONV // N_BA, 0)),
            pl.BlockSpec(memory_space=pl.ANY),
        ],
        out_specs=pl.BlockSpec((D_MODEL, 2 * LANES), lambda i: (0, COL_BETA // (2 * LANES))),
        out_shape=jax.ShapeDtypeStruct((D_MODEL, PROJ_PACKED), jnp.bfloat16),
        input_output_aliases={1: 0},
        name="pack_beta_decay",
    )(wt, packed)


def _layer(x, norm_in_w, w_in, conv_qkv_w, a_log, dt_bias, gdn_norm_w, conv_w, conv_b, w_out,
           out_norm_w):
    assert w_in.shape[1] + 2 * LANES - N_BA == PROJ_PACKED
    w_packed = _pack_weights(w_in.T)
    return _block(x, norm_in_w.reshape(1, -1), w_packed, conv_qkv_w,
                  _pad_lanes(a_log.reshape(1, -1)), _pad_lanes(dt_bias.reshape(1, -1)),
                  gdn_norm_w.reshape(1, -1), conv_w, conv_b.reshape(1, -1),
                  w_out.astype(jnp.bfloat16), out_norm_w.reshape(1, -1))


def kernel(x, norm_in_w, w_in, conv_qkv_w, A_log, dt_bias, gdn_norm_w, conv_w, conv_b, w_out,
           final_norm_w):
    batch, seq, d_model = x.shape
    assert norm_in_w.shape[0] == 1 and d_model == D_MODEL
    assert batch % ROWS == 0 and seq % TLR == 0 and TLR % CHUNK == 0
    return _layer(x, norm_in_w[0], w_in[0], conv_qkv_w[0], A_log[0], dt_bias[0], gdn_norm_w[0],
                  conv_w[0], conv_b[0], w_out[0], final_norm_w)
```

```python
import jax
import jax.numpy as jnp
from jax import lax
from jax.experimental import pallas as pl
from jax.experimental.pallas import tpu as pltpu

D_MODEL = 1024
HEADS = 8
HEAD_DIM = 128
GDN_WIDTH = HEADS * HEAD_DIM
CONV_WIDTH = 1024
GDN_CONV = 4
SHORT_CONV = 3
CHUNK = 64
EPS = 1e-6
LANES = 128
HALO = 8
ROW_STRIDE = 4

COL_Z = 3 * GDN_WIDTH
COL_CONV = COL_Z + GDN_WIDTH
COL_BETA = COL_CONV + 4 * CONV_WIDTH
COL_A = COL_BETA + LANES
PROJ_PACKED = COL_A + LANES
N_BA = 2 * HEADS
PACK_TN = 1024

ROWS = 4
TLR = 64
TL = ROWS * TLR
VMEM_LIMIT = 60 * 1024 * 1024


def _bdot(a, b):
    return jnp.dot(a.astype(jnp.bfloat16), b.astype(jnp.bfloat16),
                   preferred_element_type=jnp.float32)


def _bdot_nt(a, b):
    return lax.dot_general(a.astype(jnp.bfloat16), b.astype(jnp.bfloat16),
                           (((1,), (1,)), ((), ())), preferred_element_type=jnp.float32)


def _bdot_tn(a, b):
    return lax.dot_general(a.astype(jnp.bfloat16), b.astype(jnp.bfloat16),
                           (((0,), (0,)), ((), ())), preferred_element_type=jnp.float32)


def _silu(x):
    return x * jax.nn.sigmoid(x)


def _rmsnorm(x, w):
    ms = jnp.mean(x * x, axis=-1, keepdims=True)
    return x * lax.rsqrt(ms + EPS) * w


def _fill_slabs(slab_ref, x):
    n = x.shape[1] // LANES
    for b in range(ROWS):
        for j in range(n):
            slab_ref[b * n + j, HALO:, :] = x[b * TLR:(b + 1) * TLR, j * LANES:(j + 1) * LANES]


def _conv_row_classes(slab_ref, j, w_ref, lane0, width):
    taps = [w_ref[width - 1 - s:width - s, lane0:lane0 + LANES] for s in range(width)]
    loaded = {}

    def rows_from(start):
        if start not in loaded:
            loaded[start] = slab_ref[j, pl.ds(start, TLR // ROW_STRIDE, stride=ROW_STRIDE), :]
        return loaded[start]

    out = []
    for r in range(ROW_STRIDE):
        acc = rows_from(HALO + r) * taps[0]
        for s in range(1, width):
            acc = acc + rows_from(HALO + r - s) * taps[s]
        out.append(acc)
    slab_ref[j, :HALO, :] = slab_ref[j, TLR:, :]
    return out


def _store_row_class(dst_ref, j, b, r, val):
    dst_ref[j, pl.ds(b * TLR + r, TLR // ROW_STRIDE, stride=ROW_STRIDE), :] = val


def _block_kernel(x_ref, nw_ref, wp_ref, cw_ref, alog_ref, dtb_ref, gnw_ref, ccw_ref, cb_ref,
                  wo_ref, fnw_ref, out_ref,
                  s_ref, pre_slab, ch_slab, cv_s, q_s, k_s, v_s, z_s, beta_s, gc_s, gct_s, u_s,
                  w_s, attn_s, o_s):
    n_chunks = TL // CHUNK
    row_chunks = TLR // CHUNK
    heads = range(HEADS)
    lanes = [slice(h * HEAD_DIM, (h + 1) * HEAD_DIM) for h in heads]
    col = lambda blk, h: blk[:, h:h + 1]
    conv_tiles = CONV_WIDTH // LANES

    @pl.when(pl.program_id(1) == 0)
    def _():
        s_ref[...] = jnp.zeros_like(s_ref)
        pre_slab[:, :, :HALO, :] = jnp.zeros(pre_slab.shape[:2] + (HALO, LANES), jnp.float32)
        ch_slab[:, :HALO, :] = jnp.zeros((ch_slab.shape[0], HALO, LANES), jnp.float32)

    x = x_ref[...].reshape(TL, D_MODEL)
    hn = _rmsnorm(x, nw_ref[...]).astype(jnp.bfloat16)
    proj = lambda lo, hi: jnp.dot(hn, wp_ref[:, lo:hi], preferred_element_type=jnp.float32)
    conv_col = lambda j: proj(COL_CONV + j * CONV_WIDTH, COL_CONV + (j + 1) * CONV_WIDTH)

    ba = proj(COL_BETA, PROJ_PACKED)
    beta_s[...] = jax.nn.sigmoid(ba[:, :LANES])
    a_in = ba[:, LANES:] + dtb_ref[...]
    softplus = jnp.maximum(a_in, 0.0) + jnp.log1p(jnp.exp(-jnp.abs(a_in)))
    gc = -jnp.exp(alog_ref[...]) * softplus
    pos = lax.broadcasted_iota(jnp.int32, gc.shape, 0) % CHUNK
    s = 1
    while s < CHUNK:
        gc = gc + jnp.where(pos >= s, pltpu.roll(gc, s, axis=0), 0.0)
        s *= 2
    gc_s[...] = gc
    gct = gc.T
    for c in range(n_chunks):
        blk = gct[:, c * CHUNK:(c + 1) * CHUNK]
        gct_s[c] = jnp.concatenate([blk, blk], axis=1)

    conv_in = []
    for grp, dst in enumerate((q_s, k_s, v_s)):
        col0 = grp * GDN_WIDTH
        slabs = pre_slab.at[grp]
        _fill_slabs(slabs, proj(col0, col0 + GDN_WIDTH))
        conv_in.append(conv_col(grp))
        scale = HEAD_DIM ** -0.5 if grp == 0 else 1.0
        for b in range(ROWS):
            for h in heads:
                classes = _conv_row_classes(slabs, b * HEADS + h, cw_ref, col0 + h * HEAD_DIM,
                                            GDN_CONV)
                for r, acc in enumerate(classes):
                    act = _silu(acc)
                    if grp < 2:
                        inv = lax.rsqrt(jnp.sum(act * act, axis=-1, keepdims=True) + EPS)
                        act = act * inv * scale
                    _store_row_class(dst, h, b, r, act)
    z_s[...] = proj(COL_Z, COL_CONV)
    gate_c = conv_col(3)

    gate_b, gate_cc, h_c = conv_in
    _fill_slabs(ch_slab, gate_cc * h_c)
    for b in range(ROWS):
        for j in range(conv_tiles):
            classes = _conv_row_classes(ch_slab, b * conv_tiles + j, ccw_ref, j * LANES,
                                        SHORT_CONV)
            for r, acc in enumerate(classes):
                _store_row_class(cv_s, j, b, r, acc)
    conv = jnp.concatenate([cv_s[j] for j in range(conv_tiles)], axis=1)
    y = gate_b * (conv + cb_ref[...]) * _silu(gate_c)
    res = x + _bdot(y, wo_ref[GDN_WIDTH:, :])

    ri = lax.broadcasted_iota(jnp.int32, (CHUNK, 2 * CHUNK), 0)
    ci = lax.broadcasted_iota(jnp.int32, (CHUNK, 2 * CHUNK), 1)
    cj = ci % CHUNK
    causal2 = ri >= cj
    strict2 = ri > cj
    left = ci < CHUNK
    eye_right = jnp.where((ri == cj) & (ci >= CHUNK), 1.0, 0.0)

    items = []
    for c in range(n_chunks):
        rows = slice(c * CHUNK, (c + 1) * CHUNK)
        beta_blk = beta_s[rows, :]
        gc_blk = gc_s[rows, :]
        eg_blk = jnp.exp(gc_blk)
        gct_blk = gct_s[c]
        for h in heads:
            items.append((rows, h, col(beta_blk, h), col(gc_blk, h), col(eg_blk, h),
                          gct_blk[h:h + 1, :]))
    n_items = range(len(items))
    k = [k_s[h, rows, :] for rows, h, *_ in items]
    kb = [k[n] * items[n][2] for n in n_items]
    decay2 = [jnp.where(causal2, jnp.exp(it[3] - it[5]), 0.0) for it in items]
    kq = [_bdot_nt(jnp.concatenate([kb[n], q_s[it[1], it[0], :]], axis=0),
                   jnp.concatenate([k[n], k[n]], axis=0)) for n, it in enumerate(items)]
    for n, it in enumerate(items):
        attn_s[it[1], it[0], :] = kq[n][CHUNK:, :CHUNK] * decay2[n][:, :CHUNK]
    a2 = [jnp.where(strict2, kq[n][:CHUNK] * decay2[n], 0.0) for n in n_items]
    prod = [_bdot(a[:, :CHUNK], a) for a in a2]
    pt = [jnp.where(left, prod[n], eye_right - a2[n]) for n in n_items]
    m = 4
    while m < CHUNK:
        prod = [_bdot(p[:, :CHUNK], p) for p in pt]
        pt = [jnp.where(left, prod[n], pt[n] + prod[n]) for n in n_items]
        m *= 2
    prod = [_bdot(p[:, :CHUNK], p) for p in pt]
    t_mat = [(pt[n] + prod[n])[:, CHUNK:] for n in n_items]
    rhs = [jnp.concatenate([v_s[it[1], it[0], :] * it[2], kb[n] * it[4]], axis=1)
           for n, it in enumerate(items)]
    sol = [_bdot(t_mat[n], rhs[n]) for n in n_items]
    for n, it in enumerate(items):
        u_s[it[0], lanes[it[1]]] = sol[n][:, :HEAD_DIM]
        w_s[it[0], lanes[it[1]]] = sol[n][:, HEAD_DIM:]

    chains = [(b, h) for b in range(ROWS) for h in heads]
    state = [s_ref[b * HEADS + h] for b, h in chains]
    for step in range(row_chunks):
        rows = [slice((b * row_chunks + step) * CHUNK, (b * row_chunks + step + 1) * CHUNK)
                for b in range(ROWS)]
        gc_blk = [gc_s[r, :] for r in rows]
        g_last = [g[CHUNK - 1:CHUNK, :] for g in gc_blk]
        eg_blk = [jnp.exp(g) for g in gc_blk]
        ek_blk = [jnp.exp(g_last[b] - gc_blk[b]) for b in range(ROWS)]
        egl_blk = [jnp.exp(g) for g in g_last]
        ws_qs = [_bdot(jnp.concatenate([w_s[rows[b], lanes[h]],
                                        q_s[h, rows[b], :] * col(eg_blk[b], h)], axis=0),
                       state[n]) for n, (b, h) in enumerate(chains)]
        v_new = [u_s[rows[b], lanes[h]] - ws_qs[n][:CHUNK] for n, (b, h) in enumerate(chains)]
        o = [ws_qs[n][CHUNK:] + _bdot(attn_s[h, rows[b], :], v_new[n])
             for n, (b, h) in enumerate(chains)]
        kv = [_bdot_tn(k_s[h, rows[b], :] * col(ek_blk[b], h), v_new[n])
              for n, (b, h) in enumerate(chains)]
        state = [state[n] * col(egl_blk[b], h) + kv[n] for n, (b, h) in enumerate(chains)]
        for n, (b, h) in enumerate(chains):
            o_s[rows[b], lanes[h]] = (_rmsnorm(o[n], gnw_ref[...])
                                      * _silu(z_s[rows[b], lanes[h]]))
    for n, (b, h) in enumerate(chains):
        s_ref[b * HEADS + h] = state[n]

    res = res + _bdot(o_s[...], wo_ref[:GDN_WIDTH, :])
    out_ref[...] = _rmsnorm(res, fnw_ref[...]).reshape(ROWS, TLR, D_MODEL)


def _block(x, norm_w, w_packed, conv_qkv_w, a_log_pad, dt_bias_pad, gdn_norm_w, conv_w, conv_b,
           w_out, out_norm_w):
    batch, seq, _ = x.shape
    n_chunks = TL // CHUNK
    conv_tiles = CONV_WIDTH // LANES
    tile = lambda g, t: (g, t, 0)
    const = lambda g, t: (0, 0)
    resident = lambda shape: pl.BlockSpec(shape, const, pipeline_mode=pl.Buffered(1))
    tile_f32 = pltpu.VMEM((TL, GDN_WIDTH), jnp.float32)
    head_tile_f32 = pltpu.VMEM((HEADS, TL, HEAD_DIM), jnp.float32)
    return pl.pallas_call(
        _block_kernel,
        grid=(batch // ROWS, seq // TLR),
        in_specs=[
            pl.BlockSpec((ROWS, TLR, D_MODEL), tile),
            resident((1, D_MODEL)),
            resident((D_MODEL, PROJ_PACKED)),
            resident((GDN_CONV, 3 * GDN_WIDTH)),
            resident((1, LANES)),
            resident((1, LANES)),
            resident((1, HEAD_DIM)),
            resident((SHORT_CONV, CONV_WIDTH)),
            resident((1, CONV_WIDTH)),
            resident((GDN_WIDTH + CONV_WIDTH, D_MODEL)),
            resident((1, D_MODEL)),
        ],
        out_specs=pl.BlockSpec((ROWS, TLR, D_MODEL), tile),
        out_shape=jax.ShapeDtypeStruct(x.shape, jnp.float32),
        scratch_shapes=[
            pltpu.VMEM((ROWS * HEADS, HEAD_DIM, HEAD_DIM), jnp.float32),
            pltpu.VMEM((3, ROWS * HEADS, HALO + TLR, LANES), jnp.float32),
            pltpu.VMEM((ROWS * conv_tiles, HALO + TLR, LANES), jnp.float32),
            pltpu.VMEM((conv_tiles, TL, LANES), jnp.float32),
            head_tile_f32, head_tile_f32, head_tile_f32,
            tile_f32,
            pltpu.VMEM((TL, LANES), jnp.float32),
            pltpu.VMEM((TL, LANES), jnp.float32),
            pltpu.VMEM((n_chunks, LANES, 2 * CHUNK), jnp.float32),
            tile_f32, tile_f32,
            pltpu.VMEM((HEADS, TL, CHUNK), jnp.float32),
            tile_f32,
        ],
        compiler_params=pltpu.CompilerParams(
            dimension_semantics=("arbitrary", "arbitrary"), vmem_limit_bytes=VMEM_LIMIT),
        name="hybrid_block",
    )(x, norm_w, w_packed, conv_qkv_w, a_log_pad, dt_bias_pad, gdn_norm_w, conv_w, conv_b,
      w_out, out_norm_w)


def _pad_lanes(v):
    return jnp.pad(v, ((0, 0), (0, LANES - v.shape[-1])))


def _pack_kernel(a_ref, m_ref, o_ref):
    j = pl.program_id(0)

    @pl.when(j < COL_CONV // PACK_TN)
    def _():
        o_ref[...] = a_ref[...].T.astype(jnp.bfloat16)

    @pl.when((j >= COL_CONV // PACK_TN) & (j < COL_BETA // PACK_TN))
    def _():
        rows = jnp.concatenate([a_ref[N_BA:, :], m_ref[...]], axis=0)
        o_ref[...] = rows.T.astype(jnp.bfloat16)

    @pl.when(j == COL_BETA // PACK_TN)
    def _():
        pad = jnp.zeros((LANES - HEADS, D_MODEL), jnp.float32)
        rest = jnp.zeros((PACK_TN - 2 * LANES, D_MODEL), jnp.float32)
        rows = jnp.concatenate([m_ref[:HEADS, :], pad, m_ref[HEADS:, :], pad, rest], axis=0)
        o_ref[...] = rows.T.astype(jnp.bfloat16)


def _pack_weights(wt):
    last = COL_BETA // PACK_TN
    mini_per_block = PACK_TN // N_BA
    mini_index = lambda j: jnp.where(j == last, COL_CONV // N_BA, (j + 1) * mini_per_block)
    return pl.pallas_call(
        _pack_kernel,
        grid=(pl.cdiv(PROJ_PACKED, PACK_TN),),
        in_specs=[
            pl.BlockSpec((PACK_TN, D_MODEL), lambda j: (jnp.minimum(j, last - 1), 0)),
            pl.BlockSpec((N_BA, D_MODEL), lambda j: (mini_index(j), 0)),
        ],
        out_specs=pl.BlockSpec((D_MODEL, PACK_TN), lambda j: (0, j)),
        out_shape=jax.ShapeDtypeStruct((D_MODEL, PROJ_PACKED), jnp.bfloat16),
        compiler_params=pltpu.CompilerParams(
            dimension_semantics=("arbitrary",), vmem_limit_bytes=VMEM_LIMIT),
        name="pack_weights",
    )(wt, wt)


def _layer(x, norm_in_w, w_in, conv_qkv_w, a_log, dt_bias, gdn_norm_w, conv_w, conv_b, w_out,
           out_norm_w):
    assert w_in.shape[1] == COL_BETA + N_BA
    w_packed = _pack_weights(w_in.T)
    return _block(x, norm_in_w.reshape(1, -1), w_packed, conv_qkv_w,
                  _pad_lanes(a_log.reshape(1, -1)), _pad_lanes(dt_bias.reshape(1, -1)),
                  gdn_norm_w.reshape(1, -1), conv_w, conv_b.reshape(1, -1),
                  w_out.astype(jnp.bfloat16), out_norm_w.reshape(1, -1))


def kernel(x, norm_in_w, w_in, conv_qkv_w, A_log, dt_bias, gdn_norm_w, conv_w, conv_b, w_out,
           final_norm_w):
    batch, seq, d_model = x.shape
    assert norm_in_w.shape[0] == 1 and d_model == D_MODEL
    assert batch % ROWS == 0 and seq % TLR == 0 and TLR % CHUNK == 0
    return _layer(x, norm_in_w[0], w_in[0], conv_qkv_w[0], A_log[0], dt_bias[0], gdn_norm_w[0],
                  conv_w[0], conv_b[0], w_out[0], final_norm_w)
```

```python
import jax
import jax.numpy as jnp
from jax import lax
from jax.experimental import pallas as pl
from jax.experimental.pallas import tpu as pltpu

D_MODEL = 1024
HEADS = 8
HEAD_DIM = 128
GDN_WIDTH = HEADS * HEAD_DIM
CONV_WIDTH = 1024
GDN_CONV = 4
SHORT_CONV = 3
CHUNK = 64
EPS = 1e-6
LANES = 128
HALO = 8
ROW_STRIDE = 4

COL_Z = 3 * GDN_WIDTH
COL_CONV = COL_Z + GDN_WIDTH
COL_BETA = COL_CONV + 4 * CONV_WIDTH
COL_A = COL_BETA + LANES
PROJ_PACKED = COL_A + LANES
N_BA = 2 * HEADS
PACK_TN = 1024

ROWS = 8
TLR = 64
TL = ROWS * TLR
VMEM_LIMIT = 63 * 1024 * 1024


def _bdot(a, b):
    return jnp.dot(a.astype(jnp.bfloat16), b.astype(jnp.bfloat16),
                   preferred_element_type=jnp.float32)


def _bdot_nt(a, b):
    return lax.dot_general(a.astype(jnp.bfloat16), b.astype(jnp.bfloat16),
                           (((1,), (1,)), ((), ())), preferred_element_type=jnp.float32)


def _bdot_tn(a, b):
    return lax.dot_general(a.astype(jnp.bfloat16), b.astype(jnp.bfloat16),
                           (((0,), (0,)), ((), ())), preferred_element_type=jnp.float32)


def _silu(x):
    return x * jax.nn.sigmoid(x)


def _rmsnorm(x, w):
    ms = jnp.mean(x * x, axis=-1, keepdims=True)
    return x * lax.rsqrt(ms + EPS) * w


def _fill_slabs(slab_ref, x):
    n = x.shape[1] // LANES
    for b in range(ROWS):
        for j in range(n):
            slab_ref[b * n + j, HALO:, :] = x[b * TLR:(b + 1) * TLR, j * LANES:(j + 1) * LANES]


def _conv_row_classes(slab_ref, halo_ref, j, w_ref, lane0, width):
    slab_ref[j, :HALO, :] = halo_ref[j]
    taps = [w_ref[width - 1 - s:width - s, lane0:lane0 + LANES] for s in range(width)]
    loaded = {}

    def rows_from(start):
        if start not in loaded:
            loaded[start] = slab_ref[j, pl.ds(start, TLR // ROW_STRIDE, stride=ROW_STRIDE), :]
        return loaded[start]

    out = []
    for r in range(ROW_STRIDE):
        acc = rows_from(HALO + r) * taps[0]
        for s in range(1, width):
            acc = acc + rows_from(HALO + r - s) * taps[s]
        out.append(acc)
    halo_ref[j] = slab_ref[j, TLR:, :]
    return out


def _store_row_class(dst_ref, j, b, r, val):
    dst_ref[j, pl.ds(b * TLR + r, TLR // ROW_STRIDE, stride=ROW_STRIDE), :] = val


def _block_kernel(x_ref, nw_ref, wp_ref, cw_ref, alog_ref, dtb_ref, gnw_ref, ccw_ref, cb_ref,
                  wo_ref, fnw_ref, out_ref,
                  s_ref, slab, pre_halo, ch_halo, cv_s, q_s, k_s, v_s, z_s, beta_s, gc_s, gct_s,
                  u_s, w_s, attn_s, o_s):
    n_chunks = TL // CHUNK
    row_chunks = TLR // CHUNK
    heads = range(HEADS)
    lanes = [slice(h * HEAD_DIM, (h + 1) * HEAD_DIM) for h in heads]
    col = lambda blk, h: blk[:, h:h + 1]
    conv_tiles = CONV_WIDTH // LANES

    @pl.when(pl.program_id(1) == 0)
    def _():
        s_ref[...] = jnp.zeros_like(s_ref)
        pre_halo[...] = jnp.zeros_like(pre_halo)
        ch_halo[...] = jnp.zeros_like(ch_halo)

    x = x_ref[...].reshape(TL, D_MODEL)
    hn = _rmsnorm(x, nw_ref[...]).astype(jnp.bfloat16)
    proj = lambda lo, hi: jnp.dot(hn, wp_ref[:, lo:hi], preferred_element_type=jnp.float32)
    conv_col = lambda j: proj(COL_CONV + j * CONV_WIDTH, COL_CONV + (j + 1) * CONV_WIDTH)

    ba = proj(COL_BETA, PROJ_PACKED)
    beta_s[...] = jax.nn.sigmoid(ba[:, :LANES])
    a_in = ba[:, LANES:] + dtb_ref[...]
    softplus = jnp.maximum(a_in, 0.0) + jnp.log1p(jnp.exp(-jnp.abs(a_in)))
    gc = -jnp.exp(alog_ref[...]) * softplus
    pos = lax.broadcasted_iota(jnp.int32, gc.shape, 0) % CHUNK
    s = 1
    while s < CHUNK:
        gc = gc + jnp.where(pos >= s, pltpu.roll(gc, s, axis=0), 0.0)
        s *= 2
    gc_s[...] = gc
    gct = gc.T
    for c in range(n_chunks):
        blk = gct[:, c * CHUNK:(c + 1) * CHUNK]
        gct_s[c] = jnp.concatenate([blk, blk], axis=1)

    conv_in = []
    for grp, dst in enumerate((q_s, k_s, v_s)):
        col0 = grp * GDN_WIDTH
        _fill_slabs(slab, proj(col0, col0 + GDN_WIDTH))
        conv_in.append(conv_col(grp))
        scale = HEAD_DIM ** -0.5 if grp == 0 else 1.0
        for b in range(ROWS):
            for h in heads:
                classes = _conv_row_classes(slab, pre_halo.at[grp], b * HEADS + h, cw_ref,
                                            col0 + h * HEAD_DIM,
                                            GDN_CONV)
                for r, acc in enumerate(classes):
                    act = _silu(acc)
                    if grp < 2:
                        inv = lax.rsqrt(jnp.sum(act * act, axis=-1, keepdims=True) + EPS)
                        act = act * inv * scale
                    _store_row_class(dst, h, b, r, act)
    z_s[...] = proj(COL_Z, COL_CONV)
    gate_c = conv_col(3)

    gate_b, gate_cc, h_c = conv_in
    _fill_slabs(slab, gate_cc * h_c)
    for b in range(ROWS):
        for j in range(conv_tiles):
            classes = _conv_row_classes(slab, ch_halo, b * conv_tiles + j, ccw_ref, j * LANES,
                                        SHORT_CONV)
            for r, acc in enumerate(classes):
                _store_row_class(cv_s, j, b, r, acc)
    conv = jnp.concatenate([cv_s[j] for j in range(conv_tiles)], axis=1)
    y = gate_b * (conv + cb_ref[...]) * _silu(gate_c)
    res = x + _bdot(y, wo_ref[GDN_WIDTH:, :])

    ri = lax.broadcasted_iota(jnp.int32, (CHUNK, 2 * CHUNK), 0)
    ci = lax.broadcasted_iota(jnp.int32, (CHUNK, 2 * CHUNK), 1)
    cj = ci % CHUNK
    causal2 = ri >= cj
    strict2 = ri > cj
    left = ci < CHUNK
    eye_right = jnp.where((ri == cj) & (ci >= CHUNK), 1.0, 0.0)

    items = []
    for c in range(n_chunks):
        rows = slice(c * CHUNK, (c + 1) * CHUNK)
        beta_blk = beta_s[rows, :]
        gc_blk = gc_s[rows, :]
        eg_blk = jnp.exp(gc_blk)
        gct_blk = gct_s[c]
        for h in heads:
            items.append((rows, h, col(beta_blk, h), col(gc_blk, h), col(eg_blk, h),
                          gct_blk[h:h + 1, :]))
    n_items = range(len(items))
    k = [k_s[h, rows, :] for rows, h, *_ in items]
    kb = [k[n] * items[n][2] for n in n_items]
    decay2 = [jnp.where(causal2, jnp.exp(it[3] - it[5]), 0.0) for it in items]
    kq = [_bdot_nt(jnp.concatenate([kb[n], q_s[it[1], it[0], :]], axis=0),
                   jnp.concatenate([k[n], k[n]], axis=0)) for n, it in enumerate(items)]
    for n, it in enumerate(items):
        attn_s[it[1], it[0], :] = kq[n][CHUNK:, :CHUNK] * decay2[n][:, :CHUNK]
    a2 = [jnp.where(strict2, kq[n][:CHUNK] * decay2[n], 0.0) for n in n_items]
    prod = [_bdot(a[:, :CHUNK], a) for a in a2]
    pt = [jnp.where(left, prod[n], eye_right - a2[n]) for n in n_items]
    m = 4
    while m < CHUNK:
        prod = [_bdot(p[:, :CHUNK], p) for p in pt]
        pt = [jnp.where(left, prod[n], pt[n] + prod[n]) for n in n_items]
        m *= 2
    prod = [_bdot(p[:, :CHUNK], p) for p in pt]
    t_mat = [(pt[n] + prod[n])[:, CHUNK:] for n in n_items]
    rhs = [jnp.concatenate([v_s[it[1], it[0], :] * it[2], kb[n] * it[4]], axis=1)
           for n, it in enumerate(items)]
    sol = [_bdot(t_mat[n], rhs[n]) for n in n_items]
    for n, it in enumerate(items):
        u_s[it[0], lanes[it[1]]] = sol[n][:, :HEAD_DIM]
        w_s[it[0], lanes[it[1]]] = sol[n][:, HEAD_DIM:]

    chains = [(b, h) for b in range(ROWS) for h in heads]
    state = [s_ref[b * HEADS + h] for b, h in chains]
    for step in range(row_chunks):
        rows = [slice((b * row_chunks + step) * CHUNK, (b * row_chunks + step + 1) * CHUNK)
                for b in range(ROWS)]
        gc_blk = [gc_s[r, :] for r in rows]
        g_last = [g[CHUNK - 1:CHUNK, :] for g in gc_blk]
        eg_blk = [jnp.exp(g) for g in gc_blk]
        ek_blk = [jnp.exp(g_last[b] - gc_blk[b]) for b in range(ROWS)]
        egl_blk = [jnp.exp(g) for g in g_last]
        ws_qs = [_bdot(jnp.concatenate([w_s[rows[b], lanes[h]],
                                        q_s[h, rows[b], :] * col(eg_blk[b], h)], axis=0),
                       state[n]) for n, (b, h) in enumerate(chains)]
        v_new = [u_s[rows[b], lanes[h]] - ws_qs[n][:CHUNK] for n, (b, h) in enumerate(chains)]
        o = [ws_qs[n][CHUNK:] + _bdot(attn_s[h, rows[b], :], v_new[n])
             for n, (b, h) in enumerate(chains)]
        kv = [_bdot_tn(k_s[h, rows[b], :] * col(ek_blk[b], h), v_new[n])
              for n, (b, h) in enumerate(chains)]
        state = [state[n] * col(egl_blk[b], h) + kv[n] for n, (b, h) in enumerate(chains)]
        for n, (b, h) in enumerate(chains):
            o_s[rows[b], lanes[h]] = (_rmsnorm(o[n], gnw_ref[...])
                                      * _silu(z_s[rows[b], lanes[h]]))
    for n, (b, h) in enumerate(chains):
        s_ref[b * HEADS + h] = state[n]

    res = res + _bdot(o_s[...], wo_ref[:GDN_WIDTH, :])
    out_ref[...] = _rmsnorm(res, fnw_ref[...]).reshape(ROWS, TLR, D_MODEL)


def _block(x, norm_w, w_packed, conv_qkv_w, a_log_pad, dt_bias_pad, gdn_norm_w, conv_w, conv_b,
           w_out, out_norm_w):
    batch, seq, _ = x.shape
    n_chunks = TL // CHUNK
    conv_tiles = CONV_WIDTH // LANES
    tile = lambda g, t: (g, t, 0)
    const = lambda g, t: (0, 0)
    resident = lambda shape: pl.BlockSpec(shape, const, pipeline_mode=pl.Buffered(1))
    tile_f32 = pltpu.VMEM((TL, GDN_WIDTH), jnp.float32)
    head_tile_f32 = pltpu.VMEM((HEADS, TL, HEAD_DIM), jnp.float32)
    return pl.pallas_call(
        _block_kernel,
        grid=(batch // ROWS, seq // TLR),
        in_specs=[
            pl.BlockSpec((ROWS, TLR, D_MODEL), tile),
            resident((1, D_MODEL)),
            resident((D_MODEL, PROJ_PACKED)),
            resident((GDN_CONV, 3 * GDN_WIDTH)),
            resident((1, LANES)),
            resident((1, LANES)),
            resident((1, HEAD_DIM)),
            resident((SHORT_CONV, CONV_WIDTH)),
            resident((1, CONV_WIDTH)),
            resident((GDN_WIDTH + CONV_WIDTH, D_MODEL)),
            resident((1, D_MODEL)),
        ],
        out_specs=pl.BlockSpec((ROWS, TLR, D_MODEL), tile),
        out_shape=jax.ShapeDtypeStruct(x.shape, jnp.float32),
        scratch_shapes=[
            pltpu.VMEM((ROWS * HEADS, HEAD_DIM, HEAD_DIM), jnp.float32),
            pltpu.VMEM((ROWS * HEADS, HALO + TLR, LANES), jnp.float32),
            pltpu.VMEM((3, ROWS * HEADS, HALO, LANES), jnp.float32),
            pltpu.VMEM((ROWS * conv_tiles, HALO, LANES), jnp.float32),
            pltpu.VMEM((conv_tiles, TL, LANES), jnp.float32),
            head_tile_f32, head_tile_f32, head_tile_f32,
            tile_f32,
            pltpu.VMEM((TL, LANES), jnp.float32),
            pltpu.VMEM((TL, LANES), jnp.float32),
            pltpu.VMEM((n_chunks, LANES, 2 * CHUNK), jnp.float32),
            tile_f32, tile_f32,
            pltpu.VMEM((HEADS, TL, CHUNK), jnp.float32),
            tile_f32,
        ],
        compiler_params=pltpu.CompilerParams(
            dimension_semantics=("arbitrary", "arbitrary"), vmem_limit_bytes=VMEM_LIMIT),
        name="hybrid_block",
    )(x, norm_w, w_packed, conv_qkv_w, a_log_pad, dt_bias_pad, gdn_norm_w, conv_w, conv_b,
      w_out, out_norm_w)


def _pad_lanes(v):
    return jnp.pad(v, ((0, 0), (0, LANES - v.shape[-1])))


def _pack_kernel(a_ref, m_ref, o_ref):
    j = pl.program_id(0)

    @pl.when(j < COL_CONV // PACK_TN)
    def _():
        o_ref[...] = a_ref[...].T.astype(jnp.bfloat16)

    @pl.when((j >= COL_CONV // PACK_TN) & (j < COL_BETA // PACK_TN))
    def _():
        rows = jnp.concatenate([a_ref[N_BA:, :], m_ref[...]], axis=0)
        o_ref[...] = rows.T.astype(jnp.bfloat16)

    @pl.when(j == COL_BETA // PACK_TN)
    def _():
        pad = jnp.zeros((LANES - HEADS, D_MODEL), jnp.float32)
        rest = jnp.zeros((PACK_TN - 2 * LANES, D_MODEL), jnp.float32)
        rows = jnp.concatenate([m_ref[:HEADS, :], pad, m_ref[HEADS:, :], pad, rest], axis=0)
        o_ref[...] = rows.T.astype(jnp.bfloat16)


def _pack_weights(wt):
    last = COL_BETA // PACK_TN
    mini_per_block = PACK_TN // N_BA
    mini_index = lambda j: jnp.where(j == last, COL_CONV // N_BA, (j + 1) * mini_per_block)
    return pl.pallas_call(
        _pack_kernel,
        grid=(pl.cdiv(PROJ_PACKED, PACK_TN),),
        in_specs=[
            pl.BlockSpec((PACK_TN, D_MODEL), lambda j: (jnp.minimum(j, last - 1), 0)),
            pl.BlockSpec((N_BA, D_MODEL), lambda j: (mini_index(j), 0)),
        ],
        out_specs=pl.BlockSpec((D_MODEL, PACK_TN), lambda j: (0, j)),
        out_shape=jax.ShapeDtypeStruct((D_MODEL, PROJ_PACKED), jnp.bfloat16),
        compiler_params=pltpu.CompilerParams(
            dimension_semantics=("arbitrary",), vmem_limit_bytes=VMEM_LIMIT),
        name="pack_weights",
    )(wt, wt)


def _layer(x, norm_in_w, w_in, conv_qkv_w, a_log, dt_bias, gdn_norm_w, conv_w, conv_b, w_out,
           out_norm_w):
    assert w_in.shape[1] == COL_BETA + N_BA
    w_packed = _pack_weights(w_in.T)
    return _block(x, norm_in_w.reshape(1, -1), w_packed, conv_qkv_w,
                  _pad_lanes(a_log.reshape(1, -1)), _pad_lanes(dt_bias.reshape(1, -1)),
                  gdn_norm_w.reshape(1, -1), conv_w, conv_b.reshape(1, -1),
                  w_out.astype(jnp.bfloat16), out_norm_w.reshape(1, -1))


def kernel(x, norm_in_w, w_in, conv_qkv_w, A_log, dt_bias, gdn_norm_w, conv_w, conv_b, w_out,
           final_norm_w):
    batch, seq, d_model = x.shape
    assert norm_in_w.shape[0] == 1 and d_model == D_MODEL
    assert batch % ROWS == 0 and seq % TLR == 0 and TLR % CHUNK == 0
    return _layer(x, norm_in_w[0], w_in[0], conv_qkv_w[0], A_log[0], dt_bias[0], gdn_norm_w[0],
                  conv_w[0], conv_b[0], w_out[0], final_norm_w)
```

```python
import jax
import jax.numpy as jnp
from jax import lax
from jax.experimental import pallas as pl
from jax.experimental.pallas import tpu as pltpu

D_MODEL = 1024
HEADS = 8
HEAD_DIM = 128
GDN_WIDTH = HEADS * HEAD_DIM
CONV_WIDTH = 1024
GDN_CONV = 4
SHORT_CONV = 3
CHUNK = 64
EPS = 1e-6
LANES = 128
HALO = 8
ROW_STRIDE = 4

COL_Z = 3 * GDN_WIDTH
COL_CONV = COL_Z + GDN_WIDTH
COL_BETA = COL_CONV + 4 * CONV_WIDTH
COL_A = COL_BETA + LANES
PROJ_PACKED = COL_A + LANES
N_BA = 2 * HEADS
PACK_TN = 1024

ROWS = 4
TLR = 64
TL = ROWS * TLR
VMEM_LIMIT = 60 * 1024 * 1024


def _bdot(a, b):
    return jnp.dot(a.astype(jnp.bfloat16), b.astype(jnp.bfloat16),
                   preferred_element_type=jnp.float32)


def _bdot_nt(a, b):
    return lax.dot_general(a.astype(jnp.bfloat16), b.astype(jnp.bfloat16),
                           (((1,), (1,)), ((), ())), preferred_element_type=jnp.float32)


def _bdot_tn(a, b):
    return lax.dot_general(a.astype(jnp.bfloat16), b.astype(jnp.bfloat16),
                           (((0,), (0,)), ((), ())), preferred_element_type=jnp.float32)


def _silu(x):
    return x * jax.nn.sigmoid(x)


def _rmsnorm(x, w):
    ms = jnp.mean(x * x, axis=-1, keepdims=True)
    return x * lax.rsqrt(ms + EPS) * w


def _fill_slabs(slab_ref, x):
    n = x.shape[1] // LANES
    for b in range(ROWS):
        for j in range(n):
            slab_ref[b * n + j, HALO:, :] = x[b * TLR:(b + 1) * TLR, j * LANES:(j + 1) * LANES]


def _conv_row_classes(slab_ref, j, w_ref, lane0, width):
    taps = [w_ref[width - 1 - s:width - s, lane0:lane0 + LANES] for s in range(width)]
    loaded = {}

    def rows_from(start):
        if start not in loaded:
            loaded[start] = slab_ref[j, pl.ds(start, TLR // ROW_STRIDE, stride=ROW_STRIDE), :]
        return loaded[start]

    out = []
    for r in range(ROW_STRIDE):
        acc = rows_from(HALO + r) * taps[0]
        for s in range(1, width):
            acc = acc + rows_from(HALO + r - s) * taps[s]
        out.append(acc)
    slab_ref[j, :HALO, :] = slab_ref[j, TLR:, :]
    return out


def _store_row_class(dst_ref, j, b, r, val):
    dst_ref[j, pl.ds(b * TLR + r, TLR // ROW_STRIDE, stride=ROW_STRIDE), :] = val


def _block_kernel(x_ref, nw_ref, wp_ref, cw_ref, alog_ref, dtb_ref, gnw_ref, ccw_ref, cb_ref,
                  wo_ref, fnw_ref, out_ref,
                  s_ref, pre_slab, ch_slab, cv_s, q_s, k_s, v_s, z_s, beta_s, gc_s, gct_s, u_s,
                  w_s, attn_s, o_s):
    n_chunks = TL // CHUNK
    row_chunks = TLR // CHUNK
    heads = range(HEADS)
    lanes = [slice(h * HEAD_DIM, (h + 1) * HEAD_DIM) for h in heads]
    col = lambda blk, h: blk[:, h:h + 1]
    conv_tiles = CONV_WIDTH // LANES

    @pl.when(pl.program_id(1) == 0)
    def _():
        s_ref[...] = jnp.zeros_like(s_ref)
        pre_slab[:, :, :HALO, :] = jnp.zeros(pre_slab.shape[:2] + (HALO, LANES), jnp.float32)
        ch_slab[:, :HALO, :] = jnp.zeros((ch_slab.shape[0], HALO, LANES), jnp.float32)

    x = x_ref[...].reshape(TL, D_MODEL)
    hn = _rmsnorm(x, nw_ref[...]).astype(jnp.bfloat16)
    proj = lambda lo, hi: jnp.dot(hn, wp_ref[:, lo:hi], preferred_element_type=jnp.float32)
    conv_col = lambda j: proj(COL_CONV + j * CONV_WIDTH, COL_CONV + (j + 1) * CONV_WIDTH)

    ba = proj(COL_BETA, PROJ_PACKED)
    beta_s[...] = jax.nn.sigmoid(ba[:, :LANES])
    a_in = ba[:, LANES:] + dtb_ref[...]
    softplus = jnp.maximum(a_in, 0.0) + jnp.log1p(jnp.exp(-jnp.abs(a_in)))
    gc = -jnp.exp(alog_ref[...]) * softplus
    pos = lax.broadcasted_iota(jnp.int32, gc.shape, 0) % CHUNK
    s = 1
    while s < CHUNK:
        gc = gc + jnp.where(pos >= s, pltpu.roll(gc, s, axis=0), 0.0)
        s *= 2
    gc_s[...] = gc
    gct = gc.T
    for c in range(n_chunks):
        blk = gct[:, c * CHUNK:(c + 1) * CHUNK]
        gct_s[c] = jnp.concatenate([blk, blk], axis=1)

    conv_in = []
    for grp, dst in enumerate((q_s, k_s, v_s)):
        col0 = grp * GDN_WIDTH
        slabs = pre_slab.at[grp]
        _fill_slabs(slabs, proj(col0, col0 + GDN_WIDTH))
        conv_in.append(conv_col(grp))
        scale = HEAD_DIM ** -0.5 if grp == 0 else 1.0
        for b in range(ROWS):
            for h in heads:
                classes = _conv_row_classes(slabs, b * HEADS + h, cw_ref, col0 + h * HEAD_DIM,
                                            GDN_CONV)
                for r, acc in enumerate(classes):
                    act = _silu(acc)
                    if grp < 2:
                        inv = lax.rsqrt(jnp.sum(act * act, axis=-1, keepdims=True) + EPS)
                        act = act * inv * scale
                    _store_row_class(dst, h, b, r, act)
    z_s[...] = proj(COL_Z, COL_CONV)
    gate_c = conv_col(3)

    gate_b, gate_cc, h_c = conv_in
    _fill_slabs(ch_slab, gate_cc * h_c)
    for b in range(ROWS):
        for j in range(conv_tiles):
            classes = _conv_row_classes(ch_slab, b * conv_tiles + j, ccw_ref, j * LANES,
                                        SHORT_CONV)
            for r, acc in enumerate(classes):
                _store_row_class(cv_s, j, b, r, acc)
    conv = jnp.concatenate([cv_s[j] for j in range(conv_tiles)], axis=1)
    y = gate_b * (conv + cb_ref[...]) * _silu(gate_c)
    res = x + _bdot(y, wo_ref[GDN_WIDTH:, :])

    ri = lax.broadcasted_iota(jnp.int32, (CHUNK, 2 * CHUNK), 0)
    ci = lax.broadcasted_iota(jnp.int32, (CHUNK, 2 * CHUNK), 1)
    cj = ci % CHUNK
    causal2 = ri >= cj
    strict2 = ri > cj
    left = ci < CHUNK
    eye_right = jnp.where((ri == cj) & (ci >= CHUNK), 1.0, 0.0)
    keep_right = jnp.where(left, 0.0, 1.0)

    items = []
    for c in range(n_chunks):
        rows = slice(c * CHUNK, (c + 1) * CHUNK)
        beta_blk = beta_s[rows, :]
        gc_blk = gc_s[rows, :]
        eg_blk = jnp.exp(gc_blk)
        gct_blk = gct_s[c]
        for h in heads:
            items.append((rows, h, col(beta_blk, h), col(gc_blk, h), col(eg_blk, h),
                          gct_blk[h:h + 1, :]))
    n_items = range(len(items))
    k = [k_s[h, rows, :] for rows, h, *_ in items]
    kb = [k[n] * items[n][2] for n in n_items]
    decay2 = [jnp.where(causal2, jnp.exp(it[3] - it[5]), 0.0) for it in items]
    kq = [_bdot_nt(jnp.concatenate([kb[n], q_s[it[1], it[0], :]], axis=0),
                   jnp.concatenate([k[n], k[n]], axis=0)) for n, it in enumerate(items)]
    for n, it in enumerate(items):
        attn_s[it[1], it[0], :] = kq[n][CHUNK:, :CHUNK] * decay2[n][:, :CHUNK]
    a2 = [jnp.where(strict2, kq[n][:CHUNK] * decay2[n], 0.0) for n in n_items]
    prod = [_bdot(a[:, :CHUNK], a) for a in a2]
    pt = [jnp.where(left, prod[n], eye_right - a2[n]) for n in n_items]
    m = 4
    while m < CHUNK:
        prod = [_bdot(p[:, :CHUNK], p) for p in pt]
        pt = [prod[n] + pt[n] * keep_right for n in n_items]
        m *= 2
    prod = [_bdot(p[:, :CHUNK], p) for p in pt]
    t_mat = [(pt[n] + prod[n])[:, CHUNK:] for n in n_items]
    rhs = [jnp.concatenate([v_s[it[1], it[0], :] * it[2], kb[n] * it[4]], axis=1)
           for n, it in enumerate(items)]
    sol = [_bdot(t_mat[n], rhs[n]) for n in n_items]
    for n, it in enumerate(items):
        u_s[it[0], lanes[it[1]]] = sol[n][:, :HEAD_DIM]
        w_s[it[0], lanes[it[1]]] = sol[n][:, HEAD_DIM:]

    chains = [(b, h) for b in range(ROWS) for h in heads]
    state = [s_ref[b * HEADS + h] for b, h in chains]
    for step in range(row_chunks):
        rows = [slice((b * row_chunks + step) * CHUNK, (b * row_chunks + step + 1) * CHUNK)
                for b in range(ROWS)]
        gc_blk = [gc_s[r, :] for r in rows]
        g_last = [g[CHUNK - 1:CHUNK, :] for g in gc_blk]
        eg_blk = [jnp.exp(g) for g in gc_blk]
        ek_blk = [jnp.exp(g_last[b] - gc_blk[b]) for b in range(ROWS)]
        egl_blk = [jnp.exp(g) for g in g_last]
        ws_qs = [_bdot(jnp.concatenate([w_s[rows[b], lanes[h]],
                                        q_s[h, rows[b], :] * col(eg_blk[b], h)], axis=0),
                       state[n]) for n, (b, h) in enumerate(chains)]
        v_new = [u_s[rows[b], lanes[h]] - ws_qs[n][:CHUNK] for n, (b, h) in enumerate(chains)]
        o = [ws_qs[n][CHUNK:] + _bdot(attn_s[h, rows[b], :], v_new[n])
             for n, (b, h) in enumerate(chains)]
        kv = [_bdot_tn(k_s[h, rows[b], :] * col(ek_blk[b], h), v_new[n])
              for n, (b, h) in enumerate(chains)]
        state = [state[n] * col(egl_blk[b], h) + kv[n] for n, (b, h) in enumerate(chains)]
        for n, (b, h) in enumerate(chains):
            o_s[rows[b], lanes[h]] = (_rmsnorm(o[n], gnw_ref[...])
                                      * _silu(z_s[rows[b], lanes[h]]))
    for n, (b, h) in enumerate(chains):
        s_ref[b * HEADS + h] = state[n]

    res = res + _bdot(o_s[...], wo_ref[:GDN_WIDTH, :])
    out_ref[...] = _rmsnorm(res, fnw_ref[...]).reshape(ROWS, TLR, D_MODEL)


def _block(x, norm_w, w_packed, conv_qkv_w, a_log_pad, dt_bias_pad, gdn_norm_w, conv_w, conv_b,
           w_out, out_norm_w):
    batch, seq, _ = x.shape
    n_chunks = TL // CHUNK
    conv_tiles = CONV_WIDTH // LANES
    tile = lambda g, t: (g, t, 0)
    const = lambda g, t: (0, 0)
    resident = lambda shape: pl.BlockSpec(shape, const, pipeline_mode=pl.Buffered(1))
    tile_f32 = pltpu.VMEM((TL, GDN_WIDTH), jnp.float32)
    head_tile_f32 = pltpu.VMEM((HEADS, TL, HEAD_DIM), jnp.float32)
    return pl.pallas_call(
        _block_kernel,
        grid=(batch // ROWS, seq // TLR),
        in_specs=[
            pl.BlockSpec((ROWS, TLR, D_MODEL), tile),
            resident((1, D_MODEL)),
            resident((D_MODEL, PROJ_PACKED)),
            resident((GDN_CONV, 3 * GDN_WIDTH)),
            resident((1, LANES)),
            resident((1, LANES)),
            resident((1, HEAD_DIM)),
            resident((SHORT_CONV, CONV_WIDTH)),
            resident((1, CONV_WIDTH)),
            resident((GDN_WIDTH + CONV_WIDTH, D_MODEL)),
            resident((1, D_MODEL)),
        ],
        out_specs=pl.BlockSpec((ROWS, TLR, D_MODEL), tile),
        out_shape=jax.ShapeDtypeStruct(x.shape, jnp.float32),
        scratch_shapes=[
            pltpu.VMEM((ROWS * HEADS, HEAD_DIM, HEAD_DIM), jnp.float32),
            pltpu.VMEM((3, ROWS * HEADS, HALO + TLR, LANES), jnp.float32),
            pltpu.VMEM((ROWS * conv_tiles, HALO + TLR, LANES), jnp.float32),
            pltpu.VMEM((conv_tiles, TL, LANES), jnp.float32),
            head_tile_f32, head_tile_f32, head_tile_f32,
            tile_f32,
            pltpu.VMEM((TL, LANES), jnp.float32),
            pltpu.VMEM((TL, LANES), jnp.float32),
            pltpu.VMEM((n_chunks, LANES, 2 * CHUNK), jnp.float32),
            tile_f32, tile_f32,
            pltpu.VMEM((HEADS, TL, CHUNK), jnp.float32),
            tile_f32,
        ],
        compiler_params=pltpu.CompilerParams(
            dimension_semantics=("arbitrary", "arbitrary"), vmem_limit_bytes=VMEM_LIMIT),
        name="hybrid_block",
    )(x, norm_w, w_packed, conv_qkv_w, a_log_pad, dt_bias_pad, gdn_norm_w, conv_w, conv_b,
      w_out, out_norm_w)


def _pad_lanes(v):
    return jnp.pad(v, ((0, 0), (0, LANES - v.shape[-1])))


def _pack_kernel(a_ref, m_ref, o_ref):
    j = pl.program_id(0)

    @pl.when(j < COL_CONV // PACK_TN)
    def _():
        o_ref[...] = a_ref[...].T.astype(jnp.bfloat16)

    @pl.when((j >= COL_CONV // PACK_TN) & (j < COL_BETA // PACK_TN))
    def _():
        rows = jnp.concatenate([a_ref[N_BA:, :], m_ref[...]], axis=0)
        o_ref[...] = rows.T.astype(jnp.bfloat16)

    @pl.when(j == COL_BETA // PACK_TN)
    def _():
        pad = jnp.zeros((LANES - HEADS, D_MODEL), jnp.float32)
        rest = jnp.zeros((PACK_TN - 2 * LANES, D_MODEL), jnp.float32)
        rows = jnp.concatenate([m_ref[:HEADS, :], pad, m_ref[HEADS:, :], pad, rest], axis=0)
        o_ref[...] = rows.T.astype(jnp.bfloat16)


def _pack_weights(wt):
    last = COL_BETA // PACK_TN
    mini_per_block = PACK_TN // N_BA
    mini_index = lambda j: jnp.where(j == last, COL_CONV // N_BA, (j + 1) * mini_per_block)
    return pl.pallas_call(
        _pack_kernel,
        grid=(pl.cdiv(PROJ_PACKED, PACK_TN),),
        in_specs=[
            pl.BlockSpec((PACK_TN, D_MODEL), lambda j: (jnp.minimum(j, last - 1), 0)),
            pl.BlockSpec((N_BA, D_MODEL), lambda j: (mini_index(j), 0)),
        ],
        out_specs=pl.BlockSpec((D_MODEL, PACK_TN), lambda j: (0, j)),
        out_shape=jax.ShapeDtypeStruct((D_MODEL, PROJ_PACKED), jnp.bfloat16),
        compiler_params=pltpu.CompilerParams(
            dimension_semantics=("arbitrary",), vmem_limit_bytes=VMEM_LIMIT),
        name="pack_weights",
    )(wt, wt)


def _layer(x, norm_in_w, w_in, conv_qkv_w, a_log, dt_bias, gdn_norm_w, conv_w, conv_b, w_out,
           out_norm_w):
    assert w_in.shape[1] == COL_BETA + N_BA
    w_packed = _pack_weights(w_in.T)
    return _block(x, norm_in_w.reshape(1, -1), w_packed, conv_qkv_w,
                  _pad_lanes(a_log.reshape(1, -1)), _pad_lanes(dt_bias.reshape(1, -1)),
                  gdn_norm_w.reshape(1, -1), conv_w, conv_b.reshape(1, -1),
                  w_out.astype(jnp.bfloat16), out_norm_w.reshape(1, -1))


def kernel(x, norm_in_w, w_in, conv_qkv_w, A_log, dt_bias, gdn_norm_w, conv_w, conv_b, w_out,
           final_norm_w):
    batch, seq, d_model = x.shape
    assert norm_in_w.shape[0] == 1 and d_model == D_MODEL
    assert batch % ROWS == 0 and seq % TLR == 0 and TLR % CHUNK == 0
    return _layer(x, norm_in_w[0], w_in[0], conv_qkv_w[0], A_log[0], dt_bias[0], gdn_norm_w[0],
                  conv_w[0], conv_b[0], w_out[0], final_norm_w)
```

```python
import jax
import jax.numpy as jnp
from jax import lax
from jax.experimental import pallas as pl
from jax.experimental.pallas import tpu as pltpu

D_MODEL = 1024
HEADS = 8
HEAD_DIM = 128
GDN_WIDTH = HEADS * HEAD_DIM
CONV_WIDTH = 1024
GDN_CONV = 4
SHORT_CONV = 3
CHUNK = 64
EPS = 1e-6
LANES = 128
HALO = 8
ROW_STRIDE = 4

COL_Z = 3 * GDN_WIDTH
COL_CONV = COL_Z + GDN_WIDTH
COL_BETA = COL_CONV + 4 * CONV_WIDTH
COL_A = COL_BETA + LANES
PROJ_PACKED = COL_A + LANES
N_BA = 2 * HEADS
PACK_TN = 1024

ROWS = 4
TLR = 64
TL = ROWS * TLR
VMEM_LIMIT = 60 * 1024 * 1024


def _bdot(a, b):
    return jnp.dot(a.astype(jnp.bfloat16), b.astype(jnp.bfloat16),
                   preferred_element_type=jnp.float32)


def _bdot_nt(a, b):
    return lax.dot_general(a.astype(jnp.bfloat16), b.astype(jnp.bfloat16),
                           (((1,), (1,)), ((), ())), preferred_element_type=jnp.float32)


def _bdot_tn(a, b):
    return lax.dot_general(a.astype(jnp.bfloat16), b.astype(jnp.bfloat16),
                           (((0,), (0,)), ((), ())), preferred_element_type=jnp.float32)


def _silu(x):
    return x * jax.nn.sigmoid(x)


def _rmsnorm(x, w):
    ms = jnp.mean(x * x, axis=-1, keepdims=True)
    return x * lax.rsqrt(ms + EPS) * w


def _fill_slabs(slab_ref, x):
    n = x.shape[1] // LANES
    for b in range(ROWS):
        for j in range(n):
            slab_ref[b * n + j, HALO:, :] = x[b * TLR:(b + 1) * TLR, j * LANES:(j + 1) * LANES]


def _conv_row_classes(slab_ref, j, w_ref, lane0, width):
    taps = [w_ref[width - 1 - s:width - s, lane0:lane0 + LANES] for s in range(width)]
    loaded = {}

    def rows_from(start):
        if start not in loaded:
            loaded[start] = slab_ref[j, pl.ds(start, TLR // ROW_STRIDE, stride=ROW_STRIDE), :]
        return loaded[start]

    out = []
    for r in range(ROW_STRIDE):
        acc = rows_from(HALO + r) * taps[0]
        for s in range(1, width):
            acc = acc + rows_from(HALO + r - s) * taps[s]
        out.append(acc)
    slab_ref[j, :HALO, :] = slab_ref[j, TLR:, :]
    return out


def _store_row_class(dst_ref, j, b, r, val):
    dst_ref[j, pl.ds(b * TLR + r, TLR // ROW_STRIDE, stride=ROW_STRIDE), :] = val


def _block_kernel(x_ref, nw_ref, wp_ref, cw_ref, alog_ref, dtb_ref, gnw_ref, ccw_ref, cb_ref,
                  wo_ref, fnw_ref, out_ref,
                  s_ref, pre_slab, ch_slab, cv_s, q_s, k_s, v_s, z_s, beta_s, gc_s, gct_s, betat_s,
                  u_s, w_s, attn_s, o_s):
    n_chunks = TL // CHUNK
    row_chunks = TLR // CHUNK
    heads = range(HEADS)
    lanes = [slice(h * HEAD_DIM, (h + 1) * HEAD_DIM) for h in heads]
    col = lambda blk, h: blk[:, h:h + 1]
    conv_tiles = CONV_WIDTH // LANES

    @pl.when(pl.program_id(1) == 0)
    def _():
        s_ref[...] = jnp.zeros_like(s_ref)
        pre_slab[:, :, :HALO, :] = jnp.zeros(pre_slab.shape[:2] + (HALO, LANES), jnp.float32)
        ch_slab[:, :HALO, :] = jnp.zeros((ch_slab.shape[0], HALO, LANES), jnp.float32)

    x = x_ref[...].reshape(TL, D_MODEL)
    hn = _rmsnorm(x, nw_ref[...]).astype(jnp.bfloat16)
    proj = lambda lo, hi: jnp.dot(hn, wp_ref[:, lo:hi], preferred_element_type=jnp.float32)
    conv_col = lambda j: proj(COL_CONV + j * CONV_WIDTH, COL_CONV + (j + 1) * CONV_WIDTH)

    ba = proj(COL_BETA, PROJ_PACKED)
    beta = jax.nn.sigmoid(ba[:, :LANES])
    beta_s[...] = beta
    a_in = ba[:, LANES:] + dtb_ref[...]
    softplus = jnp.maximum(a_in, 0.0) + jnp.log1p(jnp.exp(-jnp.abs(a_in)))
    gc = -jnp.exp(alog_ref[...]) * softplus
    pos = lax.broadcasted_iota(jnp.int32, gc.shape, 0) % CHUNK
    s = 1
    while s < CHUNK:
        gc = gc + jnp.where(pos >= s, pltpu.roll(gc, s, axis=0), 0.0)
        s *= 2
    gc_s[...] = gc
    for src, dst in ((gc.T, gct_s), (beta.T, betat_s)):
        for c in range(n_chunks):
            blk = src[:, c * CHUNK:(c + 1) * CHUNK]
            dst[c] = jnp.concatenate([blk, blk], axis=1)

    conv_in = []
    for grp, dst in enumerate((q_s, k_s, v_s)):
        col0 = grp * GDN_WIDTH
        slabs = pre_slab.at[grp]
        _fill_slabs(slabs, proj(col0, col0 + GDN_WIDTH))
        conv_in.append(conv_col(grp))
        scale = HEAD_DIM ** -0.5 if grp == 0 else 1.0
        for b in range(ROWS):
            for h in heads:
                classes = _conv_row_classes(slabs, b * HEADS + h, cw_ref, col0 + h * HEAD_DIM,
                                            GDN_CONV)
                for r, acc in enumerate(classes):
                    act = _silu(acc)
                    if grp < 2:
                        inv = lax.rsqrt(jnp.sum(act * act, axis=-1, keepdims=True) + EPS)
                        act = act * inv * scale
                    _store_row_class(dst, h, b, r, act)
    z_s[...] = proj(COL_Z, COL_CONV)
    gate_c = conv_col(3)

    gate_b, gate_cc, h_c = conv_in
    _fill_slabs(ch_slab, gate_cc * h_c)
    for b in range(ROWS):
        for j in range(conv_tiles):
            classes = _conv_row_classes(ch_slab, b * conv_tiles + j, ccw_ref, j * LANES,
                                        SHORT_CONV)
            for r, acc in enumerate(classes):
                _store_row_class(cv_s, j, b, r, acc)
    conv = jnp.concatenate([cv_s[j] for j in range(conv_tiles)], axis=1)
    y = gate_b * (conv + cb_ref[...]) * _silu(gate_c)
    res = x + _bdot(y, wo_ref[GDN_WIDTH:, :])

    ri = lax.broadcasted_iota(jnp.int32, (CHUNK, 2 * CHUNK), 0)
    ci = lax.broadcasted_iota(jnp.int32, (CHUNK, 2 * CHUNK), 1)
    cj = ci % CHUNK
    causal2 = ri >= cj
    strict2 = ri > cj
    left = ci < CHUNK
    eye_right = jnp.where((ri == cj) & (ci >= CHUNK), 1.0, 0.0)
    keep_right = jnp.where(left, 0.0, 1.0)

    items = []
    for c in range(n_chunks):
        rows = slice(c * CHUNK, (c + 1) * CHUNK)
        gc_blk = gc_s[rows, :]
        eg_blk = jnp.exp(gc_blk)
        gct_blk = gct_s[c]
        betat_blk = betat_s[c]
        for h in heads:
            items.append((rows, h, betat_blk[h:h + 1, :], col(gc_blk, h), col(eg_blk, h),
                          gct_blk[h:h + 1, :]))
    n_items = range(len(items))
    k = [k_s[h, rows, :] for rows, h, *_ in items]
    q = [q_s[h, rows, :] for rows, h, *_ in items]
    decayb = [jnp.where(causal2, jnp.exp(it[3] - it[5]), 0.0) * it[2] for it in items]
    kq = [_bdot_nt(jnp.concatenate([k[n], q[n]], axis=0),
                   jnp.concatenate([k[n], k[n]], axis=0)) for n in n_items]
    for n, it in enumerate(items):
        attn_s[it[1], it[0], :] = kq[n][CHUNK:, :CHUNK] * decayb[n][:, :CHUNK]
        q_s[it[1], it[0], :] = q[n] * it[4]
    a2 = [jnp.where(strict2, kq[n][:CHUNK] * decayb[n], 0.0) for n in n_items]
    prod = [_bdot(a[:, :CHUNK], a) for a in a2]
    pt = [jnp.where(left, prod[n], eye_right - a2[n]) for n in n_items]
    m = 4
    while m < CHUNK:
        prod = [_bdot(p[:, :CHUNK], p) for p in pt]
        pt = [prod[n] + pt[n] * keep_right for n in n_items]
        m *= 2
    prod = [_bdot(p[:, :CHUNK], p) for p in pt]
    t_mat = [(pt[n] + prod[n])[:, CHUNK:] for n in n_items]
    rhs = [jnp.concatenate([v_s[it[1], it[0], :], k[n] * it[4]], axis=1)
           for n, it in enumerate(items)]
    sol = [_bdot(t_mat[n], rhs[n]) for n in n_items]
    for n, it in enumerate(items):
        u_s[it[0], lanes[it[1]]] = sol[n][:, :HEAD_DIM]
        w_s[it[0], lanes[it[1]]] = sol[n][:, HEAD_DIM:]

    chains = [(b, h) for b in range(ROWS) for h in heads]
    state = [s_ref[b * HEADS + h] for b, h in chains]
    for step in range(row_chunks):
        rows = [slice((b * row_chunks + step) * CHUNK, (b * row_chunks + step + 1) * CHUNK)
                for b in range(ROWS)]
        gc_blk = [gc_s[r, :] for r in rows]
        g_last = [g[CHUNK - 1:CHUNK, :] for g in gc_blk]
        ekb_blk = [jnp.exp(g_last[b] - gc_blk[b]) * beta_s[rows[b], :] for b in range(ROWS)]
        egl_blk = [jnp.exp(g) for g in g_last]
        ws_qs = [_bdot(jnp.concatenate([w_s[rows[b], lanes[h]], q_s[h, rows[b], :]], axis=0),
                       state[n]) for n, (b, h) in enumerate(chains)]
        v_new = [u_s[rows[b], lanes[h]] - ws_qs[n][:CHUNK] for n, (b, h) in enumerate(chains)]
        o = [ws_qs[n][CHUNK:] + _bdot(attn_s[h, rows[b], :], v_new[n])
             for n, (b, h) in enumerate(chains)]
        kv = [_bdot_tn(k_s[h, rows[b], :] * col(ekb_blk[b], h), v_new[n])
              for n, (b, h) in enumerate(chains)]
        state = [state[n] * col(egl_blk[b], h) + kv[n] for n, (b, h) in enumerate(chains)]
        for n, (b, h) in enumerate(chains):
            o_s[rows[b], lanes[h]] = (_rmsnorm(o[n], gnw_ref[...])
                                      * _silu(z_s[rows[b], lanes[h]]))
    for n, (b, h) in enumerate(chains):
        s_ref[b * HEADS + h] = state[n]

    res = res + _bdot(o_s[...], wo_ref[:GDN_WIDTH, :])
    out_ref[...] = _rmsnorm(res, fnw_ref[...]).reshape(ROWS, TLR, D_MODEL)


def _block(x, norm_w, w_packed, conv_qkv_w, a_log_pad, dt_bias_pad, gdn_norm_w, conv_w, conv_b,
           w_out, out_norm_w):
    batch, seq, _ = x.shape
    n_chunks = TL // CHUNK
    conv_tiles = CONV_WIDTH // LANES
    tile = lambda g, t: (g, t, 0)
    const = lambda g, t: (0, 0)
    resident = lambda shape: pl.BlockSpec(shape, const, pipeline_mode=pl.Buffered(1))
    tile_f32 = pltpu.VMEM((TL, GDN_WIDTH), jnp.float32)
    head_tile_f32 = pltpu.VMEM((HEADS, TL, HEAD_DIM), jnp.float32)
    return pl.pallas_call(
        _block_kernel,
        grid=(batch // ROWS, seq // TLR),
        in_specs=[
            pl.BlockSpec((ROWS, TLR, D_MODEL), tile),
            resident((1, D_MODEL)),
            resident((D_MODEL, PROJ_PACKED)),
            resident((GDN_CONV, 3 * GDN_WIDTH)),
            resident((1, LANES)),
            resident((1, LANES)),
            resident((1, HEAD_DIM)),
            resident((SHORT_CONV, CONV_WIDTH)),
            resident((1, CONV_WIDTH)),
            resident((GDN_WIDTH + CONV_WIDTH, D_MODEL)),
            resident((1, D_MODEL)),
        ],
        out_specs=pl.BlockSpec((ROWS, TLR, D_MODEL), tile),
        out_shape=jax.ShapeDtypeStruct(x.shape, jnp.float32),
        scratch_shapes=[
            pltpu.VMEM((ROWS * HEADS, HEAD_DIM, HEAD_DIM), jnp.float32),
            pltpu.VMEM((3, ROWS * HEADS, HALO + TLR, LANES), jnp.float32),
            pltpu.VMEM((ROWS * conv_tiles, HALO + TLR, LANES), jnp.float32),
            pltpu.VMEM((conv_tiles, TL, LANES), jnp.float32),
            head_tile_f32, head_tile_f32, head_tile_f32,
            tile_f32,
            pltpu.VMEM((TL, LANES), jnp.float32),
            pltpu.VMEM((TL, LANES), jnp.float32),
            pltpu.VMEM((n_chunks, LANES, 2 * CHUNK), jnp.float32),
            pltpu.VMEM((n_chunks, LANES, 2 * CHUNK), jnp.float32),
            tile_f32, tile_f32,
            pltpu.VMEM((HEADS, TL, CHUNK), jnp.float32),
            tile_f32,
        ],
        compiler_params=pltpu.CompilerParams(
            dimension_semantics=("arbitrary", "arbitrary"), vmem_limit_bytes=VMEM_LIMIT),
        name="hybrid_block",
    )(x, norm_w, w_packed, conv_qkv_w, a_log_pad, dt_bias_pad, gdn_norm_w, conv_w, conv_b,
      w_out, out_norm_w)


def _pad_lanes(v):
    return jnp.pad(v, ((0, 0), (0, LANES - v.shape[-1])))


def _pack_kernel(a_ref, m_ref, o_ref):
    j = pl.program_id(0)

    @pl.when(j < COL_CONV // PACK_TN)
    def _():
        o_ref[...] = a_ref[...].T.astype(jnp.bfloat16)

    @pl.when((j >= COL_CONV // PACK_TN) & (j < COL_BETA // PACK_TN))
    def _():
        rows = jnp.concatenate([a_ref[N_BA:, :], m_ref[...]], axis=0)
        o_ref[...] = rows.T.astype(jnp.bfloat16)

    @pl.when(j == COL_BETA // PACK_TN)
    def _():
        pad = jnp.zeros((LANES - HEADS, D_MODEL), jnp.float32)
        rest = jnp.zeros((PACK_TN - 2 * LANES, D_MODEL), jnp.float32)
        rows = jnp.concatenate([m_ref[:HEADS, :], pad, m_ref[HEADS:, :], pad, rest], axis=0)
        o_ref[...] = rows.T.astype(jnp.bfloat16)


def _pack_weights(wt):
    last = COL_BETA // PACK_TN
    mini_per_block = PACK_TN // N_BA
    mini_index = lambda j: jnp.where(j == last, COL_CONV // N_BA, (j + 1) * mini_per_block)
    return pl.pallas_call(
        _pack_kernel,
        grid=(pl.cdiv(PROJ_PACKED, PACK_TN),),
        in_specs=[
            pl.BlockSpec((PACK_TN, D_MODEL), lambda j: (jnp.minimum(j, last - 1), 0)),
            pl.BlockSpec((N_BA, D_MODEL), lambda j: (mini_index(j), 0)),
        ],
        out_specs=pl.BlockSpec((D_MODEL, PACK_TN), lambda j: (0, j)),
        out_shape=jax.ShapeDtypeStruct((D_MODEL, PROJ_PACKED), jnp.bfloat16),
        compiler_params=pltpu.CompilerParams(
            dimension_semantics=("arbitrary",), vmem_limit_bytes=VMEM_LIMIT),
        name="pack_weights",
    )(wt, wt)


def _layer(x, norm_in_w, w_in, conv_qkv_w, a_log, dt_bias, gdn_norm_w, conv_w, conv_b, w_out,
           out_norm_w):
    assert w_in.shape[1] == COL_BETA + N_BA
    w_packed = _pack_weights(w_in.T)
    return _block(x, norm_in_w.reshape(1, -1), w_packed, conv_qkv_w,
                  _pad_lanes(a_log.reshape(1, -1)), _pad_lanes(dt_bias.reshape(1, -1)),
                  gdn_norm_w.reshape(1, -1), conv_w, conv_b.reshape(1, -1),
                  w_out.astype(jnp.bfloat16), out_norm_w.reshape(1, -1))


def kernel(x, norm_in_w, w_in, conv_qkv_w, A_log, dt_bias, gdn_norm_w, conv_w, conv_b, w_out,
           final_norm_w):
    batch, seq, d_model = x.shape
    assert norm_in_w.shape[0] == 1 and d_model == D_MODEL
    assert batch % ROWS == 0 and seq % TLR == 0 and TLR % CHUNK == 0
    return _layer(x, norm_in_w[0], w_in[0], conv_qkv_w[0], A_log[0], dt_bias[0], gdn_norm_w[0],
                  conv_w[0], conv_b[0], w_out[0], final_norm_w)
```

```python
import jax
import jax.numpy as jnp
from jax import lax
from jax.experimental import pallas as pl
from jax.experimental.pallas import tpu as pltpu

D_MODEL = 1024
HEADS = 8
HEAD_DIM = 128
GDN_WIDTH = HEADS * HEAD_DIM
CONV_WIDTH = 1024
GDN_CONV = 4
SHORT_CONV = 3
CHUNK = 64
EPS = 1e-6
LANES = 128
HALO = 8
ROW_STRIDE = 4

COL_Z = 3 * GDN_WIDTH
COL_CONV = COL_Z + GDN_WIDTH
COL_BETA = COL_CONV + 4 * CONV_WIDTH
COL_A = COL_BETA + LANES
PROJ_PACKED = COL_A + LANES
N_BA = 2 * HEADS
PACK_TN = 1024

ROWS = 4
TLR = 64
TL = ROWS * TLR
VMEM_LIMIT = 60 * 1024 * 1024


def _bdot(a, b):
    return jnp.dot(a.astype(jnp.bfloat16), b.astype(jnp.bfloat16),
                   preferred_element_type=jnp.float32)


def _bdot_nt(a, b):
    return lax.dot_general(a.astype(jnp.bfloat16), b.astype(jnp.bfloat16),
                           (((1,), (1,)), ((), ())), preferred_element_type=jnp.float32)


def _bdot_tn(a, b):
    return lax.dot_general(a.astype(jnp.bfloat16), b.astype(jnp.bfloat16),
                           (((0,), (0,)), ((), ())), preferred_element_type=jnp.float32)


def _silu(x):
    return x * jax.nn.sigmoid(x)


def _rmsnorm(x, w):
    ms = jnp.mean(x * x, axis=-1, keepdims=True)
    return x * lax.rsqrt(ms + EPS) * w


def _fill_slabs(slab_ref, x):
    n = x.shape[1] // LANES
    for b in range(ROWS):
        for j in range(n):
            slab_ref[b * n + j, HALO:, :] = x[b * TLR:(b + 1) * TLR, j * LANES:(j + 1) * LANES]


def _conv_row_classes(slab_ref, j, w_ref, lane0, width):
    taps = [w_ref[width - 1 - s:width - s, lane0:lane0 + LANES] for s in range(width)]
    loaded = {}

    def rows_from(start):
        if start not in loaded:
            loaded[start] = slab_ref[j, pl.ds(start, TLR // ROW_STRIDE, stride=ROW_STRIDE), :]
        return loaded[start]

    out = []
    for r in range(ROW_STRIDE):
        acc = rows_from(HALO + r) * taps[0]
        for s in range(1, width):
            acc = acc + rows_from(HALO + r - s) * taps[s]
        out.append(acc)
    slab_ref[j, :HALO, :] = slab_ref[j, TLR:, :]
    return out


def _store_row_class(dst_ref, j, b, r, val):
    dst_ref[j, pl.ds(b * TLR + r, TLR // ROW_STRIDE, stride=ROW_STRIDE), :] = val


def _block_kernel(x_ref, nw_ref, wp_ref, cw_ref, alog_ref, dtb_ref, gnw_ref, ccw_ref, cb_ref,
                  wo_ref, fnw_ref, out_ref,
                  s_ref, pre_slab, ch_slab, cv_s, q_s, k_s, v_s, z_s, beta_s, gc_s, gct_s, betat_s,
                  u_s, w_s, attn_s, o_s):
    n_chunks = TL // CHUNK
    row_chunks = TLR // CHUNK
    heads = range(HEADS)
    lanes = [slice(h * HEAD_DIM, (h + 1) * HEAD_DIM) for h in heads]
    col = lambda blk, h: blk[:, h:h + 1]
    conv_tiles = CONV_WIDTH // LANES

    @pl.when(pl.program_id(1) == 0)
    def _():
        s_ref[...] = jnp.zeros_like(s_ref)
        pre_slab[:, :, :HALO, :] = jnp.zeros(pre_slab.shape[:2] + (HALO, LANES), jnp.float32)
        ch_slab[:, :HALO, :] = jnp.zeros((ch_slab.shape[0], HALO, LANES), jnp.float32)

    x = x_ref[...].reshape(TL, D_MODEL)
    hn = _rmsnorm(x, nw_ref[...]).astype(jnp.bfloat16)
    proj = lambda lo, hi: jnp.dot(hn, wp_ref[:, lo:hi], preferred_element_type=jnp.float32)
    conv_col = lambda j: proj(COL_CONV + j * CONV_WIDTH, COL_CONV + (j + 1) * CONV_WIDTH)

    ba = proj(COL_BETA, PROJ_PACKED)
    beta = jax.nn.sigmoid(ba[:, :LANES])
    beta_s[...] = beta
    a_in = ba[:, LANES:] + dtb_ref[...]
    softplus = jnp.maximum(a_in, 0.0) + jnp.log1p(jnp.exp(-jnp.abs(a_in)))
    gc = -jnp.exp(alog_ref[...]) * softplus
    pos = lax.broadcasted_iota(jnp.int32, gc.shape, 0) % CHUNK
    s = 1
    while s < CHUNK:
        gc = gc + jnp.where(pos >= s, pltpu.roll(gc, s, axis=0), 0.0)
        s *= 2
    gc_s[...] = gc
    for src, dst in ((gc.T, gct_s), (beta.T, betat_s)):
        for c in range(n_chunks):
            blk = src[:, c * CHUNK:(c + 1) * CHUNK]
            dst[c] = jnp.concatenate([blk, blk], axis=1)

    conv_in = []
    for grp, dst in enumerate((q_s, k_s, v_s)):
        col0 = grp * GDN_WIDTH
        slabs = pre_slab.at[grp]
        _fill_slabs(slabs, proj(col0, col0 + GDN_WIDTH))
        conv_in.append(conv_col(grp))
        scale = HEAD_DIM ** -0.5 if grp == 0 else 1.0
        for b in range(ROWS):
            for h in heads:
                classes = _conv_row_classes(slabs, b * HEADS + h, cw_ref, col0 + h * HEAD_DIM,
                                            GDN_CONV)
                for r, acc in enumerate(classes):
                    act = _silu(acc)
                    if grp < 2:
                        inv = lax.rsqrt(jnp.sum(act * act, axis=-1, keepdims=True) + EPS)
                        act = act * inv * scale
                    _store_row_class(dst, h, b, r, act)
    z_s[...] = proj(COL_Z, COL_CONV)
    gate_c = conv_col(3)

    gate_b, gate_cc, h_c = conv_in
    _fill_slabs(ch_slab, gate_cc * h_c)
    for b in range(ROWS):
        for j in range(conv_tiles):
            classes = _conv_row_classes(ch_slab, b * conv_tiles + j, ccw_ref, j * LANES,
                                        SHORT_CONV)
            for r, acc in enumerate(classes):
                _store_row_class(cv_s, j, b, r, acc)
    conv = jnp.concatenate([cv_s[j] for j in range(conv_tiles)], axis=1)
    y = (gate_b * (conv + cb_ref[...]) * _silu(gate_c)).astype(jnp.bfloat16)

    ri = lax.broadcasted_iota(jnp.int32, (CHUNK, 2 * CHUNK), 0)
    ci = lax.broadcasted_iota(jnp.int32, (CHUNK, 2 * CHUNK), 1)
    cj = ci % CHUNK
    causal2 = ri >= cj
    strict2 = ri > cj
    left = ci < CHUNK
    eye_right = jnp.where((ri == cj) & (ci >= CHUNK), 1.0, 0.0)
    keep_right = jnp.where(left, 0.0, 1.0)

    items = []
    for c in range(n_chunks):
        rows = slice(c * CHUNK, (c + 1) * CHUNK)
        gc_blk = gc_s[rows, :]
        eg_blk = jnp.exp(gc_blk)
        gct_blk = gct_s[c]
        betat_blk = betat_s[c]
        for h in heads:
            items.append((rows, h, betat_blk[h:h + 1, :], col(gc_blk, h), col(eg_blk, h),
                          gct_blk[h:h + 1, :]))
    n_items = range(len(items))
    k = [k_s[h, rows, :] for rows, h, *_ in items]
    q = [q_s[h, rows, :] for rows, h, *_ in items]
    decayb = [jnp.where(causal2, jnp.exp(it[3] - it[5]), 0.0) * it[2] for it in items]
    kq = [_bdot_nt(jnp.concatenate([k[n], q[n]], axis=0),
                   jnp.concatenate([k[n], k[n]], axis=0)) for n in n_items]
    for n, it in enumerate(items):
        attn_s[it[1], it[0], :] = kq[n][CHUNK:, :CHUNK] * decayb[n][:, :CHUNK]
        q_s[it[1], it[0], :] = q[n] * it[4]
    a2 = [jnp.where(strict2, kq[n][:CHUNK] * decayb[n], 0.0) for n in n_items]
    prod = [_bdot(a[:, :CHUNK], a) for a in a2]
    pt = [jnp.where(left, prod[n], eye_right - a2[n]) for n in n_items]
    m = 4
    while m < CHUNK:
        prod = [_bdot(p[:, :CHUNK], p) for p in pt]
        pt = [prod[n] + pt[n] * keep_right for n in n_items]
        m *= 2
    prod = [_bdot(p[:, :CHUNK], p) for p in pt]
    t_mat = [(pt[n] + prod[n])[:, CHUNK:] for n in n_items]
    rhs = [jnp.concatenate([v_s[it[1], it[0], :], k[n] * it[4]], axis=1)
           for n, it in enumerate(items)]
    sol = [_bdot(t_mat[n], rhs[n]) for n in n_items]
    for n, it in enumerate(items):
        u_s[it[0], lanes[it[1]]] = sol[n][:, :HEAD_DIM]
        w_s[it[0], lanes[it[1]]] = sol[n][:, HEAD_DIM:]

    chains = [(b, h) for b in range(ROWS) for h in heads]
    state = [s_ref[b * HEADS + h] for b, h in chains]
    for step in range(row_chunks):
        rows = [slice((b * row_chunks + step) * CHUNK, (b * row_chunks + step + 1) * CHUNK)
                for b in range(ROWS)]
        gc_blk = [gc_s[r, :] for r in rows]
        g_last = [g[CHUNK - 1:CHUNK, :] for g in gc_blk]
        ekb_blk = [jnp.exp(g_last[b] - gc_blk[b]) * beta_s[rows[b], :] for b in range(ROWS)]
        egl_blk = [jnp.exp(g) for g in g_last]
        ws_qs = [_bdot(jnp.concatenate([w_s[rows[b], lanes[h]], q_s[h, rows[b], :]], axis=0),
                       state[n]) for n, (b, h) in enumerate(chains)]
        v_new = [u_s[rows[b], lanes[h]] - ws_qs[n][:CHUNK] for n, (b, h) in enumerate(chains)]
        o = [ws_qs[n][CHUNK:] + _bdot(attn_s[h, rows[b], :], v_new[n])
             for n, (b, h) in enumerate(chains)]
        kv = [_bdot_tn(k_s[h, rows[b], :] * col(ekb_blk[b], h), v_new[n])
              for n, (b, h) in enumerate(chains)]
        state = [state[n] * col(egl_blk[b], h) + kv[n] for n, (b, h) in enumerate(chains)]
        for n, (b, h) in enumerate(chains):
            o_s[rows[b], lanes[h]] = o[n]
    for n, (b, h) in enumerate(chains):
        s_ref[b * HEADS + h] = state[n]

    res = x + jnp.dot(y, wo_ref[GDN_WIDTH:, :], preferred_element_type=jnp.float32)
    gated = jnp.concatenate(
        [_rmsnorm(o_s[:, lanes[h]], gnw_ref[...]) * _silu(z_s[:, lanes[h]]) for h in heads],
        axis=1)

    res = res + _bdot(gated, wo_ref[:GDN_WIDTH, :])
    out_ref[...] = _rmsnorm(res, fnw_ref[...]).reshape(ROWS, TLR, D_MODEL)


def _block(x, norm_w, w_packed, conv_qkv_w, a_log_pad, dt_bias_pad, gdn_norm_w, conv_w, conv_b,
           w_out, out_norm_w):
    batch, seq, _ = x.shape
    n_chunks = TL // CHUNK
    conv_tiles = CONV_WIDTH // LANES
    tile = lambda g, t: (g, t, 0)
    const = lambda g, t: (0, 0)
    resident = lambda shape: pl.BlockSpec(shape, const, pipeline_mode=pl.Buffered(1))
    tile_f32 = pltpu.VMEM((TL, GDN_WIDTH), jnp.float32)
    head_tile_f32 = pltpu.VMEM((HEADS, TL, HEAD_DIM), jnp.float32)
    return pl.pallas_call(
        _block_kernel,
        grid=(batch // ROWS, seq // TLR),
        in_specs=[
            pl.BlockSpec((ROWS, TLR, D_MODEL), tile),
            resident((1, D_MODEL)),
            resident((D_MODEL, PROJ_PACKED)),
            resident((GDN_CONV, 3 * GDN_WIDTH)),
            resident((1, LANES)),
            resident((1, LANES)),
            resident((1, HEAD_DIM)),
            resident((SHORT_CONV, CONV_WIDTH)),
            resident((1, CONV_WIDTH)),
            resident((GDN_WIDTH + CONV_WIDTH, D_MODEL)),
            resident((1, D_MODEL)),
        ],
        out_specs=pl.BlockSpec((ROWS, TLR, D_MODEL), tile),
        out_shape=jax.ShapeDtypeStruct(x.shape, jnp.float32),
        scratch_shapes=[
            pltpu.VMEM((ROWS * HEADS, HEAD_DIM, HEAD_DIM), jnp.float32),
            pltpu.VMEM((3, ROWS * HEADS, HALO + TLR, LANES), jnp.float32),
            pltpu.VMEM((ROWS * conv_tiles, HALO + TLR, LANES), jnp.float32),
            pltpu.VMEM((conv_tiles, TL, LANES), jnp.float32),
            head_tile_f32, head_tile_f32, head_tile_f32,
            tile_f32,
            pltpu.VMEM((TL, LANES), jnp.float32),
            pltpu.VMEM((TL, LANES), jnp.float32),
            pltpu.VMEM((n_chunks, LANES, 2 * CHUNK), jnp.float32),
            pltpu.VMEM((n_chunks, LANES, 2 * CHUNK), jnp.float32),
            tile_f32, tile_f32,
            pltpu.VMEM((HEADS, TL, CHUNK), jnp.float32),
            tile_f32,
        ],
        compiler_params=pltpu.CompilerParams(
            dimension_semantics=("arbitrary", "arbitrary"), vmem_limit_bytes=VMEM_LIMIT),
        name="hybrid_block",
    )(x, norm_w, w_packed, conv_qkv_w, a_log_pad, dt_bias_pad, gdn_norm_w, conv_w, conv_b,
      w_out, out_norm_w)


def _pad_lanes(v):
    return jnp.pad(v, ((0, 0), (0, LANES - v.shape[-1])))


def _pack_kernel(a_ref, m_ref, o_ref):
    j = pl.program_id(0)

    @pl.when(j < COL_CONV // PACK_TN)
    def _():
        o_ref[...] = a_ref[...].T.astype(jnp.bfloat16)

    @pl.when((j >= COL_CONV // PACK_TN) & (j < COL_BETA // PACK_TN))
    def _():
        rows = jnp.concatenate([a_ref[N_BA:, :], m_ref[...]], axis=0)
        o_ref[...] = rows.T.astype(jnp.bfloat16)

    @pl.when(j == COL_BETA // PACK_TN)
    def _():
        pad = jnp.zeros((LANES - HEADS, D_MODEL), jnp.float32)
        rest = jnp.zeros((PACK_TN - 2 * LANES, D_MODEL), jnp.float32)
        rows = jnp.concatenate([m_ref[:HEADS, :], pad, m_ref[HEADS:, :], pad, rest], axis=0)
        o_ref[...] = rows.T.astype(jnp.bfloat16)


def _pack_weights(wt):
    last = COL_BETA // PACK_TN
    mini_per_block = PACK_TN // N_BA
    mini_index = lambda j: jnp.where(j == last, COL_CONV // N_BA, (j + 1) * mini_per_block)
    return pl.pallas_call(
        _pack_kernel,
        grid=(pl.cdiv(PROJ_PACKED, PACK_TN),),
        in_specs=[
            pl.BlockSpec((PACK_TN, D_MODEL), lambda j: (jnp.minimum(j, last - 1), 0)),
            pl.BlockSpec((N_BA, D_MODEL), lambda j: (mini_index(j), 0)),
        ],
        out_specs=pl.BlockSpec((D_MODEL, PACK_TN), lambda j: (0, j)),
        out_shape=jax.ShapeDtypeStruct((D_MODEL, PROJ_PACKED), jnp.bfloat16),
        compiler_params=pltpu.CompilerParams(
            dimension_semantics=("arbitrary",), vmem_limit_bytes=VMEM_LIMIT),
        name="pack_weights",
    )(wt, wt)


def _layer(x, norm_in_w, w_in, conv_qkv_w, a_log, dt_bias, gdn_norm_w, conv_w, conv_b, w_out,
           out_norm_w):
    assert w_in.shape[1] == COL_BETA + N_BA
    w_packed = _pack_weights(w_in.T)
    return _block(x, norm_in_w.reshape(1, -1), w_packed, conv_qkv_w,
                  _pad_lanes(a_log.reshape(1, -1)), _pad_lanes(dt_bias.reshape(1, -1)),
                  gdn_norm_w.reshape(1, -1), conv_w, conv_b.reshape(1, -1),
                  w_out.astype(jnp.bfloat16), out_norm_w.reshape(1, -1))


def kernel(x, norm_in_w, w_in, conv_qkv_w, A_log, dt_bias, gdn_norm_w, conv_w, conv_b, w_out,
           final_norm_w):
    batch, seq, d_model = x.shape
    assert norm_in_w.shape[0] == 1 and d_model == D_MODEL
    assert batch % ROWS == 0 and seq % TLR == 0 and TLR % CHUNK == 0
    return _layer(x, norm_in_w[0], w_in[0], conv_qkv_w[0], A_log[0], dt_bias[0], gdn_norm_w[0],
                  conv_w[0], conv_b[0], w_out[0], final_norm_w)
```

```python
import jax
import jax.numpy as jnp
from jax import lax
from jax.experimental import pallas as pl
from jax.experimental.pallas import tpu as pltpu

D_MODEL = 1024
HEADS = 8
HEAD_DIM = 128
GDN_WIDTH = HEADS * HEAD_DIM
CONV_WIDTH = 1024
GDN_CONV = 4
SHORT_CONV = 3
CHUNK = 64
EPS = 1e-6
LANES = 128
HALO = 8
ROW_STRIDE = 4

COL_Z = 3 * GDN_WIDTH
COL_CONV = COL_Z + GDN_WIDTH
COL_BETA = COL_CONV + 4 * CONV_WIDTH
COL_A = COL_BETA + LANES
PROJ_PACKED = COL_A + LANES
N_BA = 2 * HEADS
PACK_TN = 1024

ROWS = 4
TLR = 64
TL = ROWS * TLR
VMEM_LIMIT = 60 * 1024 * 1024


def _bdot(a, b):
    return jnp.dot(a.astype(jnp.bfloat16), b.astype(jnp.bfloat16),
                   preferred_element_type=jnp.float32)


def _bdot_nt(a, b):
    return lax.dot_general(a.astype(jnp.bfloat16), b.astype(jnp.bfloat16),
                           (((1,), (1,)), ((), ())), preferred_element_type=jnp.float32)


def _bdot_tn(a, b):
    return lax.dot_general(a.astype(jnp.bfloat16), b.astype(jnp.bfloat16),
                           (((0,), (0,)), ((), ())), preferred_element_type=jnp.float32)


def _silu(x):
    return x * jax.nn.sigmoid(x)


def _rmsnorm(x, w):
    ms = jnp.mean(x * x, axis=-1, keepdims=True)
    return x * lax.rsqrt(ms + EPS) * w


def _fill_slabs(slab_ref, x):
    n = x.shape[1] // LANES
    for b in range(ROWS):
        for j in range(n):
            slab_ref[b * n + j, HALO:, :] = x[b * TLR:(b + 1) * TLR, j * LANES:(j + 1) * LANES]


def _conv_row_classes(slab_ref, j, w_ref, lane0, width):
    taps = [w_ref[width - 1 - s:width - s, lane0:lane0 + LANES] for s in range(width)]
    loaded = {}

    def rows_from(start):
        if start not in loaded:
            loaded[start] = slab_ref[j, pl.ds(start, TLR // ROW_STRIDE, stride=ROW_STRIDE), :]
        return loaded[start]

    out = []
    for r in range(ROW_STRIDE):
        acc = rows_from(HALO + r) * taps[0]
        for s in range(1, width):
            acc = acc + rows_from(HALO + r - s) * taps[s]
        out.append(acc)
    slab_ref[j, :HALO, :] = slab_ref[j, TLR:, :]
    return out


def _store_row_class(dst_ref, j, b, r, val):
    dst_ref[j, pl.ds(b * TLR + r, TLR // ROW_STRIDE, stride=ROW_STRIDE), :] = val


def _block_kernel(x_ref, nw_ref, wp_ref, cw_ref, alog_ref, dtb_ref, gnw_ref, ccw_ref, cb_ref,
                  wo_ref, fnw_ref, out_ref,
                  s_ref, pre_slab, ch_slab, cv_s, q_s, k_s, v_s, z_s, beta_s, gc_s, gct_s, betat_s,
                  o_s):
    n_chunks = TL // CHUNK
    row_chunks = TLR // CHUNK
    heads = range(HEADS)
    lanes = [slice(h * HEAD_DIM, (h + 1) * HEAD_DIM) for h in heads]
    col = lambda blk, h: blk[:, h:h + 1]
    conv_tiles = CONV_WIDTH // LANES

    @pl.when(pl.program_id(1) == 0)
    def _():
        s_ref[...] = jnp.zeros_like(s_ref)
        pre_slab[:, :, :HALO, :] = jnp.zeros(pre_slab.shape[:2] + (HALO, LANES), jnp.float32)
        ch_slab[:, :HALO, :] = jnp.zeros((ch_slab.shape[0], HALO, LANES), jnp.float32)

    x = x_ref[...].reshape(TL, D_MODEL)
    hn = _rmsnorm(x, nw_ref[...]).astype(jnp.bfloat16)
    proj = lambda lo, hi: jnp.dot(hn, wp_ref[:, lo:hi], preferred_element_type=jnp.float32)
    conv_col = lambda j: proj(COL_CONV + j * CONV_WIDTH, COL_CONV + (j + 1) * CONV_WIDTH)

    ba = proj(COL_BETA, PROJ_PACKED)
    beta = jax.nn.sigmoid(ba[:, :LANES])
    beta_s[...] = beta
    a_in = ba[:, LANES:] + dtb_ref[...]
    softplus = jnp.maximum(a_in, 0.0) + jnp.log1p(jnp.exp(-jnp.abs(a_in)))
    gc = -jnp.exp(alog_ref[...]) * softplus
    pos = lax.broadcasted_iota(jnp.int32, gc.shape, 0) % CHUNK
    s = 1
    while s < CHUNK:
        gc = gc + jnp.where(pos >= s, pltpu.roll(gc, s, axis=0), 0.0)
        s *= 2
    gc_s[...] = gc
    for src, dst in ((gc.T, gct_s), (beta.T, betat_s)):
        for c in range(n_chunks):
            blk = src[:, c * CHUNK:(c + 1) * CHUNK]
            dst[c] = jnp.concatenate([blk, blk], axis=1)

    conv_in = []
    for grp, dst in enumerate((q_s, k_s, v_s)):
        col0 = grp * GDN_WIDTH
        slabs = pre_slab.at[grp]
        _fill_slabs(slabs, proj(col0, col0 + GDN_WIDTH))
        conv_in.append(conv_col(grp))
        scale = HEAD_DIM ** -0.5 if grp == 0 else 1.0
        for b in range(ROWS):
            for h in heads:
                classes = _conv_row_classes(slabs, b * HEADS + h, cw_ref, col0 + h * HEAD_DIM,
                                            GDN_CONV)
                for r, acc in enumerate(classes):
                    act = _silu(acc)
                    if grp < 2:
                        inv = lax.rsqrt(jnp.sum(act * act, axis=-1, keepdims=True) + EPS)
                        act = act * inv * scale
                    _store_row_class(dst, h, b, r, act)
    z_s[...] = proj(COL_Z, COL_CONV)
    gate_c = conv_col(3)

    gate_b, gate_cc, h_c = conv_in
    _fill_slabs(ch_slab, gate_cc * h_c)
    for b in range(ROWS):
        for j in range(conv_tiles):
            classes = _conv_row_classes(ch_slab, b * conv_tiles + j, ccw_ref, j * LANES,
                                        SHORT_CONV)
            for r, acc in enumerate(classes):
                _store_row_class(cv_s, j, b, r, acc)
    conv = jnp.concatenate([cv_s[j] for j in range(conv_tiles)], axis=1)
    y = (gate_b * (conv + cb_ref[...]) * _silu(gate_c)).astype(jnp.bfloat16)

    ri = lax.broadcasted_iota(jnp.int32, (CHUNK, 2 * CHUNK), 0)
    ci = lax.broadcasted_iota(jnp.int32, (CHUNK, 2 * CHUNK), 1)
    cj = ci % CHUNK
    causal2 = ri >= cj
    strict2 = ri > cj
    left = ci < CHUNK
    eye_right = jnp.where((ri == cj) & (ci >= CHUNK), 1.0, 0.0)
    keep_right = jnp.where(left, 0.0, 1.0)

    items = []
    for c in range(n_chunks):
        rows = slice(c * CHUNK, (c + 1) * CHUNK)
        gc_blk = gc_s[rows, :]
        eg_blk = jnp.exp(gc_blk)
        gct_blk = gct_s[c]
        betat_blk = betat_s[c]
        for h in heads:
            items.append((rows, h, betat_blk[h:h + 1, :], col(gc_blk, h), col(eg_blk, h),
                          gct_blk[h:h + 1, :]))
    n_items = range(len(items))
    k = [k_s[h, rows, :] for rows, h, *_ in items]
    q = [q_s[h, rows, :] for rows, h, *_ in items]
    decayb = [jnp.where(causal2, jnp.exp(it[3] - it[5]), 0.0) * it[2] for it in items]
    kq = [_bdot_nt(jnp.concatenate([k[n], q[n]], axis=0),
                   jnp.concatenate([k[n], k[n]], axis=0)) for n in n_items]
    attn_d = [kq[n][CHUNK:, :CHUNK] * decayb[n][:, :CHUNK] for n in n_items]
    qd = [q[n] * items[n][4] for n in n_items]
    a2 = [jnp.where(strict2, kq[n][:CHUNK] * decayb[n], 0.0) for n in n_items]
    prod = [_bdot(a[:, :CHUNK], a) for a in a2]
    pt = [jnp.where(left, prod[n], eye_right - a2[n]) for n in n_items]
    m = 4
    while m < CHUNK:
        prod = [_bdot(p[:, :CHUNK], p) for p in pt]
        pt = [prod[n] + pt[n] * keep_right for n in n_items]
        m *= 2
    prod = [_bdot(p[:, :CHUNK], p) for p in pt]
    t_mat = [(pt[n] + prod[n])[:, CHUNK:] for n in n_items]
    rhs = [jnp.concatenate([v_s[it[1], it[0], :], k[n] * it[4]], axis=1)
           for n, it in enumerate(items)]
    sol = [_bdot(t_mat[n], rhs[n]) for n in n_items]

    chains = [(b, h) for b in range(ROWS) for h in heads]
    state = [s_ref[b * HEADS + h] for b, h in chains]
    for step in range(row_chunks):
        item = [(b * row_chunks + step) * HEADS + h for b, h in chains]
        rows = [slice((b * row_chunks + step) * CHUNK, (b * row_chunks + step + 1) * CHUNK)
                for b in range(ROWS)]
        gc_blk = [gc_s[r, :] for r in rows]
        g_last = [g[CHUNK - 1:CHUNK, :] for g in gc_blk]
        ekb_blk = [jnp.exp(g_last[b] - gc_blk[b]) * beta_s[rows[b], :] for b in range(ROWS)]
        egl_blk = [jnp.exp(g) for g in g_last]
        ws_qs = [_bdot(jnp.concatenate([sol[item[n]][:, HEAD_DIM:], qd[item[n]]], axis=0),
                       state[n]) for n in range(len(chains))]
        v_new = [sol[item[n]][:, :HEAD_DIM] - ws_qs[n][:CHUNK] for n in range(len(chains))]
        o = [ws_qs[n][CHUNK:] + _bdot(attn_d[item[n]], v_new[n]) for n in range(len(chains))]
        kv = [_bdot_tn(k[item[n]] * col(ekb_blk[b], h), v_new[n])
              for n, (b, h) in enumerate(chains)]
        state = [state[n] * col(egl_blk[b], h) + kv[n] for n, (b, h) in enumerate(chains)]
        for n, (b, h) in enumerate(chains):
            o_s[rows[b], lanes[h]] = o[n]
    for n, (b, h) in enumerate(chains):
        s_ref[b * HEADS + h] = state[n]

    res = x + jnp.dot(y, wo_ref[GDN_WIDTH:, :], preferred_element_type=jnp.float32)
    gated = jnp.concatenate(
        [_rmsnorm(o_s[:, lanes[h]], gnw_ref[...]) * _silu(z_s[:, lanes[h]]) for h in heads],
        axis=1)

    res = res + _bdot(gated, wo_ref[:GDN_WIDTH, :])
    out_ref[...] = _rmsnorm(res, fnw_ref[...]).reshape(ROWS, TLR, D_MODEL)


def _block(x, norm_w, w_packed, conv_qkv_w, a_log_pad, dt_bias_pad, gdn_norm_w, conv_w, conv_b,
           w_out, out_norm_w):
    batch, seq, _ = x.shape
    n_chunks = TL // CHUNK
    conv_tiles = CONV_WIDTH // LANES
    tile = lambda g, t: (g, t, 0)
    const = lambda g, t: (0, 0)
    resident = lambda shape: pl.BlockSpec(shape, const, pipeline_mode=pl.Buffered(1))
    tile_f32 = pltpu.VMEM((TL, GDN_WIDTH), jnp.float32)
    head_tile_f32 = pltpu.VMEM((HEADS, TL, HEAD_DIM), jnp.float32)
    return pl.pallas_call(
        _block_kernel,
        grid=(batch // ROWS, seq // TLR),
        in_specs=[
            pl.BlockSpec((ROWS, TLR, D_MODEL), tile),
            resident((1, D_MODEL)),
            resident((D_MODEL, PROJ_PACKED)),
            resident((GDN_CONV, 3 * GDN_WIDTH)),
            resident((1, LANES)),
            resident((1, LANES)),
            resident((1, HEAD_DIM)),
            resident((SHORT_CONV, CONV_WIDTH)),
            resident((1, CONV_WIDTH)),
            resident((GDN_WIDTH + CONV_WIDTH, D_MODEL)),
            resident((1, D_MODEL)),
        ],
        out_specs=pl.BlockSpec((ROWS, TLR, D_MODEL), tile),
        out_shape=jax.ShapeDtypeStruct(x.shape, jnp.float32),
        scratch_shapes=[
            pltpu.VMEM((ROWS * HEADS, HEAD_DIM, HEAD_DIM), jnp.float32),
            pltpu.VMEM((3, ROWS * HEADS, HALO + TLR, LANES), jnp.float32),
            pltpu.VMEM((ROWS * conv_tiles, HALO + TLR, LANES), jnp.float32),
            pltpu.VMEM((conv_tiles, TL, LANES), jnp.float32),
            head_tile_f32, head_tile_f32, head_tile_f32,
            tile_f32,
            pltpu.VMEM((TL, LANES), jnp.float32),
            pltpu.VMEM((TL, LANES), jnp.float32),
            pltpu.VMEM((n_chunks, LANES, 2 * CHUNK), jnp.float32),
            pltpu.VMEM((n_chunks, LANES, 2 * CHUNK), jnp.float32),
            tile_f32,
        ],
        compiler_params=pltpu.CompilerParams(
            dimension_semantics=("arbitrary", "arbitrary"), vmem_limit_bytes=VMEM_LIMIT),
        name="hybrid_block",
    )(x, norm_w, w_packed, conv_qkv_w, a_log_pad, dt_bias_pad, gdn_norm_w, conv_w, conv_b,
      w_out, out_norm_w)


def _pad_lanes(v):
    return jnp.pad(v, ((0, 0), (0, LANES - v.shape[-1])))


def _pack_kernel(a_ref, m_ref, o_ref):
    j = pl.program_id(0)

    @pl.when(j < COL_CONV // PACK_TN)
    def _():
        o_ref[...] = a_ref[...].T.astype(jnp.bfloat16)

    @pl.when((j >= COL_CONV // PACK_TN) & (j < COL_BETA // PACK_TN))
    def _():
        rows = jnp.concatenate([a_ref[N_BA:, :], m_ref[...]], axis=0)
        o_ref[...] = rows.T.astype(jnp.bfloat16)

    @pl.when(j == COL_BETA // PACK_TN)
    def _():
        pad = jnp.zeros((LANES - HEADS, D_MODEL), jnp.float32)
        rest = jnp.zeros((PACK_TN - 2 * LANES, D_MODEL), jnp.float32)
        rows = jnp.concatenate([m_ref[:HEADS, :], pad, m_ref[HEADS:, :], pad, rest], axis=0)
        o_ref[...] = rows.T.astype(jnp.bfloat16)


def _pack_weights(wt):
    last = COL_BETA // PACK_TN
    mini_per_block = PACK_TN // N_BA
    mini_index = lambda j: jnp.where(j == last, COL_CONV // N_BA, (j + 1) * mini_per_block)
    return pl.pallas_call(
        _pack_kernel,
        grid=(pl.cdiv(PROJ_PACKED, PACK_TN),),
        in_specs=[
            pl.BlockSpec((PACK_TN, D_MODEL), lambda j: (jnp.minimum(j, last - 1), 0)),
            pl.BlockSpec((N_BA, D_MODEL), lambda j: (mini_index(j), 0)),
        ],
        out_specs=pl.BlockSpec((D_MODEL, PACK_TN), lambda j: (0, j)),
        out_shape=jax.ShapeDtypeStruct((D_MODEL, PROJ_PACKED), jnp.bfloat16),
        compiler_params=pltpu.CompilerParams(
            dimension_semantics=("arbitrary",), vmem_limit_bytes=VMEM_LIMIT),
        name="pack_weights",
    )(wt, wt)


def _layer(x, norm_in_w, w_in, conv_qkv_w, a_log, dt_bias, gdn_norm_w, conv_w, conv_b, w_out,
           out_norm_w):
    assert w_in.shape[1] == COL_BETA + N_BA
    w_packed = _pack_weights(w_in.T)
    return _block(x, norm_in_w.reshape(1, -1), w_packed, conv_qkv_w,
                  _pad_lanes(a_log.reshape(1, -1)), _pad_lanes(dt_bias.reshape(1, -1)),
                  gdn_norm_w.reshape(1, -1), conv_w, conv_b.reshape(1, -1),
                  w_out.astype(jnp.bfloat16), out_norm_w.reshape(1, -1))


def kernel(x, norm_in_w, w_in, conv_qkv_w, A_log, dt_bias, gdn_norm_w, conv_w, conv_b, w_out,
           final_norm_w):
    batch, seq, d_model = x.shape
    assert norm_in_w.shape[0] == 1 and d_model == D_MODEL
    assert batch % ROWS == 0 and seq % TLR == 0 and TLR % CHUNK == 0
    return _layer(x, norm_in_w[0], w_in[0], conv_qkv_w[0], A_log[0], dt_bias[0], gdn_norm_w[0],
                  conv_w[0], conv_b[0], w_out[0], final_norm_w)
```

```python
import jax
import jax.numpy as jnp
from jax import lax
from jax.experimental import pallas as pl
from jax.experimental.pallas import tpu as pltpu

D_MODEL = 1024
HEADS = 8
HEAD_DIM = 128
GDN_WIDTH = HEADS * HEAD_DIM
CONV_WIDTH = 1024
GDN_CONV = 4
SHORT_CONV = 3
CHUNK = 64
EPS = 1e-6
LANES = 128
HALO = 8
ROW_STRIDE = 4

COL_Z = 3 * GDN_WIDTH
COL_CONV = COL_Z + GDN_WIDTH
COL_BETA = COL_CONV + 4 * CONV_WIDTH
COL_A = COL_BETA + LANES
PROJ_PACKED = COL_A + LANES
N_BA = 2 * HEADS
PACK_TN = 1024

ROWS = 4
TLR = 64
TL = ROWS * TLR
VMEM_LIMIT = 60 * 1024 * 1024


def _bdot(a, b):
    return jnp.dot(a.astype(jnp.bfloat16), b.astype(jnp.bfloat16),
                   preferred_element_type=jnp.float32)


def _bdot_nt(a, b):
    return lax.dot_general(a.astype(jnp.bfloat16), b.astype(jnp.bfloat16),
                           (((1,), (1,)), ((), ())), preferred_element_type=jnp.float32)


def _bdot_tn(a, b):
    return lax.dot_general(a.astype(jnp.bfloat16), b.astype(jnp.bfloat16),
                           (((0,), (0,)), ((), ())), preferred_element_type=jnp.float32)


def _silu(x):
    return x * jax.nn.sigmoid(x)


def _rmsnorm(x, w):
    ms = jnp.mean(x * x, axis=-1, keepdims=True)
    return x * lax.rsqrt(ms + EPS) * w


def _fill_slabs(slab_ref, x):
    n = x.shape[1] // LANES
    for b in range(ROWS):
        for j in range(n):
            slab_ref[b * n + j, HALO:, :] = x[b * TLR:(b + 1) * TLR, j * LANES:(j + 1) * LANES]


def _conv_row_classes(slab_ref, j, w_ref, lane0, width):
    taps = [w_ref[width - 1 - s:width - s, lane0:lane0 + LANES] for s in range(width)]
    loaded = {}

    def rows_from(start):
        if start not in loaded:
            loaded[start] = slab_ref[j, pl.ds(start, TLR // ROW_STRIDE, stride=ROW_STRIDE), :]
        return loaded[start]

    out = []
    for r in range(ROW_STRIDE):
        acc = rows_from(HALO + r) * taps[0]
        for s in range(1, width):
            acc = acc + rows_from(HALO + r - s) * taps[s]
        out.append(acc)
    slab_ref[j, :HALO, :] = slab_ref[j, TLR:, :]
    return out


def _store_row_class(dst_ref, j, b, r, val):
    dst_ref[j, pl.ds(b * TLR + r, TLR // ROW_STRIDE, stride=ROW_STRIDE), :] = val


def _block_kernel(x_ref, nw_ref, wp_ref, cw_ref, alog_ref, dtb_ref, gnw_ref, ccw_ref, cb_ref,
                  wo_ref, fnw_ref, out_ref,
                  s_ref, pre_slab, ch_slab, cv_s, q_s, k_s, v_s, z_s, beta_s, gc_s, gct_s, betat_s,
                  o_s):
    n_chunks = TL // CHUNK
    row_chunks = TLR // CHUNK
    heads = range(HEADS)
    lanes = [slice(h * HEAD_DIM, (h + 1) * HEAD_DIM) for h in heads]
    col = lambda blk, h: blk[:, h:h + 1]
    conv_tiles = CONV_WIDTH // LANES

    @pl.when(pl.program_id(1) == 0)
    def _():
        s_ref[...] = jnp.zeros_like(s_ref)
        pre_slab[:, :, :HALO, :] = jnp.zeros(pre_slab.shape[:2] + (HALO, LANES), jnp.float32)
        ch_slab[:, :HALO, :] = jnp.zeros((ch_slab.shape[0], HALO, LANES), jnp.float32)

    x = x_ref[...].reshape(TL, D_MODEL)
    hn = _rmsnorm(x, nw_ref[...]).astype(jnp.bfloat16)
    proj = lambda lo, hi: jnp.dot(hn, wp_ref[:, lo:hi], preferred_element_type=jnp.float32)
    conv_col = lambda j: proj(COL_CONV + j * CONV_WIDTH, COL_CONV + (j + 1) * CONV_WIDTH)

    ba = proj(COL_BETA, PROJ_PACKED)
    beta = jax.nn.sigmoid(ba[:, :LANES])
    beta_s[...] = beta
    a_in = ba[:, LANES:] + dtb_ref[...]
    softplus = jnp.maximum(a_in, 0.0) + jnp.log1p(jnp.exp(-jnp.abs(a_in)))
    gc = -jnp.exp(alog_ref[...]) * softplus
    pos = lax.broadcasted_iota(jnp.int32, gc.shape, 0) % CHUNK
    s = 1
    while s < CHUNK:
        gc = gc + jnp.where(pos >= s, pltpu.roll(gc, s, axis=0), 0.0)
        s *= 2
    gc_s[...] = gc
    for src, dst in ((gc.T, gct_s), (beta.T, betat_s)):
        for c in range(n_chunks):
            blk = src[:, c * CHUNK:(c + 1) * CHUNK]
            dst[c] = jnp.concatenate([blk, blk], axis=1)

    conv_in = []
    for grp, dst in enumerate((q_s, k_s, v_s)):
        col0 = grp * GDN_WIDTH
        slabs = pre_slab.at[grp]
        _fill_slabs(slabs, proj(col0, col0 + GDN_WIDTH))
        conv_in.append(conv_col(grp))
        scale = HEAD_DIM ** -0.5 if grp == 0 else 1.0
        for b in range(ROWS):
            for h in heads:
                classes = _conv_row_classes(slabs, b * HEADS + h, cw_ref, col0 + h * HEAD_DIM,
                                            GDN_CONV)
                for r, acc in enumerate(classes):
                    act = _silu(acc)
                    if grp < 2:
                        inv = lax.rsqrt(jnp.sum(act * act, axis=-1, keepdims=True) + EPS)
                        act = act * inv * scale
                    _store_row_class(dst, h, b, r, act)
    z_s[...] = proj(COL_Z, COL_CONV)
    gate_c = conv_col(3)

    gate_b, gate_cc, h_c = conv_in
    _fill_slabs(ch_slab, gate_cc * h_c)
    for b in range(ROWS):
        for j in range(conv_tiles):
            classes = _conv_row_classes(ch_slab, b * conv_tiles + j, ccw_ref, j * LANES,
                                        SHORT_CONV)
            for r, acc in enumerate(classes):
                _store_row_class(cv_s, j, b, r, acc)
    conv = jnp.concatenate([cv_s[j] for j in range(conv_tiles)], axis=1)
    y = (gate_b * (conv + cb_ref[...]) * _silu(gate_c)).astype(jnp.bfloat16)

    ri = lax.broadcasted_iota(jnp.int32, (CHUNK, 2 * CHUNK), 0)
    ci = lax.broadcasted_iota(jnp.int32, (CHUNK, 2 * CHUNK), 1)
    cj = ci % CHUNK
    causal2 = ri >= cj
    left = ci < CHUNK
    eye_right = jnp.where((ri == cj) & (ci >= CHUNK), 1.0, 0.0)
    keep_right = jnp.where(left, 0.0, 1.0)
    keep_left = jnp.where(left, 1.0, 0.0)
    off_diag = jnp.where(ri == cj, 0.0, 1.0)

    items = []
    for c in range(n_chunks):
        rows = slice(c * CHUNK, (c + 1) * CHUNK)
        gc_blk = gc_s[rows, :]
        eg_blk = jnp.exp(gc_blk)
        gct_blk = gct_s[c]
        betat_blk = betat_s[c]
        for h in heads:
            items.append((rows, h, betat_blk[h:h + 1, :], col(gc_blk, h), col(eg_blk, h),
                          gct_blk[h:h + 1, :]))
    n_items = range(len(items))
    k = [k_s[h, rows, :] for rows, h, *_ in items]
    q = [q_s[h, rows, :] for rows, h, *_ in items]
    decayb = [jnp.where(causal2, jnp.exp(it[3] - it[5]), 0.0) * it[2] for it in items]
    kq = [_bdot_nt(jnp.concatenate([k[n], q[n]], axis=0),
                   jnp.concatenate([k[n], k[n]], axis=0)) for n in n_items]
    attn_d = [kq[n][CHUNK:, :CHUNK] * decayb[n][:, :CHUNK] for n in n_items]
    qd = [q[n] * items[n][4] for n in n_items]
    a2 = [kq[n][:CHUNK] * decayb[n] * off_diag for n in n_items]
    prod = [_bdot(a[:, :CHUNK], a) for a in a2]
    pt = [prod[n] * keep_left + (eye_right - a2[n] * keep_right) for n in n_items]
    m = 4
    while m < CHUNK:
        prod = [_bdot(p[:, :CHUNK], p) for p in pt]
        pt = [prod[n] + pt[n] * keep_right for n in n_items]
        m *= 2
    prod = [_bdot(p[:, :CHUNK], p) for p in pt]
    t_pad = [pt[n] + prod[n] for n in n_items]
    rhs = [jnp.concatenate([v_s[it[1], it[0], :], k[n] * it[4]], axis=1)
           for n, it in enumerate(items)]
    no_rows = jnp.zeros((CHUNK, 2 * HEAD_DIM), jnp.float32)
    sol = [_bdot(t_pad[n], jnp.concatenate([no_rows, rhs[n]], axis=0))
           for n in n_items]

    chains = [(b, h) for b in range(ROWS) for h in heads]
    state = [s_ref[b * HEADS + h] for b, h in chains]
    for step in range(row_chunks):
        item = [(b * row_chunks + step) * HEADS + h for b, h in chains]
        rows = [slice((b * row_chunks + step) * CHUNK, (b * row_chunks + step + 1) * CHUNK)
                for b in range(ROWS)]
        gc_blk = [gc_s[r, :] for r in rows]
        g_last = [g[CHUNK - 1:CHUNK, :] for g in gc_blk]
        ekb_blk = [jnp.exp(g_last[b] - gc_blk[b]) * beta_s[rows[b], :] for b in range(ROWS)]
        egl_blk = [jnp.exp(g) for g in g_last]
        ws_qs = [_bdot(jnp.concatenate([sol[item[n]][:, HEAD_DIM:], qd[item[n]]], axis=0),
                       state[n]) for n in range(len(chains))]
        v_new = [sol[item[n]][:, :HEAD_DIM] - ws_qs[n][:CHUNK] for n in range(len(chains))]
        o = [ws_qs[n][CHUNK:] + _bdot(attn_d[item[n]], v_new[n]) for n in range(len(chains))]
        kv = [_bdot_tn(k[item[n]] * col(ekb_blk[b], h), v_new[n])
              for n, (b, h) in enumerate(chains)]
        state = [state[n] * col(egl_blk[b], h) + kv[n] for n, (b, h) in enumerate(chains)]
        for n, (b, h) in enumerate(chains):
            o_s[rows[b], lanes[h]] = o[n]
    for n, (b, h) in enumerate(chains):
        s_ref[b * HEADS + h] = state[n]

    res = x + jnp.dot(y, wo_ref[GDN_WIDTH:, :], preferred_element_type=jnp.float32)
    gated = jnp.concatenate(
        [_rmsnorm(o_s[:, lanes[h]], gnw_ref[...]) * _silu(z_s[:, lanes[h]]) for h in heads],
        axis=1)

    res = res + _bdot(gated, wo_ref[:GDN_WIDTH, :])
    out_ref[...] = _rmsnorm(res, fnw_ref[...]).reshape(ROWS, TLR, D_MODEL)


def _block(x, norm_w, w_packed, conv_qkv_w, a_log_pad, dt_bias_pad, gdn_norm_w, conv_w, conv_b,
           w_out, out_norm_w):
    batch, seq, _ = x.shape
    n_chunks = TL // CHUNK
    conv_tiles = CONV_WIDTH // LANES
    tile = lambda g, t: (g, t, 0)
    const = lambda g, t: (0, 0)
    resident = lambda shape: pl.BlockSpec(shape, const, pipeline_mode=pl.Buffered(1))
    tile_f32 = pltpu.VMEM((TL, GDN_WIDTH), jnp.float32)
    head_tile_f32 = pltpu.VMEM((HEADS, TL, HEAD_DIM), jnp.float32)
    return pl.pallas_call(
        _block_kernel,
        grid=(batch // ROWS, seq // TLR),
        in_specs=[
            pl.BlockSpec((ROWS, TLR, D_MODEL), tile),
            resident((1, D_MODEL)),
            resident((D_MODEL, PROJ_PACKED)),
            resident((GDN_CONV, 3 * GDN_WIDTH)),
            resident((1, LANES)),
            resident((1, LANES)),
            resident((1, HEAD_DIM)),
            resident((SHORT_CONV, CONV_WIDTH)),
            resident((1, CONV_WIDTH)),
            resident((GDN_WIDTH + CONV_WIDTH, D_MODEL)),
            resident((1, D_MODEL)),
        ],
        out_specs=pl.BlockSpec((ROWS, TLR, D_MODEL), tile),
        out_shape=jax.ShapeDtypeStruct(x.shape, jnp.float32),
        scratch_shapes=[
            pltpu.VMEM((ROWS * HEADS, HEAD_DIM, HEAD_DIM), jnp.float32),
            pltpu.VMEM((3, ROWS * HEADS, HALO + TLR, LANES), jnp.float32),
            pltpu.VMEM((ROWS * conv_tiles, HALO + TLR, LANES), jnp.float32),
            pltpu.VMEM((conv_tiles, TL, LANES), jnp.float32),
            head_tile_f32, head_tile_f32, head_tile_f32,
            tile_f32,
            pltpu.VMEM((TL, LANES), jnp.float32),
            pltpu.VMEM((TL, LANES), jnp.float32),
            pltpu.VMEM((n_chunks, LANES, 2 * CHUNK), jnp.float32),
            pltpu.VMEM((n_chunks, LANES, 2 * CHUNK), jnp.float32),
            tile_f32,
        ],
        compiler_params=pltpu.CompilerParams(
            dimension_semantics=("arbitrary", "arbitrary"), vmem_limit_bytes=VMEM_LIMIT),
        name="hybrid_block",
    )(x, norm_w, w_packed, conv_qkv_w, a_log_pad, dt_bias_pad, gdn_norm_w, conv_w, conv_b,
      w_out, out_norm_w)


def _pad_lanes(v):
    return jnp.pad(v, ((0, 0), (0, LANES - v.shape[-1])))


def _pack_kernel(a_ref, m_ref, o_ref):
    j = pl.program_id(0)

    @pl.when(j < COL_CONV // PACK_TN)
    def _():
        o_ref[...] = a_ref[...].T.astype(jnp.bfloat16)

    @pl.when((j >= COL_CONV // PACK_TN) & (j < COL_BETA // PACK_TN))
    def _():
        rows = jnp.concatenate([a_ref[N_BA:, :], m_ref[...]], axis=0)
        o_ref[...] = rows.T.astype(jnp.bfloat16)

    @pl.when(j == COL_BETA // PACK_TN)
    def _():
        pad = jnp.zeros((LANES - HEADS, D_MODEL), jnp.float32)
        rest = jnp.zeros((PACK_TN - 2 * LANES, D_MODEL), jnp.float32)
        rows = jnp.concatenate([m_ref[:HEADS, :], pad, m_ref[HEADS:, :], pad, rest], axis=0)
        o_ref[...] = rows.T.astype(jnp.bfloat16)


def _pack_weights(wt):
    last = COL_BETA // PACK_TN
    mini_per_block = PACK_TN // N_BA
    mini_index = lambda j: jnp.where(j == last, COL_CONV // N_BA, (j + 1) * mini_per_block)
    return pl.pallas_call(
        _pack_kernel,
        grid=(pl.cdiv(PROJ_PACKED, PACK_TN),),
        in_specs=[
            pl.BlockSpec((PACK_TN, D_MODEL), lambda j: (jnp.minimum(j, last - 1), 0)),
            pl.BlockSpec((N_BA, D_MODEL), lambda j: (mini_index(j), 0)),
        ],
        out_specs=pl.BlockSpec((D_MODEL, PACK_TN), lambda j: (0, j)),
        out_shape=jax.ShapeDtypeStruct((D_MODEL, PROJ_PACKED), jnp.bfloat16),
        compiler_params=pltpu.CompilerParams(
            dimension_semantics=("arbitrary",), vmem_limit_bytes=VMEM_LIMIT),
        name="pack_weights",
    )(wt, wt)


def _layer(x, norm_in_w, w_in, conv_qkv_w, a_log, dt_bias, gdn_norm_w, conv_w, conv_b, w_out,
           out_norm_w):
    assert w_in.shape[1] == COL_BETA + N_BA
    w_packed = _pack_weights(w_in.T)
    return _block(x, norm_in_w.reshape(1, -1), w_packed, conv_qkv_w,
                  _pad_lanes(a_log.reshape(1, -1)), _pad_lanes(dt_bias.reshape(1, -1)),
                  gdn_norm_w.reshape(1, -1), conv_w, conv_b.reshape(1, -1),
                  w_out.astype(jnp.bfloat16), out_norm_w.reshape(1, -1))


def kernel(x, norm_in_w, w_in, conv_qkv_w, A_log, dt_bias, gdn_norm_w, conv_w, conv_b, w_out,
           final_norm_w):
    batch, seq, d_model = x.shape
    assert norm_in_w.shape[0] == 1 and d_model == D_MODEL
    assert batch % ROWS == 0 and seq % TLR == 0 and TLR % CHUNK == 0
    return _layer(x, norm_in_w[0], w_in[0], conv_qkv_w[0], A_log[0], dt_bias[0], gdn_norm_w[0],
                  conv_w[0], conv_b[0], w_out[0], final_norm_w)
```

```python
import functools

import jax
import jax.numpy as jnp
from jax import lax
from jax.experimental import pallas as pl
from jax.experimental.pallas import tpu as pltpu

D_MODEL = 1024
HEADS = 8
HEAD_DIM = 128
GDN_WIDTH = HEADS * HEAD_DIM
CONV_WIDTH = 1024
GDN_CONV = 4
SHORT_CONV = 3
CHUNK = 64
EPS = 1e-6
LANES = 128
HALO = 8
ROW_STRIDE = 4

COL_Z = 3 * GDN_WIDTH
COL_CONV = COL_Z + GDN_WIDTH
COL_BETA = COL_CONV + 4 * CONV_WIDTH
COL_A = COL_BETA + LANES
PROJ_PACKED = COL_A + LANES
N_BA = 2 * HEADS
PACK_TN = 1024
PACK_STEPS = COL_BETA // PACK_TN + 1
WOUT_TM = 512
WOUT_STEPS = (GDN_WIDTH + CONV_WIDTH) // WOUT_TM
PRE_STEPS = PACK_STEPS + WOUT_STEPS

ROWS = 4
TLR = 64
TL = ROWS * TLR
VMEM_LIMIT = 60 * 1024 * 1024


def _bdot(a, b):
    return jnp.dot(a.astype(jnp.bfloat16), b.astype(jnp.bfloat16),
                   preferred_element_type=jnp.float32)


def _bdot_nt(a, b):
    return lax.dot_general(a.astype(jnp.bfloat16), b.astype(jnp.bfloat16),
                           (((1,), (1,)), ((), ())), preferred_element_type=jnp.float32)


def _bdot_tn(a, b):
    return lax.dot_general(a.astype(jnp.bfloat16), b.astype(jnp.bfloat16),
                           (((0,), (0,)), ((), ())), preferred_element_type=jnp.float32)


def _silu(x):
    return x * jax.nn.sigmoid(x)


def _rmsnorm(x, w):
    ms = jnp.mean(x * x, axis=-1, keepdims=True)
    return x * lax.rsqrt(ms + EPS) * w


def _fill_slabs(slab_ref, x):
    n = x.shape[1] // LANES
    for b in range(ROWS):
        for j in range(n):
            slab_ref[b * n + j, HALO:, :] = x[b * TLR:(b + 1) * TLR, j * LANES:(j + 1) * LANES]


def _conv_row_classes(slab_ref, j, w_ref, lane0, width):
    taps = [w_ref[width - 1 - s:width - s, lane0:lane0 + LANES] for s in range(width)]
    loaded = {}

    def rows_from(start):
        if start not in loaded:
            loaded[start] = slab_ref[j, pl.ds(start, TLR // ROW_STRIDE, stride=ROW_STRIDE), :]
        return loaded[start]

    out = []
    for r in range(ROW_STRIDE):
        acc = rows_from(HALO + r) * taps[0]
        for s in range(1, width):
            acc = acc + rows_from(HALO + r - s) * taps[s]
        out.append(acc)
    slab_ref[j, :HALO, :] = slab_ref[j, TLR:, :]
    return out


def _store_row_class(dst_ref, j, b, r, val):
    dst_ref[j, pl.ds(b * TLR + r, TLR // ROW_STRIDE, stride=ROW_STRIDE), :] = val


def _tile_body(first_tile, x_ref, nw_ref, wp_ref, cw_ref, alog_ref, dtb_ref, gnw_ref, ccw_ref,
               cb_ref, wo_ref, fnw_ref, out_ref,
               s_ref, pre_slab, ch_slab, cv_s, q_s, k_s, v_s, z_s, beta_s, gc_s, gct_s, betat_s,
               o_s):
    n_chunks = TL // CHUNK
    row_chunks = TLR // CHUNK
    heads = range(HEADS)
    lanes = [slice(h * HEAD_DIM, (h + 1) * HEAD_DIM) for h in heads]
    col = lambda blk, h: blk[:, h:h + 1]
    conv_tiles = CONV_WIDTH // LANES

    @pl.when(first_tile)
    def _():
        s_ref[...] = jnp.zeros_like(s_ref)
        pre_slab[:, :, :HALO, :] = jnp.zeros(pre_slab.shape[:2] + (HALO, LANES), jnp.float32)
        ch_slab[:, :HALO, :] = jnp.zeros((ch_slab.shape[0], HALO, LANES), jnp.float32)

    x = x_ref[...].reshape(TL, D_MODEL)
    hn = _rmsnorm(x, nw_ref[...]).astype(jnp.bfloat16)
    proj = lambda lo, hi: jnp.dot(hn, wp_ref[:, lo:hi], preferred_element_type=jnp.float32)
    conv_col = lambda j: proj(COL_CONV + j * CONV_WIDTH, COL_CONV + (j + 1) * CONV_WIDTH)

    ba = proj(COL_BETA, PROJ_PACKED)
    beta = jax.nn.sigmoid(ba[:, :LANES])
    beta_s[...] = beta
    a_in = ba[:, LANES:] + dtb_ref[...]
    softplus = jnp.maximum(a_in, 0.0) + jnp.log1p(jnp.exp(-jnp.abs(a_in)))
    gc = -jnp.exp(alog_ref[...]) * softplus
    pos = lax.broadcasted_iota(jnp.int32, gc.shape, 0) % CHUNK
    s = 1
    while s < CHUNK:
        gc = gc + jnp.where(pos >= s, pltpu.roll(gc, s, axis=0), 0.0)
        s *= 2
    gc_s[...] = gc
    for src, dst in ((gc.T, gct_s), (beta.T, betat_s)):
        for c in range(n_chunks):
            blk = src[:, c * CHUNK:(c + 1) * CHUNK]
            dst[c] = jnp.concatenate([blk, blk], axis=1)

    conv_in = []
    for grp, dst in enumerate((q_s, k_s, v_s)):
        col0 = grp * GDN_WIDTH
        slabs = pre_slab.at[grp]
        _fill_slabs(slabs, proj(col0, col0 + GDN_WIDTH))
        conv_in.append(conv_col(grp))
        scale = HEAD_DIM ** -0.5 if grp == 0 else 1.0
        for b in range(ROWS):
            for h in heads:
                classes = _conv_row_classes(slabs, b * HEADS + h, cw_ref, col0 + h * HEAD_DIM,
                                            GDN_CONV)
                for r, acc in enumerate(classes):
                    act = _silu(acc)
                    if grp < 2:
                        inv = lax.rsqrt(jnp.sum(act * act, axis=-1, keepdims=True) + EPS)
                        act = act * inv * scale
                    _store_row_class(dst, h, b, r, act)
    z_s[...] = proj(COL_Z, COL_CONV)
    gate_c = conv_col(3)

    gate_b, gate_cc, h_c = conv_in
    _fill_slabs(ch_slab, gate_cc * h_c)
    for b in range(ROWS):
        for j in range(conv_tiles):
            classes = _conv_row_classes(ch_slab, b * conv_tiles + j, ccw_ref, j * LANES,
                                        SHORT_CONV)
            for r, acc in enumerate(classes):
                _store_row_class(cv_s, j, b, r, acc)
    conv = jnp.concatenate([cv_s[j] for j in range(conv_tiles)], axis=1)
    y = (gate_b * (conv + cb_ref[...]) * _silu(gate_c)).astype(jnp.bfloat16)

    ri = lax.broadcasted_iota(jnp.int32, (CHUNK, 2 * CHUNK), 0)
    ci = lax.broadcasted_iota(jnp.int32, (CHUNK, 2 * CHUNK), 1)
    cj = ci % CHUNK
    causal2 = ri >= cj
    left = ci < CHUNK
    eye_right = jnp.where((ri == cj) & (ci >= CHUNK), 1.0, 0.0)
    keep_right = jnp.where(left, 0.0, 1.0)
    keep_left = jnp.where(left, 1.0, 0.0)
    off_diag = jnp.where(ri == cj, 0.0, 1.0)

    items = []
    for c in range(n_chunks):
        rows = slice(c * CHUNK, (c + 1) * CHUNK)
        gc_blk = gc_s[rows, :]
        eg_blk = jnp.exp(gc_blk)
        gct_blk = gct_s[c]
        betat_blk = betat_s[c]
        for h in heads:
            items.append((rows, h, betat_blk[h:h + 1, :], col(gc_blk, h), col(eg_blk, h),
                          gct_blk[h:h + 1, :]))
    n_items = range(len(items))
    k = [k_s[h, rows, :] for rows, h, *_ in items]
    q = [q_s[h, rows, :] for rows, h, *_ in items]
    decayb = [jnp.where(causal2, jnp.exp(it[3] - it[5]), 0.0) * it[2] for it in items]
    kq = [_bdot_nt(jnp.concatenate([k[n], q[n]], axis=0),
                   jnp.concatenate([k[n], k[n]], axis=0)) for n in n_items]
    attn_d = [kq[n][CHUNK:, :CHUNK] * decayb[n][:, :CHUNK] for n in n_items]
    qd = [q[n] * items[n][4] for n in n_items]
    a2 = [kq[n][:CHUNK] * decayb[n] * off_diag for n in n_items]
    prod = [_bdot(a[:, :CHUNK], a) for a in a2]
    pt = [prod[n] * keep_left + (eye_right - a2[n] * keep_right) for n in n_items]
    m = 4
    while m < CHUNK:
        prod = [_bdot(p[:, :CHUNK], p) for p in pt]
        pt = [prod[n] + pt[n] * keep_right for n in n_items]
        m *= 2
    prod = [_bdot(p[:, :CHUNK], p) for p in pt]
    t_pad = [pt[n] + prod[n] for n in n_items]
    rhs = [jnp.concatenate([v_s[it[1], it[0], :], k[n] * it[4]], axis=1)
           for n, it in enumerate(items)]
    no_rows = jnp.zeros((CHUNK, 2 * HEAD_DIM), jnp.float32)
    sol = [_bdot(t_pad[n], jnp.concatenate([no_rows, rhs[n]], axis=0))
           for n in n_items]

    chains = [(b, h) for b in range(ROWS) for h in heads]
    state = [s_ref[b * HEADS + h] for b, h in chains]
    for step in range(row_chunks):
        item = [(b * row_chunks + step) * HEADS + h for b, h in chains]
        rows = [slice((b * row_chunks + step) * CHUNK, (b * row_chunks + step + 1) * CHUNK)
                for b in range(ROWS)]
        gc_blk = [gc_s[r, :] for r in rows]
        g_last = [g[CHUNK - 1:CHUNK, :] for g in gc_blk]
        ekb_blk = [jnp.exp(g_last[b] - gc_blk[b]) * beta_s[rows[b], :] for b in range(ROWS)]
        egl_blk = [jnp.exp(g) for g in g_last]
        ws_qs = [_bdot(jnp.concatenate([sol[item[n]][:, HEAD_DIM:], qd[item[n]]], axis=0),
                       state[n]) for n in range(len(chains))]
        v_new = [sol[item[n]][:, :HEAD_DIM] - ws_qs[n][:CHUNK] for n in range(len(chains))]
        o = [ws_qs[n][CHUNK:] + _bdot(attn_d[item[n]], v_new[n]) for n in range(len(chains))]
        kv = [_bdot_tn(k[item[n]] * col(ekb_blk[b], h), v_new[n])
              for n, (b, h) in enumerate(chains)]
        state = [state[n] * col(egl_blk[b], h) + kv[n] for n, (b, h) in enumerate(chains)]
        for n, (b, h) in enumerate(chains):
            o_s[rows[b], lanes[h]] = o[n]
    for n, (b, h) in enumerate(chains):
        s_ref[b * HEADS + h] = state[n]

    res = x + jnp.dot(y, wo_ref[GDN_WIDTH:, :], preferred_element_type=jnp.float32)
    gated = jnp.concatenate(
        [_rmsnorm(o_s[:, lanes[h]], gnw_ref[...]) * _silu(z_s[:, lanes[h]]) for h in heads],
        axis=1)

    res = res + _bdot(gated, wo_ref[:GDN_WIDTH, :])
    out_ref[...] = _rmsnorm(res, fnw_ref[...]).reshape(ROWS, TLR, D_MODEL)


def _pack_block(j, wt_ref, mini_ref, wp_s):
    cols = slice(j * PACK_TN, (j + 1) * PACK_TN)
    if j < COL_CONV // PACK_TN:
        wp_s[:, cols] = wt_ref[...].T.astype(jnp.bfloat16)
    elif j < COL_BETA // PACK_TN:
        rows = jnp.concatenate([wt_ref[N_BA:, :], mini_ref[...]], axis=0)
        wp_s[:, cols] = rows.T.astype(jnp.bfloat16)
    else:
        pad = jnp.zeros((LANES - HEADS, D_MODEL), jnp.float32)
        rows = jnp.concatenate([mini_ref[:HEADS, :], pad, mini_ref[HEADS:, :], pad], axis=0)
        wp_s[:, COL_BETA:] = rows.T.astype(jnp.bfloat16)


def _block_kernel(x_ref, wt_ref, mini_ref, wout_ref, nw_ref, cw_ref, alog_ref, dtb_ref, gnw_ref,
                  ccw_ref, cb_ref, fnw_ref, out_ref, wp_s, wo_s, *scratch, tiles):
    step = pl.program_id(0)
    for j in range(PACK_STEPS):
        pl.when(step == j)(functools.partial(_pack_block, j, wt_ref, mini_ref, wp_s))
    for j in range(WOUT_STEPS):
        @pl.when(step == PACK_STEPS + j)
        def _(j=j):
            wo_s[j * WOUT_TM:(j + 1) * WOUT_TM, :] = wout_ref[...].astype(jnp.bfloat16)

    @pl.when(step >= PRE_STEPS)
    def _():
        _tile_body(lax.rem(step - PRE_STEPS, tiles) == 0, x_ref, nw_ref, wp_s, cw_ref, alog_ref,
                   dtb_ref, gnw_ref, ccw_ref, cb_ref, wo_s, fnw_ref, out_ref, *scratch)


def _block(x, wt, w_out, norm_w, conv_qkv_w, a_log_pad, dt_bias_pad, gdn_norm_w, conv_w, conv_b,
           out_norm_w):
    batch, seq, _ = x.shape
    tiles = seq // TLR
    n_chunks = TL // CHUNK
    conv_tiles = CONV_WIDTH // LANES
    last_pack = PACK_STEPS - 1
    mini_per_block = PACK_TN // N_BA

    def tile(i):
        j = jnp.maximum(i - PRE_STEPS, 0)
        return (j // tiles, j % tiles, 0)

    mini = lambda i: (jnp.where(i >= last_pack, COL_CONV // N_BA,
                                (i + 1) * mini_per_block), 0)
    const = lambda i: (0, 0)
    resident = lambda shape: pl.BlockSpec(shape, const, pipeline_mode=pl.Buffered(1))
    tile_f32 = pltpu.VMEM((TL, GDN_WIDTH), jnp.float32)
    head_tile_f32 = pltpu.VMEM((HEADS, TL, HEAD_DIM), jnp.float32)
    return pl.pallas_call(
        functools.partial(_block_kernel, tiles=tiles),
        grid=(PRE_STEPS + (batch // ROWS) * tiles,),
        in_specs=[
            pl.BlockSpec((ROWS, TLR, D_MODEL), tile),
            pl.BlockSpec((PACK_TN, D_MODEL), lambda i: (jnp.minimum(i, last_pack - 1), 0)),
            pl.BlockSpec((N_BA, D_MODEL), mini),
            pl.BlockSpec((WOUT_TM, D_MODEL),
                         lambda i: (jnp.clip(i - PACK_STEPS, 0, WOUT_STEPS - 1), 0)),
            resident((1, D_MODEL)),
            resident((GDN_CONV, 3 * GDN_WIDTH)),
            resident((1, LANES)),
            resident((1, LANES)),
            resident((1, HEAD_DIM)),
            resident((SHORT_CONV, CONV_WIDTH)),
            resident((1, CONV_WIDTH)),
            resident((1, D_MODEL)),
        ],
        out_specs=pl.BlockSpec((ROWS, TLR, D_MODEL), tile),
        out_shape=jax.ShapeDtypeStruct(x.shape, jnp.float32),
        scratch_shapes=[
            pltpu.VMEM((D_MODEL, PROJ_PACKED), jnp.bfloat16),
            pltpu.VMEM((GDN_WIDTH + CONV_WIDTH, D_MODEL), jnp.bfloat16),
            pltpu.VMEM((ROWS * HEADS, HEAD_DIM, HEAD_DIM), jnp.float32),
            pltpu.VMEM((3, ROWS * HEADS, HALO + TLR, LANES), jnp.float32),
            pltpu.VMEM((ROWS * conv_tiles, HALO + TLR, LANES), jnp.float32),
            pltpu.VMEM((conv_tiles, TL, LANES), jnp.float32),
            head_tile_f32, head_tile_f32, head_tile_f32,
            tile_f32,
            pltpu.VMEM((TL, LANES), jnp.float32),
            pltpu.VMEM((TL, LANES), jnp.float32),
            pltpu.VMEM((n_chunks, LANES, 2 * CHUNK), jnp.float32),
            pltpu.VMEM((n_chunks, LANES, 2 * CHUNK), jnp.float32),
            tile_f32,
        ],
        compiler_params=pltpu.CompilerParams(
            dimension_semantics=("arbitrary",), vmem_limit_bytes=VMEM_LIMIT),
        name="hybrid_block",
    )(x, wt, wt, w_out, norm_w, conv_qkv_w, a_log_pad, dt_bias_pad, gdn_norm_w, conv_w, conv_b,
      out_norm_w)


def _pad_lanes(v):
    return jnp.pad(v, ((0, 0), (0, LANES - v.shape[-1])))


def _layer(x, norm_in_w, w_in, conv_qkv_w, a_log, dt_bias, gdn_norm_w, conv_w, conv_b, w_out,
           out_norm_w):
    assert w_in.shape[1] == COL_BETA + N_BA
    return _block(x, w_in.T, w_out, norm_in_w.reshape(1, -1), conv_qkv_w,
                  _pad_lanes(a_log.reshape(1, -1)), _pad_lanes(dt_bias.reshape(1, -1)),
                  gdn_norm_w.reshape(1, -1), conv_w, conv_b.reshape(1, -1),
                  out_norm_w.reshape(1, -1))


def kernel(x, norm_in_w, w_in, conv_qkv_w, A_log, dt_bias, gdn_norm_w, conv_w, conv_b, w_out,
           final_norm_w):
    batch, seq, d_model = x.shape
    assert norm_in_w.shape[0] == 1 and d_model == D_MODEL
    assert batch % ROWS == 0 and seq % TLR == 0 and TLR % CHUNK == 0
    return _layer(x, norm_in_w[0], w_in[0], conv_qkv_w[0], A_log[0], dt_bias[0], gdn_norm_w[0],
                  conv_w[0], conv_b[0], w_out[0], final_norm_w)
```

```python
import functools

import jax
import jax.numpy as jnp
from jax import lax
from jax.experimental import pallas as pl
from jax.experimental.pallas import tpu as pltpu

D_MODEL = 1024
HEADS = 8
HEAD_DIM = 128
GDN_WIDTH = HEADS * HEAD_DIM
CONV_WIDTH = 1024
GDN_CONV = 4
SHORT_CONV = 3
CHUNK = 64
EPS = 1e-6
LANES = 128
HALO = 8
ROW_STRIDE = 4

COL_Z = 3 * GDN_WIDTH
COL_CONV = COL_Z + GDN_WIDTH
COL_BETA = COL_CONV + 4 * CONV_WIDTH
COL_A = COL_BETA + LANES
PROJ_PACKED = COL_A + LANES
N_BA = 2 * HEADS
PACK_TN = 1024
PACK_STEPS = COL_BETA // PACK_TN + 1
WOUT_TM = 512
WOUT_STEPS = (GDN_WIDTH + CONV_WIDTH) // WOUT_TM
PRE_STEPS = PACK_STEPS + WOUT_STEPS

ROWS = 2
TLR = 128
TL = ROWS * TLR
VMEM_LIMIT = 60 * 1024 * 1024


def _bdot(a, b):
    return jnp.dot(a.astype(jnp.bfloat16), b.astype(jnp.bfloat16),
                   preferred_element_type=jnp.float32)


def _bdot_nt(a, b):
    return lax.dot_general(a.astype(jnp.bfloat16), b.astype(jnp.bfloat16),
                           (((1,), (1,)), ((), ())), preferred_element_type=jnp.float32)


def _bdot_tn(a, b):
    return lax.dot_general(a.astype(jnp.bfloat16), b.astype(jnp.bfloat16),
                           (((0,), (0,)), ((), ())), preferred_element_type=jnp.float32)


def _silu(x):
    return x * jax.nn.sigmoid(x)


def _rmsnorm(x, w):
    ms = jnp.mean(x * x, axis=-1, keepdims=True)
    return x * lax.rsqrt(ms + EPS) * w


def _fill_slabs(slab_ref, x):
    n = x.shape[1] // LANES
    for b in range(ROWS):
        for j in range(n):
            slab_ref[b * n + j, HALO:, :] = x[b * TLR:(b + 1) * TLR, j * LANES:(j + 1) * LANES]


def _conv_row_classes(slab_ref, j, w_ref, lane0, width):
    taps = [w_ref[width - 1 - s:width - s, lane0:lane0 + LANES] for s in range(width)]
    loaded = {}

    def rows_from(start):
        if start not in loaded:
            loaded[start] = slab_ref[j, pl.ds(start, TLR // ROW_STRIDE, stride=ROW_STRIDE), :]
        return loaded[start]

    out = []
    for r in range(ROW_STRIDE):
        acc = rows_from(HALO + r) * taps[0]
        for s in range(1, width):
            acc = acc + rows_from(HALO + r - s) * taps[s]
        out.append(acc)
    slab_ref[j, :HALO, :] = slab_ref[j, TLR:, :]
    return out


def _store_row_class(dst_ref, j, b, r, val):
    dst_ref[j, pl.ds(b * TLR + r, TLR // ROW_STRIDE, stride=ROW_STRIDE), :] = val


def _tile_body(first_tile, x_ref, nw_ref, wp_ref, cw_ref, alog_ref, dtb_ref, gnw_ref, ccw_ref,
               cb_ref, wo_ref, fnw_ref, out_ref,
               s_ref, pre_slab, ch_slab, cv_s, q_s, k_s, v_s, z_s, beta_s, gc_s, gct_s, betat_s,
               o_s):
    n_chunks = TL // CHUNK
    row_chunks = TLR // CHUNK
    heads = range(HEADS)
    lanes = [slice(h * HEAD_DIM, (h + 1) * HEAD_DIM) for h in heads]
    col = lambda blk, h: blk[:, h:h + 1]
    conv_tiles = CONV_WIDTH // LANES

    @pl.when(first_tile)
    def _():
        s_ref[...] = jnp.zeros_like(s_ref)
        pre_slab[:, :, :HALO, :] = jnp.zeros(pre_slab.shape[:2] + (HALO, LANES), jnp.float32)
        ch_slab[:, :HALO, :] = jnp.zeros((ch_slab.shape[0], HALO, LANES), jnp.float32)

    x = x_ref[...].reshape(TL, D_MODEL)
    hn = _rmsnorm(x, nw_ref[...]).astype(jnp.bfloat16)
    proj = lambda lo, hi: jnp.dot(hn, wp_ref[:, lo:hi], preferred_element_type=jnp.float32)
    conv_col = lambda j: proj(COL_CONV + j * CONV_WIDTH, COL_CONV + (j + 1) * CONV_WIDTH)

    ba = proj(COL_BETA, PROJ_PACKED)
    beta = jax.nn.sigmoid(ba[:, :LANES])
    beta_s[...] = beta
    a_in = ba[:, LANES:] + dtb_ref[...]
    softplus = jnp.maximum(a_in, 0.0) + jnp.log1p(jnp.exp(-jnp.abs(a_in)))
    gc = -jnp.exp(alog_ref[...]) * softplus
    pos = lax.broadcasted_iota(jnp.int32, gc.shape, 0) % CHUNK
    s = 1
    while s < CHUNK:
        gc = gc + jnp.where(pos >= s, pltpu.roll(gc, s, axis=0), 0.0)
        s *= 2
    gc_s[...] = gc
    for src, dst in ((gc.T, gct_s), (beta.T, betat_s)):
        for c in range(n_chunks):
            blk = src[:, c * CHUNK:(c + 1) * CHUNK]
            dst[c] = jnp.concatenate([blk, blk], axis=1)

    conv_in = []
    for grp, dst in enumerate((q_s, k_s, v_s)):
        col0 = grp * GDN_WIDTH
        slabs = pre_slab.at[grp]
        _fill_slabs(slabs, proj(col0, col0 + GDN_WIDTH))
        conv_in.append(conv_col(grp))
        scale = HEAD_DIM ** -0.5 if grp == 0 else 1.0
        for b in range(ROWS):
            for h in heads:
                classes = _conv_row_classes(slabs, b * HEADS + h, cw_ref, col0 + h * HEAD_DIM,
                                            GDN_CONV)
                for r, acc in enumerate(classes):
                    act = _silu(acc)
                    if grp < 2:
                        inv = lax.rsqrt(jnp.sum(act * act, axis=-1, keepdims=True) + EPS)
                        act = act * inv * scale
                    _store_row_class(dst, h, b, r, act)
    z_s[...] = proj(COL_Z, COL_CONV)
    gate_c = conv_col(3)

    gate_b, gate_cc, h_c = conv_in
    _fill_slabs(ch_slab, gate_cc * h_c)
    for b in range(ROWS):
        for j in range(conv_tiles):
            classes = _conv_row_classes(ch_slab, b * conv_tiles + j, ccw_ref, j * LANES,
                                        SHORT_CONV)
            for r, acc in enumerate(classes):
                _store_row_class(cv_s, j, b, r, acc)
    conv = jnp.concatenate([cv_s[j] for j in range(conv_tiles)], axis=1)
    y = (gate_b * (conv + cb_ref[...]) * _silu(gate_c)).astype(jnp.bfloat16)

    ri = lax.broadcasted_iota(jnp.int32, (CHUNK, 2 * CHUNK), 0)
    ci = lax.broadcasted_iota(jnp.int32, (CHUNK, 2 * CHUNK), 1)
    cj = ci % CHUNK
    causal2 = ri >= cj
    left = ci < CHUNK
    eye_right = jnp.where((ri == cj) & (ci >= CHUNK), 1.0, 0.0)
    keep_right = jnp.where(left, 0.0, 1.0)
    keep_left = jnp.where(left, 1.0, 0.0)
    off_diag = jnp.where(ri == cj, 0.0, 1.0)

    items = []
    for c in range(n_chunks):
        rows = slice(c * CHUNK, (c + 1) * CHUNK)
        gc_blk = gc_s[rows, :]
        eg_blk = jnp.exp(gc_blk)
        gct_blk = gct_s[c]
        betat_blk = betat_s[c]
        for h in heads:
            items.append((rows, h, betat_blk[h:h + 1, :], col(gc_blk, h), col(eg_blk, h),
                          gct_blk[h:h + 1, :]))
    n_items = range(len(items))
    k = [k_s[h, rows, :] for rows, h, *_ in items]
    q = [q_s[h, rows, :] for rows, h, *_ in items]
    decayb = [jnp.where(causal2, jnp.exp(it[3] - it[5]), 0.0) * it[2] for it in items]
    kq = [_bdot_nt(jnp.concatenate([k[n], q[n]], axis=0),
                   jnp.concatenate([k[n], k[n]], axis=0)) for n in n_items]
    attn_d = [kq[n][CHUNK:, :CHUNK] * decayb[n][:, :CHUNK] for n in n_items]
    qd = [q[n] * items[n][4] for n in n_items]
    a2 = [kq[n][:CHUNK] * decayb[n] * off_diag for n in n_items]
    prod = [_bdot(a[:, :CHUNK], a) for a in a2]
    pt = [prod[n] * keep_left + (eye_right - a2[n] * keep_right) for n in n_items]
    m = 4
    while m < CHUNK:
        prod = [_bdot(p[:, :CHUNK], p) for p in pt]
        pt = [prod[n] + pt[n] * keep_right for n in n_items]
        m *= 2
    prod = [_bdot(p[:, :CHUNK], p) for p in pt]
    t_pad = [pt[n] + prod[n] for n in n_items]
    rhs = [jnp.concatenate([v_s[it[1], it[0], :], k[n] * it[4]], axis=1)
           for n, it in enumerate(items)]
    no_rows = jnp.zeros((CHUNK, 2 * HEAD_DIM), jnp.float32)
    sol = [_bdot(t_pad[n], jnp.concatenate([no_rows, rhs[n]], axis=0))
           for n in n_items]

    chains = [(b, h) for b in range(ROWS) for h in heads]
    state = [s_ref[b * HEADS + h] for b, h in chains]
    for step in range(row_chunks):
        item = [(b * row_chunks + step) * HEADS + h for b, h in chains]
        rows = [slice((b * row_chunks + step) * CHUNK, (b * row_chunks + step + 1) * CHUNK)
                for b in range(ROWS)]
        gc_blk = [gc_s[r, :] for r in rows]
        g_last = [g[CHUNK - 1:CHUNK, :] for g in gc_blk]
        ekb_blk = [jnp.exp(g_last[b] - gc_blk[b]) * beta_s[rows[b], :] for b in range(ROWS)]
        egl_blk = [jnp.exp(g) for g in g_last]
        ws_qs = [_bdot(jnp.concatenate([sol[item[n]][:, HEAD_DIM:], qd[item[n]]], axis=0),
                       state[n]) for n in range(len(chains))]
        v_new = [sol[item[n]][:, :HEAD_DIM] - ws_qs[n][:CHUNK] for n in range(len(chains))]
        o = [ws_qs[n][CHUNK:] + _bdot(attn_d[item[n]], v_new[n]) for n in range(len(chains))]
        kv = [_bdot_tn(k[item[n]] * col(ekb_blk[b], h), v_new[n])
              for n, (b, h) in enumerate(chains)]
        state = [state[n] * col(egl_blk[b], h) + kv[n] for n, (b, h) in enumerate(chains)]
        for n, (b, h) in enumerate(chains):
            o_s[rows[b], lanes[h]] = o[n]
    for n, (b, h) in enumerate(chains):
        s_ref[b * HEADS + h] = state[n]

    res = x + jnp.dot(y, wo_ref[GDN_WIDTH:, :], preferred_element_type=jnp.float32)
    gated = jnp.concatenate(
        [_rmsnorm(o_s[:, lanes[h]], gnw_ref[...]) * _silu(z_s[:, lanes[h]]) for h in heads],
        axis=1)

    res = res + _bdot(gated, wo_ref[:GDN_WIDTH, :])
    out_ref[...] = _rmsnorm(res, fnw_ref[...]).reshape(ROWS, TLR, D_MODEL)


def _pack_block(j, wt_ref, mini_ref, wp_s):
    cols = slice(j * PACK_TN, (j + 1) * PACK_TN)
    if j < COL_CONV // PACK_TN:
        wp_s[:, cols] = wt_ref[...].T.astype(jnp.bfloat16)
    elif j < COL_BETA // PACK_TN:
        rows = jnp.concatenate([wt_ref[N_BA:, :], mini_ref[...]], axis=0)
        wp_s[:, cols] = rows.T.astype(jnp.bfloat16)
    else:
        pad = jnp.zeros((LANES - HEADS, D_MODEL), jnp.float32)
        rows = jnp.concatenate([mini_ref[:HEADS, :], pad, mini_ref[HEADS:, :], pad], axis=0)
        wp_s[:, COL_BETA:] = rows.T.astype(jnp.bfloat16)


def _block_kernel(x_ref, wt_ref, mini_ref, wout_ref, nw_ref, cw_ref, alog_ref, dtb_ref, gnw_ref,
                  ccw_ref, cb_ref, fnw_ref, out_ref, wp_s, wo_s, *scratch, tiles):
    step = pl.program_id(0)
    for j in range(PACK_STEPS):
        pl.when(step == j)(functools.partial(_pack_block, j, wt_ref, mini_ref, wp_s))
    for j in range(WOUT_STEPS):
        @pl.when(step == PACK_STEPS + j)
        def _(j=j):
            wo_s[j * WOUT_TM:(j + 1) * WOUT_TM, :] = wout_ref[...].astype(jnp.bfloat16)

    @pl.when(step >= PRE_STEPS)
    def _():
        _tile_body(lax.rem(step - PRE_STEPS, tiles) == 0, x_ref, nw_ref, wp_s, cw_ref, alog_ref,
                   dtb_ref, gnw_ref, ccw_ref, cb_ref, wo_s, fnw_ref, out_ref, *scratch)


def _block(x, wt, w_out, norm_w, conv_qkv_w, a_log_pad, dt_bias_pad, gdn_norm_w, conv_w, conv_b,
           out_norm_w):
    batch, seq, _ = x.shape
    tiles = seq // TLR
    n_chunks = TL // CHUNK
    conv_tiles = CONV_WIDTH // LANES
    last_pack = PACK_STEPS - 1
    mini_per_block = PACK_TN // N_BA

    def tile(i):
        j = jnp.maximum(i - PRE_STEPS, 0)
        return (j // tiles, j % tiles, 0)

    mini = lambda i: (jnp.where(i >= last_pack, COL_CONV // N_BA,
                                (i + 1) * mini_per_block), 0)
    const = lambda i: (0, 0)
    resident = lambda shape: pl.BlockSpec(shape, const, pipeline_mode=pl.Buffered(1))
    tile_f32 = pltpu.VMEM((TL, GDN_WIDTH), jnp.float32)
    head_tile_f32 = pltpu.VMEM((HEADS, TL, HEAD_DIM), jnp.float32)
    return pl.pallas_call(
        functools.partial(_block_kernel, tiles=tiles),
        grid=(PRE_STEPS + (batch // ROWS) * tiles,),
        in_specs=[
            pl.BlockSpec((ROWS, TLR, D_MODEL), tile),
            pl.BlockSpec((PACK_TN, D_MODEL), lambda i: (jnp.minimum(i, last_pack - 1), 0)),
            pl.BlockSpec((N_BA, D_MODEL), mini),
            pl.BlockSpec((WOUT_TM, D_MODEL),
                         lambda i: (jnp.clip(i - PACK_STEPS, 0, WOUT_STEPS - 1), 0)),
            resident((1, D_MODEL)),
            resident((GDN_CONV, 3 * GDN_WIDTH)),
            resident((1, LANES)),
            resident((1, LANES)),
            resident((1, HEAD_DIM)),
            resident((SHORT_CONV, CONV_WIDTH)),
            resident((1, CONV_WIDTH)),
            resident((1, D_MODEL)),
        ],
        out_specs=pl.BlockSpec((ROWS, TLR, D_MODEL), tile),
        out_shape=jax.ShapeDtypeStruct(x.shape, jnp.float32),
        scratch_shapes=[
            pltpu.VMEM((D_MODEL, PROJ_PACKED), jnp.bfloat16),
            pltpu.VMEM((GDN_WIDTH + CONV_WIDTH, D_MODEL), jnp.bfloat16),
            pltpu.VMEM((ROWS * HEADS, HEAD_DIM, HEAD_DIM), jnp.float32),
            pltpu.VMEM((3, ROWS * HEADS, HALO + TLR, LANES), jnp.float32),
            pltpu.VMEM((ROWS * conv_tiles, HALO + TLR, LANES), jnp.float32),
            pltpu.VMEM((conv_tiles, TL, LANES), jnp.float32),
            head_tile_f32, head_tile_f32, head_tile_f32,
            tile_f32,
            pltpu.VMEM((TL, LANES), jnp.float32),
            pltpu.VMEM((TL, LANES), jnp.float32),
            pltpu.VMEM((n_chunks, LANES, 2 * CHUNK), jnp.float32),
            pltpu.VMEM((n_chunks, LANES, 2 * CHUNK), jnp.float32),
            tile_f32,
        ],
        compiler_params=pltpu.CompilerParams(
            dimension_semantics=("arbitrary",), vmem_limit_bytes=VMEM_LIMIT),
        name="hybrid_block",
    )(x, wt, wt, w_out, norm_w, conv_qkv_w, a_log_pad, dt_bias_pad, gdn_norm_w, conv_w, conv_b,
      out_norm_w)


def _pad_lanes(v):
    return jnp.pad(v, ((0, 0), (0, LANES - v.shape[-1])))


def _layer(x, norm_in_w, w_in, conv_qkv_w, a_log, dt_bias, gdn_norm_w, conv_w, conv_b, w_out,
           out_norm_w):
    assert w_in.shape[1] == COL_BETA + N_BA
    return _block(x, w_in.T, w_out, norm_in_w.reshape(1, -1), conv_qkv_w,
                  _pad_lanes(a_log.reshape(1, -1)), _pad_lanes(dt_bias.reshape(1, -1)),
                  gdn_norm_w.reshape(1, -1), conv_w, conv_b.reshape(1, -1),
                  out_norm_w.reshape(1, -1))


def kernel(x, norm_in_w, w_in, conv_qkv_w, A_log, dt_bias, gdn_norm_w, conv_w, conv_b, w_out,
           final_norm_w):
    batch, seq, d_model = x.shape
    assert norm_in_w.shape[0] == 1 and d_model == D_MODEL
    assert batch % ROWS == 0 and seq % TLR == 0 and TLR % CHUNK == 0
    return _layer(x, norm_in_w[0], w_in[0], conv_qkv_w[0], A_log[0], dt_bias[0], gdn_norm_w[0],
                  conv_w[0], conv_b[0], w_out[0], final_norm_w)
```

```python
import functools

import jax
import jax.numpy as jnp
from jax import lax
from jax.experimental import pallas as pl
from jax.experimental.pallas import tpu as pltpu

D_MODEL = 1024
HEADS = 8
HEAD_DIM = 128
GDN_WIDTH = HEADS * HEAD_DIM
CONV_WIDTH = 1024
GDN_CONV = 4
SHORT_CONV = 3
CHUNK = 64
EPS = 1e-6
LANES = 128
HALO = 8
ROW_STRIDE = 4

COL_Z = 3 * GDN_WIDTH
COL_CONV = COL_Z + GDN_WIDTH
COL_BETA = COL_CONV + 4 * CONV_WIDTH
COL_A = COL_BETA + LANES
PROJ_PACKED = COL_A + LANES
N_BA = 2 * HEADS
PACK_TN = 1024
PACK_STEPS = COL_BETA // PACK_TN + 1
WOUT_TM = 512
WOUT_STEPS = (GDN_WIDTH + CONV_WIDTH) // WOUT_TM
PRE_STEPS = PACK_STEPS + WOUT_STEPS

ROWS = 4
TLR = 64
TL = ROWS * TLR
VMEM_LIMIT = 60 * 1024 * 1024


def _bdot(a, b):
    return jnp.dot(a.astype(jnp.bfloat16), b.astype(jnp.bfloat16),
                   preferred_element_type=jnp.float32)


def _bdot_nt(a, b):
    return lax.dot_general(a.astype(jnp.bfloat16), b.astype(jnp.bfloat16),
                           (((1,), (1,)), ((), ())), preferred_element_type=jnp.float32)


def _bdot_tn(a, b):
    return lax.dot_general(a.astype(jnp.bfloat16), b.astype(jnp.bfloat16),
                           (((0,), (0,)), ((), ())), preferred_element_type=jnp.float32)


def _silu(x):
    return x * jax.nn.sigmoid(x)


def _rmsnorm(x, w):
    ms = jnp.mean(x * x, axis=-1, keepdims=True)
    return x * lax.rsqrt(ms + EPS) * w


def _fill_slabs(slab_ref, x):
    n = x.shape[1] // LANES
    for b in range(ROWS):
        for j in range(n):
            slab_ref[b * n + j, HALO:, :] = x[b * TLR:(b + 1) * TLR, j * LANES:(j + 1) * LANES]


def _conv_row_classes(slab_ref, j, w_ref, lane0, width):
    taps = [w_ref[width - 1 - s:width - s, lane0:lane0 + LANES] for s in range(width)]
    loaded = {}

    def rows_from(start):
        if start not in loaded:
            loaded[start] = slab_ref[j, pl.ds(start, TLR // ROW_STRIDE, stride=ROW_STRIDE), :]
        return loaded[start]

    out = []
    for r in range(ROW_STRIDE):
        acc = rows_from(HALO + r) * taps[0]
        for s in range(1, width):
            acc = acc + rows_from(HALO + r - s) * taps[s]
        out.append(acc)
    slab_ref[j, :HALO, :] = slab_ref[j, TLR:, :]
    return out


def _store_row_class(dst_ref, j, b, r, val):
    dst_ref[j, pl.ds(b * TLR + r, TLR // ROW_STRIDE, stride=ROW_STRIDE), :] = val


def _tile_body(first_tile, x_ref, nw_ref, wp_ref, cw_ref, alog_ref, dtb_ref, gnw_ref, ccw_ref,
               cb_ref, wo_ref, fnw_ref, out_ref,
               s_ref, pre_slab, ch_slab, cv_s, q_s, k_s, v_s, z_s, beta_s, gc_s, gct_s, betat_s,
               o_s):
    n_chunks = TL // CHUNK
    row_chunks = TLR // CHUNK
    heads = range(HEADS)
    lanes = [slice(h * HEAD_DIM, (h + 1) * HEAD_DIM) for h in heads]
    col = lambda blk, h: blk[:, h:h + 1]
    conv_tiles = CONV_WIDTH // LANES

    @pl.when(first_tile)
    def _():
        s_ref[...] = jnp.zeros_like(s_ref)
        pre_slab[:, :, :HALO, :] = jnp.zeros(pre_slab.shape[:2] + (HALO, LANES), jnp.float32)
        ch_slab[:, :HALO, :] = jnp.zeros((ch_slab.shape[0], HALO, LANES), jnp.float32)

    x = x_ref[...].reshape(TL, D_MODEL)
    hn = _rmsnorm(x, nw_ref[...]).astype(jnp.bfloat16)
    proj = lambda lo, hi: jnp.dot(hn, wp_ref[:, lo:hi], preferred_element_type=jnp.float32)
    conv_col = lambda j: proj(COL_CONV + j * CONV_WIDTH, COL_CONV + (j + 1) * CONV_WIDTH)

    ba = proj(COL_BETA, PROJ_PACKED)
    beta = jax.nn.sigmoid(ba[:, :LANES])
    beta_s[...] = beta
    a_in = ba[:, LANES:] + dtb_ref[...]
    softplus = jnp.maximum(a_in, 0.0) + jnp.log1p(jnp.exp(-jnp.abs(a_in)))
    gc = -jnp.exp(alog_ref[...]) * softplus
    pos = lax.broadcasted_iota(jnp.int32, gc.shape, 0) % CHUNK
    s = 1
    while s < CHUNK:
        gc = gc + jnp.where(pos >= s, pltpu.roll(gc, s, axis=0), 0.0)
        s *= 2
    gc_s[...] = gc
    for src, dst in ((gc.T, gct_s), (beta.T, betat_s)):
        for c in range(n_chunks):
            blk = src[:, c * CHUNK:(c + 1) * CHUNK]
            dst[c] = jnp.concatenate([blk, blk], axis=1)

    conv_in = []
    for grp, dst in enumerate((q_s, k_s, v_s)):
        col0 = grp * GDN_WIDTH
        slabs = pre_slab.at[grp]
        _fill_slabs(slabs, proj(col0, col0 + GDN_WIDTH))
        conv_in.append(conv_col(grp))
        scale = HEAD_DIM ** -0.5 if grp == 0 else 1.0
        for b in range(ROWS):
            for h in heads:
                classes = _conv_row_classes(slabs, b * HEADS + h, cw_ref, col0 + h * HEAD_DIM,
                                            GDN_CONV)
                for r, acc in enumerate(classes):
                    act = _silu(acc)
                    if grp < 2:
                        inv = lax.rsqrt(jnp.sum(act * act, axis=-1, keepdims=True) + EPS)
                        act = act * inv * scale
                    _store_row_class(dst, h, b, r, act)
    z_s[...] = proj(COL_Z, COL_CONV)
    gate_c = conv_col(3)

    gate_b, gate_cc, h_c = conv_in
    _fill_slabs(ch_slab, gate_cc * h_c)
    for b in range(ROWS):
        for j in range(conv_tiles):
            classes = _conv_row_classes(ch_slab, b * conv_tiles + j, ccw_ref, j * LANES,
                                        SHORT_CONV)
            for r, acc in enumerate(classes):
                _store_row_class(cv_s, j, b, r, acc)
    conv = jnp.concatenate([cv_s[j] for j in range(conv_tiles)], axis=1)
    y = (gate_b * (conv + cb_ref[...]) * _silu(gate_c)).astype(jnp.bfloat16)

    ri = lax.broadcasted_iota(jnp.int32, (CHUNK, 2 * CHUNK), 0)
    ci = lax.broadcasted_iota(jnp.int32, (CHUNK, 2 * CHUNK), 1)
    cj = ci % CHUNK
    causal2 = ri >= cj
    left = ci < CHUNK
    eye_right = jnp.where((ri == cj) & (ci >= CHUNK), 1.0, 0.0)
    keep_right = jnp.where(left, 0.0, 1.0)
    keep_left = jnp.where(left, 1.0, 0.0)
    off_diag = jnp.where(ri == cj, 0.0, 1.0)

    items = []
    for c in range(n_chunks):
        rows = slice(c * CHUNK, (c + 1) * CHUNK)
        gc_blk = gc_s[rows, :]
        eg_blk = jnp.exp(gc_blk)
        gct_blk = gct_s[c]
        betat_blk = betat_s[c]
        for h in heads:
            items.append((rows, h, betat_blk[h:h + 1, :], col(gc_blk, h), col(eg_blk, h),
                          gct_blk[h:h + 1, :]))
    phase1 = {}

    def solve_stages(sub):
        k = {n: k_s[items[n][1], items[n][0], :] for n in sub}
        q = {n: q_s[items[n][1], items[n][0], :] for n in sub}
        decayb = {n: jnp.where(causal2, jnp.exp(items[n][3] - items[n][5]), 0.0) * items[n][2]
                  for n in sub}
        kq = {n: _bdot_nt(jnp.concatenate([k[n], q[n]], axis=0),
                          jnp.concatenate([k[n], k[n]], axis=0)) for n in sub}
        yield
        attn_d = {n: kq[n][CHUNK:, :CHUNK] * decayb[n][:, :CHUNK] for n in sub}
        qd = {n: q[n] * items[n][4] for n in sub}
        a2 = {n: kq[n][:CHUNK] * decayb[n] * off_diag for n in sub}
        prod = {n: _bdot(a2[n][:, :CHUNK], a2[n]) for n in sub}
        yield
        pt = {n: prod[n] * keep_left + (eye_right - a2[n] * keep_right) for n in sub}
        m = 4
        while m < CHUNK:
            prod = {n: _bdot(pt[n][:, :CHUNK], pt[n]) for n in sub}
            yield
            pt = {n: prod[n] + pt[n] * keep_right for n in sub}
            m *= 2
        prod = {n: _bdot(pt[n][:, :CHUNK], pt[n]) for n in sub}
        yield
        t_pad = {n: pt[n] + prod[n] for n in sub}
        rhs = {n: jnp.concatenate([v_s[items[n][1], items[n][0], :], k[n] * items[n][4]], axis=1)
               for n in sub}
        no_rows = jnp.zeros((CHUNK, 2 * HEAD_DIM), jnp.float32)
        sol = {n: _bdot(t_pad[n], jnp.concatenate([no_rows, rhs[n]], axis=0)) for n in sub}
        yield
        for n in sub:
            phase1[n] = (k[n], qd[n], attn_d[n], sol[n])

    def recurrence_stages(batch_rows):
        chains = [(b, h) for b in batch_rows for h in heads]
        state = [s_ref[b * HEADS + h] for b, h in chains]
        for step in range(row_chunks):
            p1 = [phase1[(b * row_chunks + step) * HEADS + h] for b, h in chains]
            rows = {b: slice((b * row_chunks + step) * CHUNK, (b * row_chunks + step + 1) * CHUNK)
                    for b in batch_rows}
            gc_blk = {b: gc_s[rows[b], :] for b in batch_rows}
            g_last = {b: gc_blk[b][CHUNK - 1:CHUNK, :] for b in batch_rows}
            ekb_blk = {b: jnp.exp(g_last[b] - gc_blk[b]) * beta_s[rows[b], :] for b in batch_rows}
            egl_blk = {b: jnp.exp(g_last[b]) for b in batch_rows}
            ws_qs = [_bdot(jnp.concatenate([p1[n][3][:, HEAD_DIM:], p1[n][1]], axis=0), state[n])
                     for n in range(len(chains))]
            yield
            v_new = [p1[n][3][:, :HEAD_DIM] - ws_qs[n][:CHUNK] for n in range(len(chains))]
            o = [ws_qs[n][CHUNK:] + _bdot(p1[n][2], v_new[n]) for n in range(len(chains))]
            kv = [_bdot_tn(p1[n][0] * col(ekb_blk[b], h), v_new[n])
                  for n, (b, h) in enumerate(chains)]
            state = [state[n] * col(egl_blk[b], h) + kv[n] for n, (b, h) in enumerate(chains)]
            for n, (b, h) in enumerate(chains):
                o_s[rows[b], lanes[h]] = o[n]
            yield
        for n, (b, h) in enumerate(chains):
            s_ref[b * HEADS + h] = state[n]

    rows_a, rows_b = range(ROWS // 2), range(ROWS // 2, ROWS)
    items_of = lambda rws: [(b * row_chunks + c) * HEADS + h
                            for b in rws for c in range(row_chunks) for h in heads]
    for _ in solve_stages(items_of(rows_a)):
        pass
    recur_a = recurrence_stages(rows_a)
    for i, _ in enumerate(solve_stages(items_of(rows_b))):
        if i % 2 == 0:
            next(recur_a, None)
    for _ in recur_a:
        pass
    for _ in recurrence_stages(rows_b):
        pass

    res = x + jnp.dot(y, wo_ref[GDN_WIDTH:, :], preferred_element_type=jnp.float32)
    gated = jnp.concatenate(
        [_rmsnorm(o_s[:, lanes[h]], gnw_ref[...]) * _silu(z_s[:, lanes[h]]) for h in heads],
        axis=1)

    res = res + _bdot(gated, wo_ref[:GDN_WIDTH, :])
    out_ref[...] = _rmsnorm(res, fnw_ref[...]).reshape(ROWS, TLR, D_MODEL)


def _pack_block(j, wt_ref, mini_ref, wp_s):
    cols = slice(j * PACK_TN, (j + 1) * PACK_TN)
    if j < COL_CONV // PACK_TN:
        wp_s[:, cols] = wt_ref[...].T.astype(jnp.bfloat16)
    elif j < COL_BETA // PACK_TN:
        rows = jnp.concatenate([wt_ref[N_BA:, :], mini_ref[...]], axis=0)
        wp_s[:, cols] = rows.T.astype(jnp.bfloat16)
    else:
        pad = jnp.zeros((LANES - HEADS, D_MODEL), jnp.float32)
        rows = jnp.concatenate([mini_ref[:HEADS, :], pad, mini_ref[HEADS:, :], pad], axis=0)
        wp_s[:, COL_BETA:] = rows.T.astype(jnp.bfloat16)


def _block_kernel(x_ref, wt_ref, mini_ref, wout_ref, nw_ref, cw_ref, alog_ref, dtb_ref, gnw_ref,
                  ccw_ref, cb_ref, fnw_ref, out_ref, wp_s, wo_s, *scratch, tiles):
    step = pl.program_id(0)
    for j in range(PACK_STEPS):
        pl.when(step == j)(functools.partial(_pack_block, j, wt_ref, mini_ref, wp_s))
    for j in range(WOUT_STEPS):
        @pl.when(step == PACK_STEPS + j)
        def _(j=j):
            wo_s[j * WOUT_TM:(j + 1) * WOUT_TM, :] = wout_ref[...].astype(jnp.bfloat16)

    @pl.when(step >= PRE_STEPS)
    def _():
        _tile_body(lax.rem(step - PRE_STEPS, tiles) == 0, x_ref, nw_ref, wp_s, cw_ref, alog_ref,
                   dtb_ref, gnw_ref, ccw_ref, cb_ref, wo_s, fnw_ref, out_ref, *scratch)


def _block(x, wt, w_out, norm_w, conv_qkv_w, a_log_pad, dt_bias_pad, gdn_norm_w, conv_w, conv_b,
           out_norm_w):
    batch, seq, _ = x.shape
    tiles = seq // TLR
    n_chunks = TL // CHUNK
    conv_tiles = CONV_WIDTH // LANES
    last_pack = PACK_STEPS - 1
    mini_per_block = PACK_TN // N_BA

    def tile(i):
        j = jnp.maximum(i - PRE_STEPS, 0)
        return (j // tiles, j % tiles, 0)

    mini = lambda i: (jnp.where(i >= last_pack, COL_CONV // N_BA,
                                (i + 1) * mini_per_block), 0)
    const = lambda i: (0, 0)
    resident = lambda shape: pl.BlockSpec(shape, const, pipeline_mode=pl.Buffered(1))
    tile_f32 = pltpu.VMEM((TL, GDN_WIDTH), jnp.float32)
    head_tile_f32 = pltpu.VMEM((HEADS, TL, HEAD_DIM), jnp.float32)
    return pl.pallas_call(
        functools.partial(_block_kernel, tiles=tiles),
        grid=(PRE_STEPS + (batch // ROWS) * tiles,),
        in_specs=[
            pl.BlockSpec((ROWS, TLR, D_MODEL), tile),
            pl.BlockSpec((PACK_TN, D_MODEL), lambda i: (jnp.minimum(i, last_pack - 1), 0)),
            pl.BlockSpec((N_BA, D_MODEL), mini),
            pl.BlockSpec((WOUT_TM, D_MODEL),
                         lambda i: (jnp.clip(i - PACK_STEPS, 0, WOUT_STEPS - 1), 0)),
            resident((1, D_MODEL)),
            resident((GDN_CONV, 3 * GDN_WIDTH)),
            resident((1, LANES)),
            resident((1, LANES)),
            resident((1, HEAD_DIM)),
            resident((SHORT_CONV, CONV_WIDTH)),
            resident((1, CONV_WIDTH)),
            resident((1, D_MODEL)),
        ],
        out_specs=pl.BlockSpec((ROWS, TLR, D_MODEL), tile),
        out_shape=jax.ShapeDtypeStruct(x.shape, jnp.float32),
        scratch_shapes=[
            pltpu.VMEM((D_MODEL, PROJ_PACKED), jnp.bfloat16),
            pltpu.VMEM((GDN_WIDTH + CONV_WIDTH, D_MODEL), jnp.bfloat16),
            pltpu.VMEM((ROWS * HEADS, HEAD_DIM, HEAD_DIM), jnp.float32),
            pltpu.VMEM((3, ROWS * HEADS, HALO + TLR, LANES), jnp.float32),
            pltpu.VMEM((ROWS * conv_tiles, HALO + TLR, LANES), jnp.float32),
            pltpu.VMEM((conv_tiles, TL, LANES), jnp.float32),
            head_tile_f32, head_tile_f32, head_tile_f32,
            tile_f32,
            pltpu.VMEM((TL, LANES), jnp.float32),
            pltpu.VMEM((TL, LANES), jnp.float32),
            pltpu.VMEM((n_chunks, LANES, 2 * CHUNK), jnp.float32),
            pltpu.VMEM((n_chunks, LANES, 2 * CHUNK), jnp.float32),
            tile_f32,
        ],
        compiler_params=pltpu.CompilerParams(
            dimension_semantics=("arbitrary",), vmem_limit_bytes=VMEM_LIMIT),
        name="hybrid_block",
    )(x, wt, wt, w_out, norm_w, conv_qkv_w, a_log_pad, dt_bias_pad, gdn_norm_w, conv_w, conv_b,
      out_norm_w)


def _pad_lanes(v):
    return jnp.pad(v, ((0, 0), (0, LANES - v.shape[-1])))


def _layer(x, norm_in_w, w_in, conv_qkv_w, a_log, dt_bias, gdn_norm_w, conv_w, conv_b, w_out,
           out_norm_w):
    assert w_in.shape[1] == COL_BETA + N_BA
    return _block(x, w_in.T, w_out, norm_in_w.reshape(1, -1), conv_qkv_w,
                  _pad_lanes(a_log.reshape(1, -1)), _pad_lanes(dt_bias.reshape(1, -1)),
                  gdn_norm_w.reshape(1, -1), conv_w, conv_b.reshape(1, -1),
                  out_norm_w.reshape(1, -1))


def kernel(x, norm_in_w, w_in, conv_qkv_w, A_log, dt_bias, gdn_norm_w, conv_w, conv_b, w_out,
           final_norm_w):
    batch, seq, d_model = x.shape
    assert norm_in_w.shape[0] == 1 and d_model == D_MODEL
    assert batch % ROWS == 0 and seq % TLR == 0 and TLR % CHUNK == 0
    return _layer(x, norm_in_w[0], w_in[0], conv_qkv_w[0], A_log[0], dt_bias[0], gdn_norm_w[0],
                  conv_w[0], conv_b[0], w_out[0], final_norm_w)
```

```python
import functools

import jax
import jax.numpy as jnp
from jax import lax
from jax.experimental import pallas as pl
from jax.experimental.pallas import tpu as pltpu

D_MODEL = 1024
HEADS = 8
HEAD_DIM = 128
GDN_WIDTH = HEADS * HEAD_DIM
CONV_WIDTH = 1024
GDN_CONV = 4
SHORT_CONV = 3
CHUNK = 64
EPS = 1e-6
LANES = 128
HALO = 8
ROW_STRIDE = 4

COL_Z = 3 * GDN_WIDTH
COL_CONV = COL_Z + GDN_WIDTH
COL_BETA = COL_CONV + 4 * CONV_WIDTH
COL_A = COL_BETA + LANES
PROJ_PACKED = COL_A + LANES
N_BA = 2 * HEADS
PACK_TN = 1024
PACK_STEPS = COL_BETA // PACK_TN + 1
WOUT_TM = 512
WOUT_STEPS = (GDN_WIDTH + CONV_WIDTH) // WOUT_TM
PRE_STEPS = PACK_STEPS + WOUT_STEPS

ROWS = 4
TLR = 64
TL = ROWS * TLR
VMEM_LIMIT = 60 * 1024 * 1024


def _bdot(a, b):
    return jnp.dot(a.astype(jnp.bfloat16), b.astype(jnp.bfloat16),
                   preferred_element_type=jnp.float32)


def _bdot_nt(a, b):
    return lax.dot_general(a.astype(jnp.bfloat16), b.astype(jnp.bfloat16),
                           (((1,), (1,)), ((), ())), preferred_element_type=jnp.float32)


def _bdot_tn(a, b):
    return lax.dot_general(a.astype(jnp.bfloat16), b.astype(jnp.bfloat16),
                           (((0,), (0,)), ((), ())), preferred_element_type=jnp.float32)


def _silu(x):
    return x * jax.nn.sigmoid(x)


def _rmsnorm(x, w):
    ms = jnp.mean(x * x, axis=-1, keepdims=True)
    return x * lax.rsqrt(ms + EPS) * w


def _fill_slabs(slab_ref, x):
    n = x.shape[1] // LANES
    for b in range(ROWS):
        for j in range(n):
            slab_ref[b * n + j, HALO:, :] = x[b * TLR:(b + 1) * TLR, j * LANES:(j + 1) * LANES]


def _conv_row_classes(slab_ref, j, w_ref, lane0, width):
    taps = [w_ref[width - 1 - s:width - s, lane0:lane0 + LANES] for s in range(width)]
    loaded = {}

    def rows_from(start):
        if start not in loaded:
            loaded[start] = slab_ref[j, pl.ds(start, TLR // ROW_STRIDE, stride=ROW_STRIDE), :]
        return loaded[start]

    out = []
    for r in range(ROW_STRIDE):
        acc = rows_from(HALO + r) * taps[0]
        for s in range(1, width):
            acc = acc + rows_from(HALO + r - s) * taps[s]
        out.append(acc)
    slab_ref[j, :HALO, :] = slab_ref[j, TLR:, :]
    return out


def _store_row_class(dst_ref, j, b, r, val):
    dst_ref[j, pl.ds(b * TLR + r, TLR // ROW_STRIDE, stride=ROW_STRIDE), :] = val


def _tile_body(first_tile, x_ref, nw_ref, wp_ref, cw_ref, alog_ref, dtb_ref, gnw_ref, ccw_ref,
               cb_ref, wo_ref, fnw_ref, out_ref,
               s_ref, pre_slab, ch_slab, cv_s, q_s, k_s, v_s, z_s, beta_s, gc_s, gct_s, betat_s,
               o_s):
    n_chunks = TL // CHUNK
    row_chunks = TLR // CHUNK
    heads = range(HEADS)
    lanes = [slice(h * HEAD_DIM, (h + 1) * HEAD_DIM) for h in heads]
    col = lambda blk, h: blk[:, h:h + 1]
    conv_tiles = CONV_WIDTH // LANES

    @pl.when(first_tile)
    def _():
        s_ref[...] = jnp.zeros_like(s_ref)
        pre_slab[:, :, :HALO, :] = jnp.zeros(pre_slab.shape[:2] + (HALO, LANES), jnp.float32)
        ch_slab[:, :HALO, :] = jnp.zeros((ch_slab.shape[0], HALO, LANES), jnp.float32)

    x = x_ref[...].reshape(TL, D_MODEL)
    hn = _rmsnorm(x, nw_ref[...]).astype(jnp.bfloat16)
    proj = lambda lo, hi: jnp.dot(hn, wp_ref[:, lo:hi], preferred_element_type=jnp.float32)
    conv_col = lambda j: proj(COL_CONV + j * CONV_WIDTH, COL_CONV + (j + 1) * CONV_WIDTH)

    ba = proj(COL_BETA, PROJ_PACKED)
    beta = jax.nn.sigmoid(ba[:, :LANES])
    beta_s[...] = beta
    a_in = ba[:, LANES:] + dtb_ref[...]
    softplus = jnp.maximum(a_in, 0.0) + jnp.log1p(jnp.exp(-jnp.abs(a_in)))
    gc = -jnp.exp(alog_ref[...]) * softplus
    pos = lax.broadcasted_iota(jnp.int32, gc.shape, 0) % CHUNK
    s = 1
    while s < CHUNK:
        gc = gc + jnp.where(pos >= s, pltpu.roll(gc, s, axis=0), 0.0)
        s *= 2
    gc_s[...] = gc
    for src, dst in ((gc.T, gct_s), (beta.T, betat_s)):
        for c in range(n_chunks):
            blk = src[:, c * CHUNK:(c + 1) * CHUNK]
            dst[c] = jnp.concatenate([blk, blk], axis=1)

    conv_in = []
    for grp, dst in enumerate((q_s, k_s, v_s)):
        col0 = grp * GDN_WIDTH
        slabs = pre_slab.at[grp]
        _fill_slabs(slabs, proj(col0, col0 + GDN_WIDTH))
        conv_in.append(conv_col(grp))
        scale = HEAD_DIM ** -0.5 if grp == 0 else 1.0
        for b in range(ROWS):
            for h in heads:
                classes = _conv_row_classes(slabs, b * HEADS + h, cw_ref, col0 + h * HEAD_DIM,
                                            GDN_CONV)
                for r, acc in enumerate(classes):
                    act = _silu(acc)
                    if grp < 2:
                        inv = lax.rsqrt(jnp.sum(act * act, axis=-1, keepdims=True) + EPS)
                        act = act * inv * scale
                    _store_row_class(dst, h, b, r, act)
    z_s[...] = proj(COL_Z, COL_CONV)
    gate_c = conv_col(3)

    gate_b, gate_cc, h_c = conv_in
    _fill_slabs(ch_slab, gate_cc * h_c)
    for b in range(ROWS):
        for j in range(conv_tiles):
            classes = _conv_row_classes(ch_slab, b * conv_tiles + j, ccw_ref, j * LANES,
                                        SHORT_CONV)
            for r, acc in enumerate(classes):
                _store_row_class(cv_s, j, b, r, acc)
    conv = jnp.concatenate([cv_s[j] for j in range(conv_tiles)], axis=1)
    y = (gate_b * (conv + cb_ref[...]) * _silu(gate_c)).astype(jnp.bfloat16)

    ri = lax.broadcasted_iota(jnp.int32, (CHUNK, 2 * CHUNK), 0)
    ci = lax.broadcasted_iota(jnp.int32, (CHUNK, 2 * CHUNK), 1)
    cj = ci % CHUNK
    causal2 = ri >= cj
    left = ci < CHUNK
    eye_right = jnp.where((ri == cj) & (ci >= CHUNK), 1.0, 0.0)
    keep_right = jnp.where(left, 0.0, 1.0)
    keep_left = jnp.where(left, 1.0, 0.0)
    off_diag = jnp.where(ri == cj, 0.0, 1.0)

    items = []
    for c in range(n_chunks):
        rows = slice(c * CHUNK, (c + 1) * CHUNK)
        gc_blk = gc_s[rows, :]
        eg_blk = jnp.exp(gc_blk)
        gct_blk = gct_s[c]
        betat_blk = betat_s[c]
        for h in heads:
            items.append((rows, h, betat_blk[h:h + 1, :], col(gc_blk, h), col(eg_blk, h),
                          gct_blk[h:h + 1, :]))
    n_items = range(len(items))
    k = [k_s[h, rows, :] for rows, h, *_ in items]
    q = [q_s[h, rows, :] for rows, h, *_ in items]
    decayb = [jnp.where(causal2, jnp.exp(it[3] - it[5]), 0.0) * it[2] for it in items]
    kq = [_bdot_nt(jnp.concatenate([k[n], q[n]], axis=0),
                   jnp.concatenate([k[n], k[n]], axis=0)) for n in n_items]
    attn_d = [kq[n][CHUNK:, :CHUNK] * decayb[n][:, :CHUNK] for n in n_items]
    qd = [q[n] * items[n][4] for n in n_items]
    a2 = [kq[n][:CHUNK] * decayb[n] * off_diag for n in n_items]
    prod = [_bdot(a[:, :CHUNK], a) for a in a2]
    pt = [prod[n] * keep_left + (eye_right - a2[n] * keep_right) for n in n_items]
    m = 4
    while m < CHUNK:
        prod = [_bdot(p[:, :CHUNK], p) for p in pt]
        pt = [prod[n] + pt[n] * keep_right for n in n_items]
        m *= 2
    prod = [_bdot(p[:, :CHUNK], p) for p in pt]
    t_pad = [pt[n] + prod[n] for n in n_items]
    rhs = [jnp.concatenate([v_s[it[1], it[0], :], k[n] * it[4]], axis=1)
           for n, it in enumerate(items)]
    no_rows = jnp.zeros((CHUNK, 2 * HEAD_DIM), jnp.float32)
    sol = [_bdot(t_pad[n], jnp.concatenate([no_rows, rhs[n]], axis=0))
           for n in n_items]

    chains = [(b, h) for b in range(ROWS) for h in heads]
    state = [s_ref[b * HEADS + h] for b, h in chains]
    for step in range(row_chunks):
        item = [(b * row_chunks + step) * HEADS + h for b, h in chains]
        rows = [slice((b * row_chunks + step) * CHUNK, (b * row_chunks + step + 1) * CHUNK)
                for b in range(ROWS)]
        gc_blk = [gc_s[r, :] for r in rows]
        g_last = [g[CHUNK - 1:CHUNK, :] for g in gc_blk]
        ekb_blk = [jnp.exp(g_last[b] - gc_blk[b]) * beta_s[rows[b], :] for b in range(ROWS)]
        egl_blk = [jnp.exp(g) for g in g_last]
        ws_qs = [_bdot(jnp.concatenate([sol[item[n]][:, HEAD_DIM:], qd[item[n]]], axis=0),
                       state[n]) for n in range(len(chains))]
        v_new = [sol[item[n]][:, :HEAD_DIM] - ws_qs[n][:CHUNK] for n in range(len(chains))]
        okv = [_bdot(jnp.concatenate([attn_d[item[n]].astype(jnp.bfloat16),
                                      (k[item[n]] * col(ekb_blk[b], h)).astype(jnp.bfloat16).T],
                                     axis=0), v_new[n])
               for n, (b, h) in enumerate(chains)]
        o = [ws_qs[n][CHUNK:] + okv[n][:CHUNK] for n in range(len(chains))]
        state = [state[n] * col(egl_blk[b], h) + okv[n][CHUNK:]
                 for n, (b, h) in enumerate(chains)]
        for n, (b, h) in enumerate(chains):
            o_s[rows[b], lanes[h]] = o[n]
    for n, (b, h) in enumerate(chains):
        s_ref[b * HEADS + h] = state[n]

    res = x + jnp.dot(y, wo_ref[GDN_WIDTH:, :], preferred_element_type=jnp.float32)
    gated = jnp.concatenate(
        [_rmsnorm(o_s[:, lanes[h]], gnw_ref[...]) * _silu(z_s[:, lanes[h]]) for h in heads],
        axis=1)

    res = res + _bdot(gated, wo_ref[:GDN_WIDTH, :])
    out_ref[...] = _rmsnorm(res, fnw_ref[...]).reshape(ROWS, TLR, D_MODEL)


def _pack_block(j, wt_ref, mini_ref, wp_s):
    cols = slice(j * PACK_TN, (j + 1) * PACK_TN)
    if j < COL_CONV // PACK_TN:
        wp_s[:, cols] = wt_ref[...].T.astype(jnp.bfloat16)
    elif j < COL_BETA // PACK_TN:
        rows = jnp.concatenate([wt_ref[N_BA:, :], mini_ref[...]], axis=0)
        wp_s[:, cols] = rows.T.astype(jnp.bfloat16)
    else:
        pad = jnp.zeros((LANES - HEADS, D_MODEL), jnp.float32)
        rows = jnp.concatenate([mini_ref[:HEADS, :], pad, mini_ref[HEADS:, :], pad], axis=0)
        wp_s[:, COL_BETA:] = rows.T.astype(jnp.bfloat16)


def _block_kernel(x_ref, wt_ref, mini_ref, wout_ref, nw_ref, cw_ref, alog_ref, dtb_ref, gnw_ref,
                  ccw_ref, cb_ref, fnw_ref, out_ref, wp_s, wo_s, *scratch, tiles):
    step = pl.program_id(0)
    for j in range(PACK_STEPS):
        pl.when(step == j)(functools.partial(_pack_block, j, wt_ref, mini_ref, wp_s))
    for j in range(WOUT_STEPS):
        @pl.when(step == PACK_STEPS + j)
        def _(j=j):
            wo_s[j * WOUT_TM:(j + 1) * WOUT_TM, :] = wout_ref[...].astype(jnp.bfloat16)

    @pl.when(step >= PRE_STEPS)
    def _():
        _tile_body(lax.rem(step - PRE_STEPS, tiles) == 0, x_ref, nw_ref, wp_s, cw_ref, alog_ref,
                   dtb_ref, gnw_ref, ccw_ref, cb_ref, wo_s, fnw_ref, out_ref, *scratch)


def _block(x, wt, w_out, norm_w, conv_qkv_w, a_log_pad, dt_bias_pad, gdn_norm_w, conv_w, conv_b,
           out_norm_w):
    batch, seq, _ = x.shape
    tiles = seq // TLR
    n_chunks = TL // CHUNK
    conv_tiles = CONV_WIDTH // LANES
    last_pack = PACK_STEPS - 1
    mini_per_block = PACK_TN // N_BA

    def tile(i):
        j = jnp.maximum(i - PRE_STEPS, 0)
        return (j // tiles, j % tiles, 0)

    mini = lambda i: (jnp.where(i >= last_pack, COL_CONV // N_BA,
                                (i + 1) * mini_per_block), 0)
    const = lambda i: (0, 0)
    resident = lambda shape: pl.BlockSpec(shape, const, pipeline_mode=pl.Buffered(1))
    tile_f32 = pltpu.VMEM((TL, GDN_WIDTH), jnp.float32)
    head_tile_f32 = pltpu.VMEM((HEADS, TL, HEAD_DIM), jnp.float32)
    return pl.pallas_call(
        functools.partial(_block_kernel, tiles=tiles),
        grid=(PRE_STEPS + (batch // ROWS) * tiles,),
        in_specs=[
            pl.BlockSpec((ROWS, TLR, D_MODEL), tile),
            pl.BlockSpec((PACK_TN, D_MODEL), lambda i: (jnp.minimum(i, last_pack - 1), 0)),
            pl.BlockSpec((N_BA, D_MODEL), mini),
            pl.BlockSpec((WOUT_TM, D_MODEL),
                         lambda i: (jnp.clip(i - PACK_STEPS, 0, WOUT_STEPS - 1), 0)),
            resident((1, D_MODEL)),
            resident((GDN_CONV, 3 * GDN_WIDTH)),
            resident((1, LANES)),
            resident((1, LANES)),
            resident((1, HEAD_DIM)),
            resident((SHORT_CONV, CONV_WIDTH)),
            resident((1, CONV_WIDTH)),
            resident((1, D_MODEL)),
        ],
        out_specs=pl.BlockSpec((ROWS, TLR, D_MODEL), tile),
        out_shape=jax.ShapeDtypeStruct(x.shape, jnp.float32),
        scratch_shapes=[
            pltpu.VMEM((D_MODEL, PROJ_PACKED), jnp.bfloat16),
            pltpu.VMEM((GDN_WIDTH + CONV_WIDTH, D_MODEL), jnp.bfloat16),
            pltpu.VMEM((ROWS * HEADS, HEAD_DIM, HEAD_DIM), jnp.float32),
            pltpu.VMEM((3, ROWS * HEADS, HALO + TLR, LANES), jnp.float32),
            pltpu.VMEM((ROWS * conv_tiles, HALO + TLR, LANES), jnp.float32),
            pltpu.VMEM((conv_tiles, TL, LANES), jnp.float32),
            head_tile_f32, head_tile_f32, head_tile_f32,
            tile_f32,
            pltpu.VMEM((TL, LANES), jnp.float32),
            pltpu.VMEM((TL, LANES), jnp.float32),
            pltpu.VMEM((n_chunks, LANES, 2 * CHUNK), jnp.float32),
            pltpu.VMEM((n_chunks, LANES, 2 * CHUNK), jnp.float32),
            tile_f32,
        ],
        compiler_params=pltpu.CompilerParams(
            dimension_semantics=("arbitrary",), vmem_limit_bytes=VMEM_LIMIT),
        name="hybrid_block",
    )(x, wt, wt, w_out, norm_w, conv_qkv_w, a_log_pad, dt_bias_pad, gdn_norm_w, conv_w, conv_b,
      out_norm_w)


def _pad_lanes(v):
    return jnp.pad(v, ((0, 0), (0, LANES - v.shape[-1])))


def _layer(x, norm_in_w, w_in, conv_qkv_w, a_log, dt_bias, gdn_norm_w, conv_w, conv_b, w_out,
           out_norm_w):
    assert w_in.shape[1] == COL_BETA + N_BA
    return _block(x, w_in.T, w_out, norm_in_w.reshape(1, -1), conv_qkv_w,
                  _pad_lanes(a_log.reshape(1, -1)), _pad_lanes(dt_bias.reshape(1, -1)),
                  gdn_norm_w.reshape(1, -1), conv_w, conv_b.reshape(1, -1),
                  out_norm_w.reshape(1, -1))


def kernel(x, norm_in_w, w_in, conv_qkv_w, A_log, dt_bias, gdn_norm_w, conv_w, conv_b, w_out,
           final_norm_w):
    batch, seq, d_model = x.shape
    assert norm_in_w.shape[0] == 1 and d_model == D_MODEL
    assert batch % ROWS == 0 and seq % TLR == 0 and TLR % CHUNK == 0
    return _layer(x, norm_in_w[0], w_in[0], conv_qkv_w[0], A_log[0], dt_bias[0], gdn_norm_w[0],
                  conv_w[0], conv_b[0], w_out[0], final_norm_w)
```

```python
import functools

import jax
import jax.numpy as jnp
from jax import lax
from jax.experimental import pallas as pl
from jax.experimental.pallas import tpu as pltpu

D_MODEL = 1024
HEADS = 8
HEAD_DIM = 128
GDN_WIDTH = HEADS * HEAD_DIM
CONV_WIDTH = 1024
GDN_CONV = 4
SHORT_CONV = 3
CHUNK = 64
EPS = 1e-6
LANES = 128
HALO = 8
ROW_STRIDE = 4

COL_Z = 3 * GDN_WIDTH
COL_CONV = COL_Z + GDN_WIDTH
COL_BETA = COL_CONV + 4 * CONV_WIDTH
COL_A = COL_BETA + LANES
PROJ_PACKED = COL_A + LANES
N_BA = 2 * HEADS
PACK_TN = 1024
PACK_STEPS = COL_BETA // PACK_TN + 1
WOUT_TM = 512
WOUT_STEPS = (GDN_WIDTH + CONV_WIDTH) // WOUT_TM
PRE_STEPS = PACK_STEPS + WOUT_STEPS

ROWS = 4
TLR = 64
TL = ROWS * TLR
VMEM_LIMIT = 60 * 1024 * 1024


def _bdot(a, b):
    return jnp.dot(a.astype(jnp.bfloat16), b.astype(jnp.bfloat16),
                   preferred_element_type=jnp.float32)


def _bdot_nt(a, b):
    return lax.dot_general(a.astype(jnp.bfloat16), b.astype(jnp.bfloat16),
                           (((1,), (1,)), ((), ())), preferred_element_type=jnp.float32)


def _bdot_tn(a, b):
    return lax.dot_general(a.astype(jnp.bfloat16), b.astype(jnp.bfloat16),
                           (((0,), (0,)), ((), ())), preferred_element_type=jnp.float32)


def _silu(x):
    return x * jax.nn.sigmoid(x)


def _rmsnorm(x, w):
    ms = jnp.mean(x * x, axis=-1, keepdims=True)
    return x * lax.rsqrt(ms + EPS) * w


def _fill_slabs(slab_ref, x):
    n = x.shape[1] // LANES
    for b in range(ROWS):
        for j in range(n):
            slab_ref[b * n + j, HALO:, :] = x[b * TLR:(b + 1) * TLR, j * LANES:(j + 1) * LANES]


def _conv_row_classes(slab_ref, j, w_ref, lane0, width):
    taps = [w_ref[width - 1 - s:width - s, lane0:lane0 + LANES] for s in range(width)]
    loaded = {}

    def rows_from(start):
        if start not in loaded:
            loaded[start] = slab_ref[j, pl.ds(start, TLR // ROW_STRIDE, stride=ROW_STRIDE), :]
        return loaded[start]

    out = []
    for r in range(ROW_STRIDE):
        acc = rows_from(HALO + r) * taps[0]
        for s in range(1, width):
            acc = acc + rows_from(HALO + r - s) * taps[s]
        out.append(acc)
    slab_ref[j, :HALO, :] = slab_ref[j, TLR:, :]
    return out


def _store_row_class(dst_ref, j, b, r, val):
    dst_ref[j, pl.ds(b * TLR + r, TLR // ROW_STRIDE, stride=ROW_STRIDE), :] = val


def _tile_body(first_tile, x_ref, nw_ref, wp_ref, cw_ref, alog_ref, dtb_ref, gnw_ref, ccw_ref,
               cb_ref, wo_ref, fnw_ref, out_ref,
               s_ref, pre_slab, ch_slab, cv_s, q_s, k_s, v_s, z_s, beta_s, gc_s, gct_s, betat_s,
               o_s):
    n_chunks = TL // CHUNK
    row_chunks = TLR // CHUNK
    heads = range(HEADS)
    lanes = [slice(h * HEAD_DIM, (h + 1) * HEAD_DIM) for h in heads]
    col = lambda blk, h: blk[:, h:h + 1]
    conv_tiles = CONV_WIDTH // LANES

    @pl.when(first_tile)
    def _():
        s_ref[...] = jnp.zeros_like(s_ref)
        pre_slab[:, :, :HALO, :] = jnp.zeros(pre_slab.shape[:2] + (HALO, LANES), jnp.float32)
        ch_slab[:, :HALO, :] = jnp.zeros((ch_slab.shape[0], HALO, LANES), jnp.float32)

    x = x_ref[...].reshape(TL, D_MODEL)
    hn = _rmsnorm(x, nw_ref[...]).astype(jnp.bfloat16)
    proj = lambda lo, hi: jnp.dot(hn, wp_ref[:, lo:hi], preferred_element_type=jnp.float32)
    conv_col = lambda j: proj(COL_CONV + j * CONV_WIDTH, COL_CONV + (j + 1) * CONV_WIDTH)

    ba = proj(COL_BETA, PROJ_PACKED)
    beta = jax.nn.sigmoid(ba[:, :LANES])
    beta_s[...] = beta
    a_in = ba[:, LANES:] + dtb_ref[...]
    softplus = jnp.maximum(a_in, 0.0) + jnp.log1p(jnp.exp(-jnp.abs(a_in)))
    gc = -jnp.exp(alog_ref[...]) * softplus
    pos = lax.broadcasted_iota(jnp.int32, gc.shape, 0) % CHUNK
    s = 1
    while s < CHUNK:
        gc = gc + jnp.where(pos >= s, pltpu.roll(gc, s, axis=0), 0.0)
        s *= 2
    gc_s[...] = gc
    for src, dst in ((gc.T, gct_s), (beta.T, betat_s)):
        for c in range(n_chunks):
            blk = src[:, c * CHUNK:(c + 1) * CHUNK]
            dst[c] = jnp.concatenate([blk, blk], axis=1)

    conv_in = []
    for grp, dst in enumerate((q_s, k_s, v_s)):
        col0 = grp * GDN_WIDTH
        slabs = pre_slab.at[grp]
        _fill_slabs(slabs, proj(col0, col0 + GDN_WIDTH))
        conv_in.append(conv_col(grp))
        scale = HEAD_DIM ** -0.5 if grp == 0 else 1.0
        for b in range(ROWS):
            for h in heads:
                classes = _conv_row_classes(slabs, b * HEADS + h, cw_ref, col0 + h * HEAD_DIM,
                                            GDN_CONV)
                for r, acc in enumerate(classes):
                    act = _silu(acc)
                    if grp < 2:
                        inv = lax.rsqrt(jnp.sum(act * act, axis=-1, keepdims=True) + EPS)
                        act = act * inv * scale
                    _store_row_class(dst, h, b, r, act)
    z_s[...] = proj(COL_Z, COL_CONV)
    gate_c = conv_col(3)

    gate_b, gate_cc, h_c = conv_in
    _fill_slabs(ch_slab, gate_cc * h_c)
    for b in range(ROWS):
        for j in range(conv_tiles):
            classes = _conv_row_classes(ch_slab, b * conv_tiles + j, ccw_ref, j * LANES,
                                        SHORT_CONV)
            for r, acc in enumerate(classes):
                _store_row_class(cv_s, j, b, r, acc)
    conv = jnp.concatenate([cv_s[j] for j in range(conv_tiles)], axis=1)
    y = (gate_b * (conv + cb_ref[...]) * _silu(gate_c)).astype(jnp.bfloat16)

    ri = lax.broadcasted_iota(jnp.int32, (CHUNK, 2 * CHUNK), 0)
    ci = lax.broadcasted_iota(jnp.int32, (CHUNK, 2 * CHUNK), 1)
    cj = ci % CHUNK
    causal2 = ri >= cj
    left = ci < CHUNK
    eye_right = jnp.where((ri == cj) & (ci >= CHUNK), 1.0, 0.0)
    keep_right = jnp.where(left, 0.0, 1.0)
    keep_left = jnp.where(left, 1.0, 0.0)
    off_diag = jnp.where(ri == cj, 0.0, 1.0)

    items = []
    for c in range(n_chunks):
        rows = slice(c * CHUNK, (c + 1) * CHUNK)
        gc_blk = gc_s[rows, :]
        eg_blk = jnp.exp(gc_blk)
        gct_blk = gct_s[c]
        betat_blk = betat_s[c]
        for h in heads:
            items.append((rows, h, betat_blk[h:h + 1, :], col(gc_blk, h), col(eg_blk, h),
                          gct_blk[h:h + 1, :]))
    n_items = range(len(items))
    k = [k_s[h, rows, :] for rows, h, *_ in items]
    q = [q_s[h, rows, :] for rows, h, *_ in items]
    decayb = [jnp.where(causal2, jnp.exp(it[3] - it[5]), 0.0) * it[2] for it in items]
    kq = [_bdot_nt(jnp.concatenate([k[n], q[n]], axis=0),
                   jnp.concatenate([k[n], k[n]], axis=0)) for n in n_items]
    attn_d = [kq[n][CHUNK:, :CHUNK] * decayb[n][:, :CHUNK] for n in n_items]
    qd = [q[n] * items[n][4] for n in n_items]
    assert row_chunks == 1
    state = [s_ref[n] for n in n_items]
    ks_qs = [_bdot(jnp.concatenate([k[n] * items[n][4], qd[n]], axis=0), state[n])
             for n in n_items]
    a2 = [kq[n][:CHUNK] * decayb[n] * off_diag for n in n_items]
    prod = [_bdot(a[:, :CHUNK], a) for a in a2]
    pt = [prod[n] * keep_left + (eye_right - a2[n] * keep_right) for n in n_items]
    m = 4
    while m < CHUNK:
        prod = [_bdot(p[:, :CHUNK], p) for p in pt]
        pt = [prod[n] + pt[n] * keep_right for n in n_items]
        m *= 2
    prod = [_bdot(p[:, :CHUNK], p) for p in pt]
    t_pad = [pt[n] + prod[n] for n in n_items]
    rhs = [v_s[it[1], it[0], :] - ks_qs[n][:CHUNK] for n, it in enumerate(items)]
    no_rows = jnp.zeros((CHUNK, HEAD_DIM), jnp.float32)
    v_new = [_bdot(t_pad[n], jnp.concatenate([no_rows, rhs[n]], axis=0))
             for n in n_items]

    chains = [(b, h) for b in range(ROWS) for h in heads]
    for step in range(row_chunks):
        item = [(b * row_chunks + step) * HEADS + h for b, h in chains]
        rows = [slice((b * row_chunks + step) * CHUNK, (b * row_chunks + step + 1) * CHUNK)
                for b in range(ROWS)]
        gc_blk = [gc_s[r, :] for r in rows]
        g_last = [g[CHUNK - 1:CHUNK, :] for g in gc_blk]
        ekb_blk = [jnp.exp(g_last[b] - gc_blk[b]) * beta_s[rows[b], :] for b in range(ROWS)]
        egl_blk = [jnp.exp(g) for g in g_last]
        o = [ks_qs[n][CHUNK:] + _bdot(attn_d[item[n]], v_new[n]) for n in range(len(chains))]
        kv = [_bdot_tn(k[item[n]] * col(ekb_blk[b], h), v_new[n])
              for n, (b, h) in enumerate(chains)]
        state = [state[n] * col(egl_blk[b], h) + kv[n] for n, (b, h) in enumerate(chains)]
        for n, (b, h) in enumerate(chains):
            o_s[rows[b], lanes[h]] = o[n]
    for n, (b, h) in enumerate(chains):
        s_ref[b * HEADS + h] = state[n]

    res = x + jnp.dot(y, wo_ref[GDN_WIDTH:, :], preferred_element_type=jnp.float32)
    gated = jnp.concatenate(
        [_rmsnorm(o_s[:, lanes[h]], gnw_ref[...]) * _silu(z_s[:, lanes[h]]) for h in heads],
        axis=1)

    res = res + _bdot(gated, wo_ref[:GDN_WIDTH, :])
    out_ref[...] = _rmsnorm(res, fnw_ref[...]).reshape(ROWS, TLR, D_MODEL)


def _pack_block(j, wt_ref, mini_ref, wp_s):
    cols = slice(j * PACK_TN, (j + 1) * PACK_TN)
    if j < COL_CONV // PACK_TN:
        wp_s[:, cols] = wt_ref[...].T.astype(jnp.bfloat16)
    elif j < COL_BETA // PACK_TN:
        rows = jnp.concatenate([wt_ref[N_BA:, :], mini_ref[...]], axis=0)
        wp_s[:, cols] = rows.T.astype(jnp.bfloat16)
    else:
        pad = jnp.zeros((LANES - HEADS, D_MODEL), jnp.float32)
        rows = jnp.concatenate([mini_ref[:HEADS, :], pad, mini_ref[HEADS:, :], pad], axis=0)
        wp_s[:, COL_BETA:] = rows.T.astype(jnp.bfloat16)


def _block_kernel(x_ref, wt_ref, mini_ref, wout_ref, nw_ref, cw_ref, alog_ref, dtb_ref, gnw_ref,
                  ccw_ref, cb_ref, fnw_ref, out_ref, wp_s, wo_s, *scratch, tiles):
    step = pl.program_id(0)
    for j in range(PACK_STEPS):
        pl.when(step == j)(functools.partial(_pack_block, j, wt_ref, mini_ref, wp_s))
    for j in range(WOUT_STEPS):
        @pl.when(step == PACK_STEPS + j)
        def _(j=j):
            wo_s[j * WOUT_TM:(j + 1) * WOUT_TM, :] = wout_ref[...].astype(jnp.bfloat16)

    @pl.when(step >= PRE_STEPS)
    def _():
        _tile_body(lax.rem(step - PRE_STEPS, tiles) == 0, x_ref, nw_ref, wp_s, cw_ref, alog_ref,
                   dtb_ref, gnw_ref, ccw_ref, cb_ref, wo_s, fnw_ref, out_ref, *scratch)


def _block(x, wt, w_out, norm_w, conv_qkv_w, a_log_pad, dt_bias_pad, gdn_norm_w, conv_w, conv_b,
           out_norm_w):
    batch, seq, _ = x.shape
    tiles = seq // TLR
    n_chunks = TL // CHUNK
    conv_tiles = CONV_WIDTH // LANES
    last_pack = PACK_STEPS - 1
    mini_per_block = PACK_TN // N_BA

    def tile(i):
        j = jnp.maximum(i - PRE_STEPS, 0)
        return (j // tiles, j % tiles, 0)

    mini = lambda i: (jnp.where(i >= last_pack, COL_CONV // N_BA,
                                (i + 1) * mini_per_block), 0)
    const = lambda i: (0, 0)
    resident = lambda shape: pl.BlockSpec(shape, const, pipeline_mode=pl.Buffered(1))
    tile_f32 = pltpu.VMEM((TL, GDN_WIDTH), jnp.float32)
    head_tile_f32 = pltpu.VMEM((HEADS, TL, HEAD_DIM), jnp.float32)
    return pl.pallas_call(
        functools.partial(_block_kernel, tiles=tiles),
        grid=(PRE_STEPS + (batch // ROWS) * tiles,),
        in_specs=[
            pl.BlockSpec((ROWS, TLR, D_MODEL), tile),
            pl.BlockSpec((PACK_TN, D_MODEL), lambda i: (jnp.minimum(i, last_pack - 1), 0)),
            pl.BlockSpec((N_BA, D_MODEL), mini),
            pl.BlockSpec((WOUT_TM, D_MODEL),
                         lambda i: (jnp.clip(i - PACK_STEPS, 0, WOUT_STEPS - 1), 0)),
            resident((1, D_MODEL)),
            resident((GDN_CONV, 3 * GDN_WIDTH)),
            resident((1, LANES)),
            resident((1, LANES)),
            resident((1, HEAD_DIM)),
            resident((SHORT_CONV, CONV_WIDTH)),
            resident((1, CONV_WIDTH)),
            resident((1, D_MODEL)),
        ],
        out_specs=pl.BlockSpec((ROWS, TLR, D_MODEL), tile),
        out_shape=jax.ShapeDtypeStruct(x.shape, jnp.float32),
        scratch_shapes=[
            pltpu.VMEM((D_MODEL, PROJ_PACKED), jnp.bfloat16),
            pltpu.VMEM((GDN_WIDTH + CONV_WIDTH, D_MODEL), jnp.bfloat16),
            pltpu.VMEM((ROWS * HEADS, HEAD_DIM, HEAD_DIM), jnp.float32),
            pltpu.VMEM((3, ROWS * HEADS, HALO + TLR, LANES), jnp.float32),
            pltpu.VMEM((ROWS * conv_tiles, HALO + TLR, LANES), jnp.float32),
            pltpu.VMEM((conv_tiles, TL, LANES), jnp.float32),
            head_tile_f32, head_tile_f32, head_tile_f32,
            tile_f32,
            pltpu.VMEM((TL, LANES), jnp.float32),
            pltpu.VMEM((TL, LANES), jnp.float32),
            pltpu.VMEM((n_chunks, LANES, 2 * CHUNK), jnp.float32),
            pltpu.VMEM((n_chunks, LANES, 2 * CHUNK), jnp.float32),
            tile_f32,
        ],
        compiler_params=pltpu.CompilerParams(
            dimension_semantics=("arbitrary",), vmem_limit_bytes=VMEM_LIMIT),
        name="hybrid_block",
    )(x, wt, wt, w_out, norm_w, conv_qkv_w, a_log_pad, dt_bias_pad, gdn_norm_w, conv_w, conv_b,
      out_norm_w)


def _pad_lanes(v):
    return jnp.pad(v, ((0, 0), (0, LANES - v.shape[-1])))


def _layer(x, norm_in_w, w_in, conv_qkv_w, a_log, dt_bias, gdn_norm_w, conv_w, conv_b, w_out,
           out_norm_w):
    assert w_in.shape[1] == COL_BETA + N_BA
    return _block(x, w_in.T, w_out, norm_in_w.reshape(1, -1), conv_qkv_w,
                  _pad_lanes(a_log.reshape(1, -1)), _pad_lanes(dt_bias.reshape(1, -1)),
                  gdn_norm_w.reshape(1, -1), conv_w, conv_b.reshape(1, -1),
                  out_norm_w.reshape(1, -1))


def kernel(x, norm_in_w, w_in, conv_qkv_w, A_log, dt_bias, gdn_norm_w, conv_w, conv_b, w_out,
           final_norm_w):
    batch, seq, d_model = x.shape
    assert norm_in_w.shape[0] == 1 and d_model == D_MODEL
    assert batch % ROWS == 0 and seq % TLR == 0 and TLR % CHUNK == 0
    return _layer(x, norm_in_w[0], w_in[0], conv_qkv_w[0], A_log[0], dt_bias[0], gdn_norm_w[0],
                  conv_w[0], conv_b[0], w_out[0], final_norm_w)
```
